```python
import math
import jax
import jax.numpy as jnp
from jax import lax
import numpy as np

D_MODEL = 1024
BATCH = 1
SEQ = 16384
DEPTH = 2

N_BIAS_HEADS = 8
N_BUCKETS = 32
MAX_DISTANCE = 128
Q_BLOCK = 128
NEG = -1e30
EPS = 1e-6

N_HEADS_A = 8
HEAD_DIM_A = 64
MOBA_BLOCK = 256
MOBA_TOPK = 3

LRU_WIDTH = D_MODEL // 2
LRU_BLOCKS = 8
LRU_BLOCK = LRU_WIDTH // LRU_BLOCKS
CONV_WIDTH = 4
LRU_C = 8.0

N_HEADS_C = 8
HEAD_DIM_C = D_MODEL // N_HEADS_C
N_KV_GROUPS = 2
HEADS_PER_GROUP = N_HEADS_C // N_KV_GROUPS
CMP_LEN = 32
CMP_STRIDE = 16
CMP_HIDDEN = 256
SEL_BLOCK = 64
SEL_TOPK = 16
WINDOW = 512
SEL_FORCE = 1e4

D_FF = 4 * D_MODEL

N_EVEN = (DEPTH + 1) // 2
N_ODD = DEPTH // 2

AB_IN = 3 * N_HEADS_A * HEAD_DIM_A + 2 * LRU_WIDTH
AB_OUT = N_HEADS_A * HEAD_DIM_A + LRU_WIDTH
C_KV = N_KV_GROUPS * HEAD_DIM_C
C_IN = N_HEADS_C * HEAD_DIM_C + 6 * C_KV + 3 * N_HEADS_C
C_OUT = N_HEADS_C * HEAD_DIM_C

kernel_name = 'hybrid_moba_rglru_nsa_trunk'


def rmsnorm(x, g):
    xf = x.astype(jnp.float32)
    y = xf * lax.rsqrt(jnp.mean(xf * xf, axis=-1, keepdims=True) + EPS)
    return (y * g.astype(jnp.float32)).astype(x.dtype)


def masked_softmax(s, mask):
    s = jnp.where(mask, s.astype(jnp.float32), NEG)
    e = jnp.exp(s - jnp.max(s, axis=-1, keepdims=True)) * mask
    return e / jnp.maximum(jnp.sum(e, axis=-1, keepdims=True), 1e-30)


def rel_bucket(dist):
    n = jnp.maximum(dist, 0)
    exact = N_BUCKETS // 2
    nf = jnp.maximum(n, 1).astype(jnp.float32)
    large = exact + (jnp.log(nf / exact) / math.log(MAX_DISTANCE / exact)
                     * (N_BUCKETS - exact)).astype(jnp.int32)
    large = jnp.minimum(large, N_BUCKETS - 1)
    return jnp.where(n < exact, n, large)


def bias_shared(table, bucket):
    return table.T[:, bucket].astype(jnp.float32)


def bias_per_head(table, bucket):
    return jax.vmap(lambda tb, bk: tb[bk])(table.T, bucket).astype(jnp.float32)


def bias_per_group(table, bucket):
    tg = table.reshape(N_BUCKETS, N_KV_GROUPS, HEADS_PER_GROUP).transpose(1, 2, 0)
    return jax.vmap(lambda tb, bk: tb[:, bk])(tg, bucket).astype(jnp.float32)


def moba_attention(q, k, v, rel_bias):
    T, H, dh = q.shape
    scale = dh ** -0.5
    nb = -(-T // MOBA_BLOCK)
    pad = nb * MOBA_BLOCK - T
    q = q.transpose(1, 0, 2)
    kp = jnp.pad(k.transpose(1, 0, 2), ((0, 0), (0, pad), (0, 0)))
    vp = jnp.pad(v.transpose(1, 0, 2), ((0, 0), (0, pad), (0, 0)))
    kb = kp.reshape(H, nb, MOBA_BLOCK, dh)
    vb = vp.reshape(H, nb, MOBA_BLOCK, dh)
    kmean = jnp.mean(kb.astype(jnp.float32), axis=2)
    n_sel = min(MOBA_TOPK, nb)
    n_g = n_sel * MOBA_BLOCK
    blk = jnp.arange(nb)
    offs = jnp.arange(MOBA_BLOCK)

    def one_block(qi):
        start = qi * Q_BLOCK
        t = start + jnp.arange(Q_BLOCK)
        own = start // MOBA_BLOCK
        qb = lax.dynamic_slice_in_dim(q, start, Q_BLOCK, axis=1)
        route = jnp.einsum('hqd,hnd->hqn', qb.astype(jnp.float32), kmean)
        route = jnp.where((blk < own)[None, None, :], route, NEG)
        rval, idx = lax.top_k(route, n_sel)
        valid = rval > NEG / 2
        kg = jax.vmap(lambda kh, ih: kh[ih])(kb, idx)
        vg = jax.vmap(lambda vh, ih: vh[ih])(vb, idx)
        pos_s = idx[..., None] * MOBA_BLOCK + offs
        s_sel = jnp.einsum('hqd,hqnbd->hqnb', qb, kg,
                           preferred_element_type=jnp.float32) * scale
        s_sel = s_sel + bias_per_head(rel_bias, rel_bucket(t[None, :, None, None] - pos_s))
        m_sel = jnp.broadcast_to(valid[..., None], s_sel.shape)
        k_own = lax.dynamic_slice_in_dim(kp, own * MOBA_BLOCK, MOBA_BLOCK, axis=1)
        v_own = lax.dynamic_slice_in_dim(vp, own * MOBA_BLOCK, MOBA_BLOCK, axis=1)
        d_own = t[:, None] - (own * MOBA_BLOCK + offs)[None, :]
        s_own = jnp.einsum('hqd,hbd->hqb', qb, k_own,
                           preferred_element_type=jnp.float32) * scale
        s_own = s_own + bias_shared(rel_bias, rel_bucket(d_own))
        m_own = jnp.broadcast_to((d_own >= 0)[None], s_own.shape)
        s = jnp.concatenate([s_sel.reshape(H, Q_BLOCK, n_g), s_own], axis=-1)
        m = jnp.concatenate([m_sel.reshape(H, Q_BLOCK, n_g), m_own], axis=-1)
        p = masked_softmax(s, m).astype(v.dtype)
        return (jnp.einsum('hqn,hqnd->hqd', p[..., :n_g], vg.reshape(H, Q_BLOCK, n_g, dh))
                + jnp.einsum('hqb,hbd->hqd', p[..., n_g:], v_own))

    out = lax.map(one_block, jnp.arange(T // Q_BLOCK))
    return out.transpose(0, 2, 1, 3).reshape(T, H * dh)


def rglru(xr, conv_w, conv_b, w_rg, b_rg, w_ig, b_ig, lru_lambda):
    T, W = xr.shape
    xp = jnp.pad(xr, ((CONV_WIDTH - 1, 0), (0, 0)))
    xc = conv_b + sum(xp[j:j + T] * conv_w[j] for j in range(CONV_WIDTH))
    xblk = xc.reshape(T, LRU_BLOCKS, LRU_BLOCK)
    r = jax.nn.sigmoid((jnp.einsum('tbi,bij->tbj', xblk, w_rg).reshape(T, W) + b_rg)
                       .astype(jnp.float32))
    i = jax.nn.sigmoid((jnp.einsum('tbi,bij->tbj', xblk, w_ig).reshape(T, W) + b_ig)
                       .astype(jnp.float32))
    log_a = -LRU_C * r * jax.nn.softplus(-lru_lambda.astype(jnp.float32))
    a = jnp.exp(log_a)
    b = jnp.sqrt(-jnp.expm1(2.0 * log_a)) * i * xc.astype(jnp.float32)

    def combine(c1, c2):
        a1, b1 = c1
        a2, b2 = c2
        return a1 * a2, a2 * b1 + b2

    _, h = lax.associative_scan(combine, (a, b), axis=0)
    return h.astype(xr.dtype)


def compress(kt, pos_emb, w1, w2):
    T, G, dh = kt.shape
    nc = (T - CMP_LEN) // CMP_STRIDE + 1
    idx = jnp.arange(nc)[:, None] * CMP_STRIDE + jnp.arange(CMP_LEN)[None, :]
    blocks = kt[idx] + pos_emb[None, :, None, :]
    flat = blocks.transpose(0, 2, 1, 3).reshape(nc, G, CMP_LEN * dh)
    return jax.nn.gelu(flat @ w1) @ w2


def nsa_attention(q, kc_t, vc_t, ks_t, vs_t, kw_t, vw_t, g_logit,
                  cmp_pos_k, cmp_pos_v, cmp_k_w1, cmp_k_w2, cmp_v_w1, cmp_v_w2, rel_bias):
    T = q.shape[0]
    G, P, dh = N_KV_GROUPS, HEADS_PER_GROUP, HEAD_DIM_C
    scale = dh ** -0.5
    qg = q.reshape(T, G, P, dh).transpose(1, 2, 0, 3)
    kc = compress(kc_t, cmp_pos_k, cmp_k_w1, cmp_k_w2).transpose(1, 0, 2)
    vc = compress(vc_t, cmp_pos_v, cmp_v_w1, cmp_v_w2).transpose(1, 0, 2)
    nc = kc.shape[1]
    cmp_start = jnp.arange(nc) * CMP_STRIDE
    cmp_end = cmp_start + CMP_LEN - 1
    ns = T // SEL_BLOCK
    sblk = jnp.arange(ns)
    sel_start = sblk * SEL_BLOCK
    overlap = ((cmp_start[:, None] < sel_start[None, :] + SEL_BLOCK)
               & (cmp_start[:, None] + CMP_LEN > sel_start[None, :])).astype(jnp.float32)
    ks_b = ks_t.transpose(1, 0, 2).reshape(G, ns, SEL_BLOCK, dh)
    vs_b = vs_t.transpose(1, 0, 2).reshape(G, ns, SEL_BLOCK, dh)
    kw_p = jnp.pad(kw_t.transpose(1, 0, 2), ((0, 0), (WINDOW, 0), (0, 0)))
    vw_p = jnp.pad(vw_t.transpose(1, 0, 2), ((0, 0), (WINDOW, 0), (0, 0)))
    gates = jax.nn.sigmoid(g_logit.astype(jnp.float32)).reshape(T, G, P, 3).transpose(1, 2, 0, 3)
    n_sel = min(SEL_TOPK, ns)
    n_g = n_sel * SEL_BLOCK
    offs_s = jnp.arange(SEL_BLOCK)
    n_w = WINDOW + Q_BLOCK

    def one_block(qi):
        start = qi * Q_BLOCK
        t = start + jnp.arange(Q_BLOCK)
        qb = lax.dynamic_slice_in_dim(qg, start, Q_BLOCK, axis=2)
        d_c = t[:, None] - cmp_end[None, :]
        s_c = jnp.einsum('gpqd,gnd->gpqn', qb, kc, preferred_element_type=jnp.float32) * scale
        s_c = s_c + bias_shared(rel_bias, rel_bucket(d_c)).reshape(G, P, Q_BLOCK, nc)
        p_c = masked_softmax(s_c, d_c >= 0)
        o_c = jnp.einsum('gpqn,gnd->gpqd', p_c.astype(vc.dtype), vc)
        imp = jnp.einsum('gpqn,nj->gqj', p_c, overlap)
        cur = t // SEL_BLOCK
        forced = ((sblk[None, :] == 0) | (sblk[None, :] == cur[:, None])
                  | (sblk[None, :] == cur[:, None] - 1))
        imp = jnp.where(forced[None], imp + SEL_FORCE, imp)
        imp = jnp.where((sblk[None, :] > cur[:, None])[None], NEG, imp)
        sval, idx = lax.top_k(imp, n_sel)
        valid = sval > NEG / 2
        kg = jax.vmap(lambda kb_, ib: kb_[ib])(ks_b, idx)
        vg = jax.vmap(lambda vb_, ib: vb_[ib])(vs_b, idx)
        pos_s = idx[..., None] * SEL_BLOCK + offs_s
        d_s = t[None, :, None, None] - pos_s
        s_s = jnp.einsum('gpqd,gqnbd->gpqnb', qb, kg, preferred_element_type=jnp.float32) * scale
        s_s = s_s + bias_per_group(rel_bias, rel_bucket(d_s))
        m_s = (valid[..., None] & (d_s >= 0))[:, None]
        p_s = masked_softmax(s_s.reshape(G, P, Q_BLOCK, n_g), m_s.reshape(G, 1, Q_BLOCK, n_g))
        o_s = jnp.einsum('gpqn,gqnd->gpqd', p_s.astype(vs_b.dtype),
                         vg.reshape(G, Q_BLOCK, n_g, dh))
        kwb = lax.dynamic_slice_in_dim(kw_p, start, n_w, axis=1)
        vwb = lax.dynamic_slice_in_dim(vw_p, start, n_w, axis=1)
        pos_w = start - WINDOW + jnp.arange(n_w)
        d_w = t[:, None] - pos_w[None, :]
        m_w = (pos_w[None, :] >= 0) & (d_w >= 0) & (d_w < WINDOW)
        s_w = jnp.einsum('gpqd,gkd->gpqk', qb, kwb, preferred_element_type=jnp.float32) * scale
        s_w = s_w + bias_shared(rel_bias, rel_bucket(d_w)).reshape(G, P, Q_BLOCK, n_w)
        p_w = masked_softmax(s_w, m_w)
        o_w = jnp.einsum('gpqk,gkd->gpqd', p_w.astype(vwb.dtype), vwb)
        gb = lax.dynamic_slice_in_dim(gates, start, Q_BLOCK, axis=2)
        return (gb[..., 0:1] * o_c + gb[..., 1:2] * o_s + gb[..., 2:3] * o_w).astype(q.dtype)

    out = lax.map(one_block, jnp.arange(T // Q_BLOCK))
    return out.transpose(0, 3, 1, 2, 4).reshape(T, G * P * dh)


def mixer_ab(h, w_in, conv_w, conv_b, w_rg, b_rg, w_ig, b_ig, lru_lambda, w_out, rel_bias):
    Bt, T, _ = h.shape
    dA = N_HEADS_A * HEAD_DIM_A
    proj = h @ w_in
    q, k, v, xr, gr = jnp.split(proj, [dA, 2 * dA, 3 * dA, 3 * dA + LRU_WIDTH], axis=-1)
    shp = (Bt, T, N_HEADS_A, HEAD_DIM_A)
    o_a = jax.vmap(moba_attention, in_axes=(0, 0, 0, None))(
        q.reshape(shp), k.reshape(shp), v.reshape(shp), rel_bias)
    o_b = jax.vmap(rglru, in_axes=(0, None, None, None, None, None, None, None))(
        xr, conv_w, conv_b, w_rg, b_rg, w_ig, b_ig, lru_lambda)
    o_b = o_b * jax.nn.gelu(gr)
    return jnp.concatenate([o_a, o_b], axis=-1) @ w_out


def mixer_c(h, w_in, cmp_pos_k, cmp_pos_v, cmp_k_w1, cmp_k_w2, cmp_v_w1, cmp_v_w2,
            w_out, rel_bias):
    Bt, T, _ = h.shape
    dq = N_HEADS_C * HEAD_DIM_C
    proj = h @ w_in
    pieces = jnp.split(proj, [dq + j * C_KV for j in range(7)], axis=-1)
    q = pieces[0].reshape(Bt, T, N_HEADS_C, HEAD_DIM_C)
    kv = [p.reshape(Bt, T, N_KV_GROUPS, HEAD_DIM_C) for p in pieces[1:7]]
    gl = pieces[7].reshape(Bt, T, N_HEADS_C, 3)
    o = jax.vmap(nsa_attention, in_axes=(0,) * 8 + (None,) * 7)(
        q, kv[0], kv[1], kv[2], kv[3], kv[4], kv[5], gl,
        cmp_pos_k, cmp_pos_v, cmp_k_w1, cmp_k_w2, cmp_v_w1, cmp_v_w2, rel_bias)
    return o @ w_out


def setup_inputs(seed: int = 0) -> dict:
    key = jax.random.key(seed)
    ks = jax.random.split(key, 26)
    f32 = jnp.float32

    def nrm(k, shape, fan_in):
        return jax.random.normal(k, shape, f32) * fan_in ** -0.5

    def gain(k, shape):
        return 1.0 + 0.05 * jax.random.normal(k, shape, f32)

    u = jax.random.uniform(ks[8], (N_EVEN, LRU_WIDTH), f32, 0.9, 0.999)
    a = u ** (1.0 / LRU_C)
    lam = jnp.log(a) - jnp.log1p(-a)
    return {
        'x': jax.random.normal(ks[0], (BATCH, SEQ, D_MODEL), f32),
        'rel_bias': 0.3 * jax.random.normal(ks[1], (N_BUCKETS, N_BIAS_HEADS), f32),
        'ab_w_in': nrm(ks[2], (N_EVEN, D_MODEL, AB_IN), D_MODEL),
        'ab_conv_w': nrm(ks[3], (N_EVEN, CONV_WIDTH, LRU_WIDTH), CONV_WIDTH),
        'ab_conv_b': 0.02 * jax.random.normal(ks[4], (N_EVEN, LRU_WIDTH), f32),
        'ab_w_rg': nrm(ks[5], (N_EVEN, LRU_BLOCKS, LRU_BLOCK, LRU_BLOCK), LRU_BLOCK),
        'ab_b_rg': 0.1 * jax.random.normal(ks[6], (N_EVEN, LRU_WIDTH), f32),
        'ab_w_ig': nrm(ks[7], (N_EVEN, LRU_BLOCKS, LRU_BLOCK, LRU_BLOCK), LRU_BLOCK),
        'ab_b_ig': 0.1 * jax.random.normal(ks[9], (N_EVEN, LRU_WIDTH), f32),
        'ab_lru_lambda': lam,
        'ab_w_out': nrm(ks[10], (N_EVEN, AB_OUT, D_MODEL), AB_OUT),
        'c_w_in': nrm(ks[11], (N_ODD, D_MODEL, C_IN), D_MODEL),
        'c_cmp_pos_k': 0.1 * jax.random.normal(ks[12], (N_ODD, CMP_LEN, HEAD_DIM_C), f32),
        'c_cmp_pos_v': 0.1 * jax.random.normal(ks[13], (N_ODD, CMP_LEN, HEAD_DIM_C), f32),
        'c_cmp_k_w1': nrm(ks[14], (N_ODD, CMP_LEN * HEAD_DIM_C, CMP_HIDDEN), CMP_LEN * HEAD_DIM_C),
        'c_cmp_k_w2': nrm(ks[15], (N_ODD, CMP_HIDDEN, HEAD_DIM_C), CMP_HIDDEN),
        'c_cmp_v_w1': nrm(ks[16], (N_ODD, CMP_LEN * HEAD_DIM_C, CMP_HIDDEN), CMP_LEN * HEAD_DIM_C),
        'c_cmp_v_w2': nrm(ks[17], (N_ODD, CMP_HIDDEN, HEAD_DIM_C), CMP_HIDDEN),
        'c_w_out': nrm(ks[18], (N_ODD, C_OUT, D_MODEL), C_OUT),
        'norm_mix_pre': gain(ks[19], (DEPTH, D_MODEL)),
        'norm_mix_post': gain(ks[20], (DEPTH, D_MODEL)),
        'norm_mlp_pre': gain(ks[21], (DEPTH, D_MODEL)),
        'norm_mlp_post': gain(ks[22], (DEPTH, D_MODEL)),
        'w_up': nrm(ks[23], (DEPTH, D_MODEL, D_FF), D_MODEL),
        'w_down': nrm(ks[24], (DEPTH, D_FF, D_MODEL), D_FF),
    }


def reference(x, rel_bias,
              ab_w_in, ab_conv_w, ab_conv_b, ab_w_rg, ab_b_rg, ab_w_ig, ab_b_ig,
              ab_lru_lambda, ab_w_out,
              c_w_in, c_cmp_pos_k, c_cmp_pos_v, c_cmp_k_w1, c_cmp_k_w2, c_cmp_v_w1,
              c_cmp_v_w2, c_w_out,
              norm_mix_pre, norm_mix_post, norm_mlp_pre, norm_mlp_post, w_up, w_down):
    for layer in range(DEPTH):
        i = layer // 2
        h = rmsnorm(x, norm_mix_pre[layer])
        if layer % 2 == 0:
            m = mixer_ab(h, ab_w_in[i], ab_conv_w[i], ab_conv_b[i], ab_w_rg[i], ab_b_rg[i],
                         ab_w_ig[i], ab_b_ig[i], ab_lru_lambda[i], ab_w_out[i], rel_bias)
        else:
            m = mixer_c(h, c_w_in[i], c_cmp_pos_k[i], c_cmp_pos_v[i], c_cmp_k_w1[i],
                        c_cmp_k_w2[i], c_cmp_v_w1[i], c_cmp_v_w2[i], c_w_out[i], rel_bias)
        x = x + rmsnorm(m, norm_mix_post[layer])
        h = rmsnorm(x, norm_mlp_pre[layer])
        f = jnp.square(jax.nn.relu(h @ w_up[layer])) @ w_down[layer]
        x = x + rmsnorm(f, norm_mlp_post[layer])
    return x
```

```python
import functools
import math

import numpy as np
import jax
import jax.numpy as jnp
from jax import lax
from jax.experimental import pallas as pl
from jax.experimental.pallas import tpu as pltpu

F32 = jnp.float32
BF16 = jnp.bfloat16
HIGHEST = lax.Precision.HIGHEST

D = 1024
EPS = 1e-6
NEG = -1e30
N_BUCKETS = 32
MAX_DISTANCE = 128

HA = 8
DHA = 64
MOBA_BLOCK = 256
MOBA_TOPK = 3
LRU_W = 512
LRU_C = 8.0
CONV_W = 4
HC = 8
DHC = 128
GC = 2
PC = 4
CMP_LEN = 32
CMP_STRIDE = 16
SEL_BLOCK = 64
SEL_TOPK = 16
WINDOW = 512
SEL_FORCE = 1e4
D_FF = 4096

TILE = 256
VMEM_LIMIT = 56 * 1024 * 1024


def _bucket_np(d):
    n = np.maximum(d, 0)
    exact = N_BUCKETS // 2
    nf = np.maximum(n, 1).astype(np.float32)
    large = exact + (np.log(nf / np.float32(exact)) / np.float32(math.log(MAX_DISTANCE / exact))
                     * np.float32(N_BUCKETS - exact)).astype(np.int32)
    return np.where(n < exact, n, np.minimum(large, N_BUCKETS - 1))


_THR = [int(np.argmax(_bucket_np(np.arange(0, 4 * MAX_DISTANCE)) >= k)) for k in range(1, N_BUCKETS)]
assert _THR[-1] <= MAX_DISTANCE


def _bias_delta(d, tab_ref, h):
    acc = jnp.full(d.shape, tab_ref[0, h] - tab_ref[N_BUCKETS - 1, h], F32)
    for k in range(1, N_BUCKETS):
        acc = acc + jnp.where(d >= _THR[k - 1], tab_ref[k, h] - tab_ref[k - 1, h], 0.0)
    return acc


def _dot_t(a, b, **kw):
    return lax.dot_general(a, b, (((1,), (1,)), ((), ())), preferred_element_type=F32, **kw)


def _gelu_tanh(x):
    return 0.5 * x * (1.0 + jnp.tanh(0.7978845608028654 * (x + 0.044715 * x * x * x)))


def _params(*sem):
    return pltpu.CompilerParams(dimension_semantics=sem, vmem_limit_bytes=VMEM_LIMIT)


def _smem():
    return pl.BlockSpec(memory_space=pltpu.SMEM)


def _proj_kernel(x_ref, g_ref, w_ref, *out_refs, segs, tm):
    x = x_ref[...]
    ms = jnp.mean(x * x, axis=-1, keepdims=True)
    hn = (x * lax.rsqrt(ms + EPS) * g_ref[...]).astype(BF16)
    row0 = pl.program_id(0) * tm
    for o_ref, (start, width, aug) in zip(out_refs, segs):
        y = jnp.dot(hn, w_ref[:, start:start + width], preferred_element_type=F32)
        if aug is not None:
            group, rows_per_block, lo_first = aug
            half = group // 2
            row = row0 + lax.broadcasted_iota(jnp.int32, y.shape, 0)
            lane = lax.broadcasted_iota(jnp.int32, y.shape, 1)
            which_half = (lane // half) % 2
            data_half = (lane // group) % 2 if lo_first else jnp.zeros_like(lane)
            blk = (row // rows_per_block) % half
            y = jnp.where((which_half != data_half) & (lane % half == blk), 1.0, y)
        o_ref[...] = y.astype(o_ref.dtype)


def _rms_proj(x, g, w, segs, dtypes, tm=512):
    t = x.shape[0]
    return pl.pallas_call(
        functools.partial(_proj_kernel, segs=segs, tm=tm),
        grid=(t // tm,),
        in_specs=[pl.BlockSpec((tm, D), lambda i: (i, 0)),
                  pl.BlockSpec((1, D), lambda i: (0, 0)),
                  pl.BlockSpec(w.shape, lambda i: (0, 0))],
        out_specs=[pl.BlockSpec((tm, wd), lambda i: (i, 0)) for (_, wd, _) in segs],
        out_shape=[jax.ShapeDtypeStruct((t, wd), dt) for (_, wd, _), dt in zip(segs, dtypes)],
        compiler_params=_params("arbitrary"),
        name="rms_proj",
    )(x, g.reshape(1, D), w)


def _blockmean_kernel(k_ref, o_ref):
    k = k_ref[...]
    o_ref[...] = jnp.mean(k.reshape(8, MOBA_BLOCK, k.shape[-1]), axis=1)


def _block_mean(k):
    t, w = k.shape
    nb = t // MOBA_BLOCK
    return pl.pallas_call(
        _blockmean_kernel,
        grid=(nb // 8,),
        in_specs=[pl.BlockSpec((8 * MOBA_BLOCK, w), lambda i: (i, 0))],
        out_specs=pl.BlockSpec((8, w), lambda i: (i, 0)),
        out_shape=jax.ShapeDtypeStruct((nb, w), F32),
        compiler_params=_params("arbitrary"),
        name="moba_block_mean",
    )(k)


def _topk_mask(r, lane, k):
    sel = jnp.zeros(r.shape, jnp.bool_)
    lanef = lane.astype(F32)
    for _ in range(k):
        mx = jnp.max(r, axis=1, keepdims=True)
        first = jnp.min(jnp.where(r == mx, lanef, 1e9), axis=1, keepdims=True)
        pick = lanef == first
        sel = sel | (pick & (mx > NEG / 2))
        r = jnp.where(pick, NEG, r)
    return sel


def _moba_kernel(tab_ref, q_ref, km_ref, ka_ref, v_ref, o_ref, f_ref, m_ref, l_ref, acc_ref):
    hp = pl.program_id(0)
    qi = pl.program_id(1)
    scale = DHA ** -0.5
    lane = lax.broadcasted_iota(jnp.int32, (TILE, 128), 1)

    @pl.when(qi == 0)
    def _():
        d0 = (lax.broadcasted_iota(jnp.int32, (TILE, TILE), 0)
              - lax.broadcasted_iota(jnp.int32, (TILE, TILE), 1))
        for hh in range(2):
            h = 2 * hp + hh
            f_ref[hh, 0] = jnp.where(d0 >= 0, _bias_delta(jnp.maximum(d0, 0), tab_ref, h), NEG)
            f_ref[hh, 1] = _bias_delta(d0 + TILE, tab_ref, h)

    q = q_ref[...]
    outs = []
    for hh in range(2):
        head_lanes = (lane < DHA) if hh == 0 else (lane >= DHA)
        qh = jnp.where(head_lanes, q, 0.0)
        route = _dot_t(qh, km_ref[...], precision=HIGHEST)
        blk = lane - DHA if hh == 0 else lane
        in_aug = jnp.logical_not(head_lanes)
        r = jnp.where(in_aug & (blk < qi), route, NEG)
        sel = _topk_mask(r, lane, MOBA_TOPK)
        keep = sel | (in_aug & (blk == qi))
        qa = jnp.where(head_lanes, q * scale, jnp.where(keep, 0.0, NEG)).astype(BF16)

        def scores(j, hh=hh, qa=qa):
            off = pl.multiple_of(j * TILE, TILE)
            kt = ka_ref[pl.ds(off, TILE), hh * 128:(hh + 1) * 128]
            return _dot_t(qa, kt), v_ref[pl.ds(off, TILE), :]

        s, vt = scores(qi)
        s = s + f_ref[hh, 0]
        m = jnp.max(s, axis=1, keepdims=True)
        p = jnp.exp(s - m)
        m_ref[hh] = m
        l_ref[hh] = jnp.sum(p, axis=1, keepdims=True)
        acc_ref[hh] = jnp.dot(p.astype(BF16), vt, preferred_element_type=F32)

        def step(j, bias, hh=hh, scores=scores):
            s, vt = scores(j)
            if bias is not None:
                s = s + bias
            m_old = m_ref[hh]
            m_new = jnp.maximum(m_old, jnp.max(s, axis=1, keepdims=True))
            alpha = jnp.exp(m_old - m_new)
            p = jnp.exp(s - m_new)
            l_ref[hh] = alpha * l_ref[hh] + jnp.sum(p, axis=1, keepdims=True)
            acc_ref[hh] = alpha * acc_ref[hh] + jnp.dot(p.astype(BF16), vt, preferred_element_type=F32)
            m_ref[hh] = m_new

        @pl.when(qi >= 1)
        def _(hh=hh, step=step):
            step(qi - 1, f_ref[hh, 1])

        def far(j, c, step=step):
            step(j, None)
            return c

        lax.fori_loop(0, jnp.maximum(qi - 1, 0), far, 0)
        outs.append(acc_ref[hh] / l_ref[hh])

    o_ref[...] = jnp.where(lane < DHA, outs[0], outs[1])


def _moba(tab, q, kmean2, kaug, v):
    t = q.shape[0]
    return pl.pallas_call(
        _moba_kernel,
        grid=(HA // 2, t // TILE),
        in_specs=[_smem(),
                  pl.BlockSpec((TILE, 128), lambda hp, qi: (qi, hp)),
                  pl.BlockSpec((128, 128), lambda hp, qi: (0, hp)),
                  pl.BlockSpec((t, 256), lambda hp, qi: (0, hp)),
                  pl.BlockSpec((t, 128), lambda hp, qi: (0, hp))],
        out_specs=pl.BlockSpec((TILE, 128), lambda hp, qi: (qi, hp)),
        out_shape=jax.ShapeDtypeStruct((t, HA * DHA), F32),
        scratch_shapes=[pltpu.VMEM((2, 2, TILE, TILE), F32),
                        pltpu.VMEM((2, TILE, 1), F32),
                        pltpu.VMEM((2, TILE, 1), F32),
                        pltpu.VMEM((2, TILE, 128), F32)],
        compiler_params=_params("arbitrary", "arbitrary"),
        name="moba_attention",
    )(tab, q, kmean2, kaug, v)


def _rglru_kernel(xr_ref, gr_ref, cw_ref, cb_ref, wrg_ref, brg_ref, wig_ref, big_ref, lam_ref,
                  o_ref, xbuf, a_s, b_s, hcar, *, tt):
    @pl.when(pl.program_id(0) == 0)
    def _():
        xbuf[0:8, :] = jnp.zeros((8, LRU_W), F32)
        hcar[...] = jnp.zeros_like(hcar)

    xbuf[8:8 + tt, :] = xr_ref[...]
    xc = cb_ref[...]
    for j in range(CONV_W):
        xc = xc + xbuf[8 - (CONV_W - 1) + j:8 - (CONV_W - 1) + j + tt, :] * cw_ref[j:j + 1, :]
    xbuf[0:8, :] = xbuf[tt:tt + 8, :]

    xcb = xc.astype(BF16)
    r = jax.nn.sigmoid(jnp.dot(xcb, wrg_ref[...], preferred_element_type=F32) + brg_ref[...])
    ig = jax.nn.sigmoid(jnp.dot(xcb, wig_ref[...], preferred_element_type=F32) + big_ref[...])
    z = -lam_ref[...]
    softplus = jnp.maximum(z, 0.0) + jnp.log(1.0 + jnp.exp(-jnp.abs(z)))
    a = jnp.exp(-LRU_C * r * softplus)
    a_s[...] = a
    b_s[...] = jnp.sqrt(1.0 - a * a) * ig * xc

    row = lax.broadcasted_iota(jnp.int32, (8, LRU_W), 0)

    def body(gi, h):
        off = pl.multiple_of(gi * 8, 8)
        a8 = a_s[pl.ds(off, 8), :]
        b8 = b_s[pl.ds(off, 8), :]
        for s in (1, 2, 4):
            a_sh = jnp.where(row >= s, pltpu.roll(a8, s, 0), 1.0)
            b_sh = jnp.where(row >= s, pltpu.roll(b8, s, 0), 0.0)
            b8 = a8 * b_sh + b8
            a8 = a8 * a_sh
        h8 = b8 + a8 * h
        b_s[pl.ds(off, 8), :] = h8
        return h8[7:8, :]

    hcar[0:1, :] = lax.fori_loop(0, tt // 8, body, hcar[0:1, :])
    o_ref[...] = b_s[...] * _gelu_tanh(gr_ref[...])


def _rglru(xr, gr, conv_w, conv_b, wrg, b_rg, wig, b_ig, lam, tt=512):
    t = xr.shape[0]
    row = lambda a: a.reshape(1, LRU_W)
    full = lambda shape: pl.BlockSpec(shape, lambda i: (0, 0))
    return pl.pallas_call(
        functools.partial(_rglru_kernel, tt=tt),
        grid=(t // tt,),
        in_specs=[pl.BlockSpec((tt, LRU_W), lambda i: (i, 0)),
                  pl.BlockSpec((tt, LRU_W), lambda i: (i, 0)),
                  full((CONV_W, LRU_W)), full((1, LRU_W)),
                  full((LRU_W, LRU_W)), full((1, LRU_W)),
                  full((LRU_W, LRU_W)), full((1, LRU_W)), full((1, LRU_W))],
        out_specs=pl.BlockSpec((tt, LRU_W), lambda i: (i, 0)),
        out_shape=jax.ShapeDtypeStruct((t, LRU_W), F32),
        scratch_shapes=[pltpu.VMEM((tt + 8, LRU_W), F32),
                        pltpu.VMEM((tt, LRU_W), F32),
                        pltpu.VMEM((tt, LRU_W), F32),
                        pltpu.VMEM((8, LRU_W), F32)],
        compiler_params=_params("arbitrary"),
        name="rglru",
    )(xr, gr, conv_w, row(conv_b), wrg, row(b_rg), wig, row(b_ig), row(lam))


def _post(y, g_ref, x_ref, o_ref):
    ms = jnp.mean(y * y, axis=-1, keepdims=True)
    o_ref[...] = x_ref[...] + y * lax.rsqrt(ms + EPS) * g_ref[...]


def _outproj_ab_kernel(a1_ref, a2_ref, w1_ref, w2_ref, g_ref, x_ref, o_ref):
    y = (jnp.dot(a1_ref[...].astype(BF16), w1_ref[...], preferred_element_type=F32)
         + jnp.dot(a2_ref[...].astype(BF16), w2_ref[...], preferred_element_type=F32))
    _post(y, g_ref, x_ref, o_ref)


def _outproj_ab(a1, a2, w1, w2, g, x, tm=512):
    t = x.shape[0]
    rows = lambda w: pl.BlockSpec((tm, w), lambda i: (i, 0))
    full = lambda shape: pl.BlockSpec(shape, lambda i: (0, 0))
    return pl.pallas_call(
        _outproj_ab_kernel,
        grid=(t // tm,),
        in_specs=[rows(a1.shape[1]), rows(a2.shape[1]), full(w1.shape), full(w2.shape),
                  full((1, D)), rows(D)],
        out_specs=rows(D),
        out_shape=jax.ShapeDtypeStruct((t, D), F32),
        compiler_params=_params("arbitrary"),
        name="outproj_ab",
    )(a1, a2, w1, w2, g.reshape(1, D), x)


def _outproj_c_kernel(oc_ref, os_ref, ow_ref, gl_ref, w_ref, g_ref, x_ref, o_ref):
    gates = jax.nn.sigmoid(gl_ref[...])
    parts = []
    for h in range(HC):
        sl = slice(h * DHC, (h + 1) * DHC)
        parts.append(gates[:, 3 * h:3 * h + 1] * oc_ref[:, sl]
                     + gates[:, 3 * h + 1:3 * h + 2] * os_ref[:, sl]
                     + gates[:, 3 * h + 2:3 * h + 3] * ow_ref[:, sl])
    o = jnp.concatenate(parts, axis=1).astype(BF16)
    _post(jnp.dot(o, w_ref[...], preferred_element_type=F32), g_ref, x_ref, o_ref)


def _outproj_c(oc, os_, ow, gl, w, g, x, tm=512):
    t = x.shape[0]
    rows = lambda w_: pl.BlockSpec((tm, w_), lambda i: (i, 0))
    full = lambda shape: pl.BlockSpec(shape, lambda i: (0, 0))
    return pl.pallas_call(
        _outproj_c_kernel,
        grid=(t // tm,),
        in_specs=[rows(D), rows(D), rows(D), rows(128), full(w.shape), full((1, D)), rows(D)],
        out_specs=rows(D),
        out_shape=jax.ShapeDtypeStruct((t, D), F32),
        compiler_params=_params("arbitrary"),
        name="outproj_c",
    )(oc, os_, ow, gl, w, g.reshape(1, D), x)


def _mlp_kernel(x_ref, gpre_ref, wup_ref, wdn_ref, gpost_ref, o_ref, h_s, acc_s):
    c = pl.program_id(1)

    @pl.when(c == 0)
    def _():
        x = x_ref[...]
        ms = jnp.mean(x * x, axis=-1, keepdims=True)
        h_s[...] = (x * lax.rsqrt(ms + EPS) * gpre_ref[...]).astype(BF16)
        acc_s[...] = jnp.zeros_like(acc_s)

    u = jnp.maximum(jnp.dot(h_s[...], wup_ref[...], preferred_element_type=F32), 0.0)
    acc_s[...] += jnp.dot((u * u).astype(BF16), wdn_ref[...], preferred_element_type=F32)

    @pl.when(c == pl.num_programs(1) - 1)
    def _():
        _post(acc_s[...], gpost_ref, x_ref, o_ref)


def _mlp(x, gpre, wup, wdn, gpost, tm=1024, tf=512):
    t = x.shape[0]
    return pl.pallas_call(
        _mlp_kernel,
        grid=(t // tm, D_FF // tf),
        in_specs=[pl.BlockSpec((tm, D), lambda i, c: (i, 0)),
                  pl.BlockSpec((1, D), lambda i, c: (0, 0)),
                  pl.BlockSpec((D, tf), lambda i, c: (0, c)),
                  pl.BlockSpec((tf, D), lambda i, c: (c, 0)),
                  pl.BlockSpec((1, D), lambda i, c: (0, 0))],
        out_specs=pl.BlockSpec((tm, D), lambda i, c: (i, 0)),
        out_shape=jax.ShapeDtypeStruct((t, D), F32),
        scratch_shapes=[pltpu.VMEM((tm, D), BF16), pltpu.VMEM((tm, D), F32)],
        compiler_params=_params("arbitrary", "arbitrary"),
        name="mlp",
    )(x, gpre.reshape(1, D), wup, wdn, gpost.reshape(1, D))


def _compress_kernel(h_ref, pos_ref, w1_ref, w2_ref, o_ref):
    half = CMP_STRIDE * DHC
    hm = h_ref[0]
    top = jnp.dot((hm + pos_ref[:, 0:half]).astype(BF16), w1_ref[0:half, :], preferred_element_type=F32)
    bot = jnp.dot((hm + pos_ref[:, half:2 * half]).astype(BF16), w1_ref[half:2 * half, :],
                  preferred_element_type=F32)
    hid = top + pltpu.roll(bot, bot.shape[0] - 1, 0)
    o_ref[0] = jnp.dot(_gelu_tanh(hid).astype(BF16), w2_ref[...], preferred_element_type=F32).astype(o_ref.dtype)


def _compress(hmat, pos, w1, w2):
    g, n, wd = hmat.shape
    return pl.pallas_call(
        _compress_kernel,
        grid=(g,),
        in_specs=[pl.BlockSpec((1, n, wd), lambda i: (i, 0, 0)),
                  pl.BlockSpec((1, 2 * wd), lambda i: (0, 0)),
                  pl.BlockSpec(w1.shape, lambda i: (0, 0)),
                  pl.BlockSpec(w2.shape, lambda i: (0, 0))],
        out_specs=pl.BlockSpec((1, n, DHC), lambda i: (i, 0, 0)),
        out_shape=jax.ShapeDtypeStruct((g, n, DHC), BF16),
        compiler_params=_params("arbitrary"),
        name="nsa_compress",
    )(hmat, pos.reshape(1, 2 * wd), w1, w2)


def _stack_heads(q):
    return jnp.concatenate([q[:, p * DHC:(p + 1) * DHC] for p in range(PC)], axis=0)


def _nsa_cmp_kernel(tab_ref, q_ref, kc_ref, vc_ref, ov_ref, oc_ref, sb_ref, g_ref, *, tq, ns):
    g = pl.program_id(0)
    qi = pl.program_id(1)
    ncp = kc_ref.shape[1]
    scale = DHC ** -0.5
    start = qi * tq
    ii = lax.broadcasted_iota(jnp.int32, (tq, ncp), 0)
    nn = lax.broadcasted_iota(jnp.int32, (tq, ncp), 1)

    @pl.when(qi == 0)
    def _():
        dg = ii - (CMP_LEN - 1) + CMP_STRIDE * (ncp - tq // CMP_STRIDE) - CMP_STRIDE * nn
        dg = jnp.maximum(dg, 0)
        for p in range(PC):
            g_ref[p] = _bias_delta(dg, tab_ref, PC * g + p)

    d = start + ii - CMP_STRIDE * nn - (CMP_LEN - 1)
    valid = d >= 0
    q4 = (_stack_heads(q_ref[...]) * scale).astype(BF16)
    s_all = _dot_t(q4, kc_ref[0])
    shift = (qi + 1) * (tq // CMP_STRIDE)
    psum = jnp.zeros((tq, ncp), F32)
    for p in range(PC):
        s = s_all[p * tq:(p + 1) * tq] + pltpu.roll(g_ref[p], shift, 1)
        s = jnp.where(valid, s, NEG)
        m = jnp.max(s, axis=1, keepdims=True)
        e = jnp.where(valid, jnp.exp(s - m), 0.0)
        pc = e / jnp.maximum(jnp.sum(e, axis=1, keepdims=True), 1e-30)
        oc_ref[:, p * DHC:(p + 1) * DHC] = jnp.dot(pc.astype(BF16), vc_ref[0], preferred_element_type=F32)
        psum = psum + pc

    imp = jnp.dot(psum, ov_ref[...], precision=HIGHEST, preferred_element_type=F32)
    nsp = imp.shape[1]
    jj = lax.broadcasted_iota(jnp.int32, (tq, nsp), 1)
    cur = (start + lax.broadcasted_iota(jnp.int32, (tq, nsp), 0)) // SEL_BLOCK
    forced = (jj == 0) | (jj == cur) | (jj == cur - 1)
    imp = jnp.where(forced, imp + SEL_FORCE, imp)
    imp = jnp.where((jj > cur) | (jj >= ns), NEG, imp)
    sel = _topk_mask(imp, jj, min(SEL_TOPK, ns))
    sb_ref[0] = jnp.where(sel, 0.0, NEG).astype(BF16)


def _nsa_cmp(tab, q, kc, vc, overlap, ns, tq=128):
    t = q.shape[0]
    ncp = kc.shape[1]
    nsp = overlap.shape[1]
    return pl.pallas_call(
        functools.partial(_nsa_cmp_kernel, tq=tq, ns=ns),
        grid=(GC, t // tq),
        in_specs=[_smem(),
                  pl.BlockSpec((tq, PC * DHC), lambda g, qi: (qi, g)),
                  pl.BlockSpec((1, ncp, DHC), lambda g, qi: (g, 0, 0)),
                  pl.BlockSpec((1, ncp, DHC), lambda g, qi: (g, 0, 0)),
                  pl.BlockSpec((ncp, nsp), lambda g, qi: (0, 0))],
        out_specs=[pl.BlockSpec((tq, PC * DHC), lambda g, qi: (qi, g)),
                   pl.BlockSpec((1, tq, nsp), lambda g, qi: (g, qi, 0))],
        out_shape=[jax.ShapeDtypeStruct((t, HC * DHC), F32),
                   jax.ShapeDtypeStruct((GC, t, nsp), BF16)],
        scratch_shapes=[pltpu.VMEM((PC, tq, ncp), F32)],
        compiler_params=_params("arbitrary", "arbitrary"),
        name="nsa_compressed_topk",
    )(tab, q, kc, vc, overlap)


def _nsa_sel_kernel(tab_ref, q_ref, sb_ref, ka_ref, v_ref, o_ref, f_ref, m_ref, l_ref, acc_ref, *, nhalf):
    g = pl.program_id(0)
    qi = pl.program_id(1)
    scale = DHC ** -0.5
    tiles_per_half = 128 * SEL_BLOCK // TILE

    @pl.when(qi == 0)
    def _():
        d0 = (lax.broadcasted_iota(jnp.int32, (TILE, TILE), 0)
              - lax.broadcasted_iota(jnp.int32, (TILE, TILE), 1))
        for p in range(PC):
            h = PC * g + p
            f_ref[p, 0] = jnp.where(d0 >= 0, _bias_delta(jnp.maximum(d0, 0), tab_ref, h), NEG)
            f_ref[p, 1] = _bias_delta(d0 + TILE, tab_ref, h)

    q4 = (_stack_heads(q_ref[...]) * scale).astype(BF16)
    sb = sb_ref[0]
    qas = [jnp.concatenate([q4, jnp.concatenate([sb[:, hf * 128:(hf + 1) * 128]] * PC, axis=0)], axis=1)
           for hf in range(nhalf)]

    def tile(j, qa):
        off = pl.multiple_of(j * TILE, TILE)
        return _dot_t(qa, ka_ref[pl.ds(off, TILE), :]), v_ref[pl.ds(off, TILE), :]

    def qa_for(j):
        if nhalf == 1:
            return qas[0]
        return jnp.where(j >= tiles_per_half, qas[1], qas[0])

    def add_bias(s, which):
        return jnp.concatenate([s[p * TILE:(p + 1) * TILE] + f_ref[p, which] for p in range(PC)], axis=0)

    s, vt = tile(qi, qa_for(qi))
    s = add_bias(s, 0)
    m = jnp.max(s, axis=1, keepdims=True)
    p = jnp.exp(s - m)
    m_ref[...] = m
    l_ref[...] = jnp.sum(p, axis=1, keepdims=True)
    acc_ref[...] = jnp.dot(p.astype(BF16), vt, preferred_element_type=F32)

    def step(j, qa, which):
        s, vt = tile(j, qa)
        if which is not None:
            s = add_bias(s, which)
        m_old = m_ref[...]
        m_new = jnp.maximum(m_old, jnp.max(s, axis=1, keepdims=True))
        alpha = jnp.exp(m_old - m_new)
        p = jnp.exp(s - m_new)
        l_ref[...] = alpha * l_ref[...] + jnp.sum(p, axis=1, keepdims=True)
        acc_ref[...] = alpha * acc_ref[...] + jnp.dot(p.astype(BF16), vt, preferred_element_type=F32)
        m_ref[...] = m_new

    @pl.when(qi >= 1)
    def _():
        step(qi - 1, qa_for(qi - 1), 1)

    nfar = jnp.maximum(qi - 1, 0)
    for hf in range(nhalf):
        def far(j, c, hf=hf):
            step(j, qas[hf], None)
            return c
        lo = hf * tiles_per_half
        hi = jnp.minimum(nfar, (hf + 1) * tiles_per_half) if hf + 1 < nhalf else nfar
        lax.fori_loop(lo, jnp.maximum(hi, lo), far, 0)

    o = acc_ref[...] / l_ref[...]
    for p in range(PC):
        o_ref[:, p * DHC:(p + 1) * DHC] = o[p * TILE:(p + 1) * TILE]


def _nsa_sel(tab, q, selbias, kaug, v):
    t = q.shape[0]
    nsp = selbias.shape[2]
    return pl.pallas_call(
        functools.partial(_nsa_sel_kernel, nhalf=nsp // 128),
        grid=(GC, t // TILE),
        in_specs=[_smem(),
                  pl.BlockSpec((TILE, PC * DHC), lambda g, qi: (qi, g)),
                  pl.BlockSpec((1, TILE, nsp), lambda g, qi: (g, qi, 0)),
                  pl.BlockSpec((t, 2 * DHC), lambda g, qi: (0, g)),
                  pl.BlockSpec((t, DHC), lambda g, qi: (0, g))],
        out_specs=pl.BlockSpec((TILE, PC * DHC), lambda g, qi: (qi, g)),
        out_shape=jax.ShapeDtypeStruct((t, HC * DHC), F32),
        scratch_shapes=[pltpu.VMEM((PC, 2, TILE, TILE), F32),
                        pltpu.VMEM((PC * TILE, 1), F32),
                        pltpu.VMEM((PC * TILE, 1), F32),
                        pltpu.VMEM((PC * TILE, DHC), F32)],
        compiler_params=_params("arbitrary", "arbitrary"),
        name="nsa_selected",
    )(tab, q, selbias, kaug, v)


def _nsa_win_kernel(tab_ref, q_ref, k_ref, v_ref, o_ref, f_ref, *, tq):
    g = pl.program_id(0)
    qi = pl.program_id(1)
    scale = DHC ** -0.5
    nw = WINDOW + tq
    ii = lax.broadcasted_iota(jnp.int32, (tq, nw), 0)
    jj = lax.broadcasted_iota(jnp.int32, (tq, nw), 1)

    @pl.when(qi == 0)
    def _():
        d = WINDOW + ii - jj
        ok = (d >= 0) & (d < WINDOW)
        for p in range(PC):
            f_ref[p] = jnp.where(ok, _bias_delta(jnp.maximum(d, 0), tab_ref, PC * g + p), NEG)

    start = pl.multiple_of(qi * tq, tq)
    pos_ok = jnp.where(jj >= WINDOW - start, 0.0, NEG)
    q4 = (_stack_heads(q_ref[...]) * scale).astype(BF16)
    s_all = _dot_t(q4, k_ref[pl.ds(start, nw), :])
    vt = v_ref[pl.ds(start, nw), :]
    for p in range(PC):
        s = s_all[p * tq:(p + 1) * tq] + f_ref[p] + pos_ok
        m = jnp.max(s, axis=1, keepdims=True)
        e = jnp.exp(s - m)
        pw = e / jnp.sum(e, axis=1, keepdims=True)
        o_ref[:, p * DHC:(p + 1) * DHC] = jnp.dot(pw.astype(BF16), vt, preferred_element_type=F32)


def _nsa_win(tab, q, kwp, vwp, tq=128):
    t = q.shape[0]
    tp = kwp.shape[0]
    return pl.pallas_call(
        functools.partial(_nsa_win_kernel, tq=tq),
        grid=(GC, t // tq),
        in_specs=[_smem(),
                  pl.BlockSpec((tq, PC * DHC), lambda g, qi: (qi, g)),
                  pl.BlockSpec((tp, DHC), lambda g, qi: (0, g)),
                  pl.BlockSpec((tp, DHC), lambda g, qi: (0, g))],
        out_specs=pl.BlockSpec((tq, PC * DHC), lambda g, qi: (qi, g)),
        out_shape=jax.ShapeDtypeStruct((t, HC * DHC), F32),
        scratch_shapes=[pltpu.VMEM((PC, tq, WINDOW + tq), F32)],
        compiler_params=_params("arbitrary", "arbitrary"),
        name="nsa_window",
    )(tab, q, kwp, vwp)


def _layer_ab(x, tab, w_in, conv_w, conv_b, w_rg, b_rg, w_ig, b_ig, lam, w_out, g_pre, g_post):
    t = x.shape[0]
    da = HA * DHA
    wk = w_in[:, da:2 * da].reshape(D, HA, DHA)
    zeros = jnp.zeros_like(wk)
    even = (jnp.arange(HA) % 2 == 0)[None, :, None]
    wk_aug = jnp.concatenate([jnp.where(even, wk, zeros), jnp.where(even, zeros, wk)], axis=-1)
    w = jnp.concatenate([w_in, wk_aug.reshape(D, HA * 128)], axis=1).astype(BF16)
    segs = [(0, da, None), (da, da, None), (2 * da, da, None), (3 * da, LRU_W, None),
            (3 * da + LRU_W, LRU_W, None), (3 * da + 2 * LRU_W, HA * 128, (128, MOBA_BLOCK, True))]
    q, k, v, xr, gr, kaug = _rms_proj(x, g_pre, w, segs, [F32, F32, BF16, F32, F32, BF16])

    nb = t // MOBA_BLOCK
    kmean = jnp.pad(_block_mean(k), ((0, DHA - nb), (0, 0)))
    o_a = _moba(tab, q, jnp.concatenate([kmean, kmean], axis=0), kaug, v)

    eye = jnp.eye(w_rg.shape[0], dtype=F32)
    blockdiag = lambda wb: jnp.einsum('bij,bc->bicj', wb, eye).reshape(LRU_W, LRU_W).astype(BF16)
    o_b = _rglru(xr, gr, conv_w, conv_b, blockdiag(w_rg), b_rg, blockdiag(w_ig), b_ig, lam)
    wo = w_out.astype(BF16)
    return _outproj_ab(o_a, o_b, wo[:da], wo[da:], g_post, x)


def _layer_c(x, tab, w_in, pos_k, pos_v, k_w1, k_w2, v_w1, v_w2, w_out, g_pre, g_post):
    t = x.shape[0]
    dq = HC * DHC
    ckv = GC * DHC
    cols = lambda j: w_in[:, dq + j * ckv: dq + (j + 1) * ckv]
    wks = cols(2).reshape(D, GC, DHC)
    wks_aug = jnp.concatenate([wks, jnp.zeros_like(wks)], axis=-1).reshape(D, GC * 2 * DHC)
    wg = jnp.pad(w_in[:, dq + 6 * ckv:], ((0, 0), (0, 128 - 3 * HC)))
    w = jnp.concatenate([w_in[:, :dq], cols(0), cols(1), wks_aug, cols(3), cols(4), cols(5), wg],
                        axis=1).astype(BF16)
    segs, off = [], 0
    for wd, aug in [(dq, None), (DHC, None), (DHC, None), (DHC, None), (DHC, None),
                    (GC * 2 * DHC, (2 * DHC, SEL_BLOCK, False)), (ckv, None), (ckv, None), (ckv, None),
                    (128, None)]:
        segs.append((off, wd, aug))
        off += wd
    q, kc0, kc1, vc0, vc1, ks_aug, vs, kw, vw, gl = _rms_proj(
        x, g_pre, w, segs, [F32, F32, F32, F32, F32, BF16, BF16, BF16, BF16, F32])

    n = t // CMP_STRIDE
    hm = lambda a0, a1: jnp.stack([a0.reshape(n, CMP_STRIDE * DHC), a1.reshape(n, CMP_STRIDE * DHC)])
    kc = _compress(hm(kc0, kc1), pos_k, k_w1.astype(BF16), k_w2.astype(BF16))
    vc = _compress(hm(vc0, vc1), pos_v, v_w1.astype(BF16), v_w2.astype(BF16))

    ns = t // SEL_BLOCK
    nsp = -(-ns // 128) * 128
    cs = jnp.arange(n)[:, None] * CMP_STRIDE
    ss = jnp.arange(nsp)[None, :] * SEL_BLOCK
    overlap = ((cs < ss + SEL_BLOCK) & (cs + CMP_LEN > ss) & (jnp.arange(n)[:, None] < n - 1)
               & (jnp.arange(nsp)[None, :] < ns)).astype(F32)
    o_c, selbias = _nsa_cmp(tab, q, kc, vc, overlap, ns)
    o_s = _nsa_sel(tab, q, selbias, ks_aug, vs)
    padw = lambda a: jnp.pad(a, ((WINDOW, 0), (0, 0)))
    o_w = _nsa_win(tab, q, padw(kw), padw(vw))
    return _outproj_c(o_c, o_s, o_w, gl, w_out.astype(BF16), g_post, x)


def kernel(x, rel_bias, ab_w_in, ab_conv_w, ab_conv_b, ab_w_rg, ab_b_rg, ab_w_ig, ab_b_ig, ab_lru_lambda, ab_w_out, c_w_in, c_cmp_pos_k, c_cmp_pos_v, c_cmp_k_w1, c_cmp_k_w2, c_cmp_v_w1, c_cmp_v_w2, c_w_out, norm_mix_pre, norm_mix_post, norm_mlp_pre, norm_mlp_post, w_up, w_down):
    batch = x.shape[0]
    depth = norm_mix_pre.shape[0]
    outs = []
    for b in range(batch):
        xb = x[b]
        for layer in range(depth):
            i = layer // 2
            if layer % 2 == 0:
                xb = _layer_ab(xb, rel_bias, ab_w_in[i], ab_conv_w[i], ab_conv_b[i], ab_w_rg[i], ab_b_rg[i],
                               ab_w_ig[i], ab_b_ig[i], ab_lru_lambda[i], ab_w_out[i],
                               norm_mix_pre[layer], norm_mix_post[layer])
            else:
                xb = _layer_c(xb, rel_bias, c_w_in[i], c_cmp_pos_k[i], c_cmp_pos_v[i], c_cmp_k_w1[i],
                              c_cmp_k_w2[i], c_cmp_v_w1[i], c_cmp_v_w2[i], c_w_out[i],
                              norm_mix_pre[layer], norm_mix_post[layer])
            xb = _mlp(xb, norm_mlp_pre[layer], w_up[layer].astype(BF16), w_down[layer].astype(BF16),
                      norm_mlp_post[layer])
        outs.append(xb)
    return jnp.stack(outs)
```

```python
import functools
import math

import numpy as np
import jax
import jax.numpy as jnp
from jax import lax
from jax.experimental import pallas as pl
from jax.experimental.pallas import tpu as pltpu

F32 = jnp.float32
BF16 = jnp.bfloat16
HIGHEST = lax.Precision.HIGHEST
LOG2E = 1.4426950408889634

D = 1024
EPS = 1e-6
NEG = -1e30
N_BUCKETS = 32
MAX_DISTANCE = 128

HA = 8
DHA = 64
MOBA_BLOCK = 256
MOBA_TOPK = 3
LRU_W = 512
LRU_C = 8.0
CONV_W = 4
HC = 8
DHC = 128
GC = 2
PC = 4
CMP_LEN = 32
CMP_STRIDE = 16
SEL_BLOCK = 64
SEL_TOPK = 16
WINDOW = 512
SEL_FORCE = 1e4
D_FF = 4096

TILE = 256
MOBA_UNROLL = 8
SEL_UNROLL = 4
SEL_CHUNKS = 2
VMEM_LIMIT = 56 * 1024 * 1024


def _bucket_np(d):
    n = np.maximum(d, 0)
    exact = N_BUCKETS // 2
    nf = np.maximum(n, 1).astype(np.float32)
    large = exact + (np.log(nf / np.float32(exact)) / np.float32(math.log(MAX_DISTANCE / exact))
                     * np.float32(N_BUCKETS - exact)).astype(np.int32)
    return np.where(n < exact, n, np.minimum(large, N_BUCKETS - 1))


_THR = [int(np.argmax(_bucket_np(np.arange(0, 4 * MAX_DISTANCE)) >= k)) for k in range(1, N_BUCKETS)]
assert _THR[-1] <= MAX_DISTANCE


def _bias_delta(d, tab_ref, h):
    acc = jnp.full(d.shape, tab_ref[0, h] - tab_ref[N_BUCKETS - 1, h], F32)
    for k in range(1, N_BUCKETS):
        acc = acc + jnp.where(d >= _THR[k - 1], tab_ref[k, h] - tab_ref[k - 1, h], 0.0)
    return acc


def _dot_t(a, b, **kw):
    return lax.dot_general(a, b, (((1,), (1,)), ((), ())), preferred_element_type=F32, **kw)


def _gelu_tanh(x):
    return 0.5 * x * (1.0 + jnp.tanh(0.7978845608028654 * (x + 0.044715 * x * x * x)))


def _params(*sem):
    return pltpu.CompilerParams(dimension_semantics=sem, vmem_limit_bytes=VMEM_LIMIT)


def _smem():
    return pl.BlockSpec(memory_space=pltpu.SMEM)


def _proj_kernel(x_ref, g_ref, w_ref, *out_refs, segs, tm):
    x = x_ref[...]
    ms = jnp.mean(x * x, axis=-1, keepdims=True)
    hn = (x * lax.rsqrt(ms + EPS) * g_ref[...]).astype(BF16)
    row0 = pl.program_id(0) * tm
    for o_ref, (start, width, aug) in zip(out_refs, segs):
        y = jnp.dot(hn, w_ref[:, start:start + width], preferred_element_type=F32)
        if aug is not None:
            lane = lax.broadcasted_iota(jnp.int32, y.shape, 1)
            if aug[0] == "onehot":
                _, group, rows_per_block, lo_first = aug
                half = group // 2
                row = row0 + lax.broadcasted_iota(jnp.int32, y.shape, 0)
                which_half = (lane // half) % 2
                data_half = (lane // group) % 2 if lo_first else jnp.zeros_like(lane)
                blk = (row // rows_per_block) % half
                y = jnp.where((which_half != data_half) & (lane % half == blk), 1.0, y)
            else:
                _, group, col = aug
                y = jnp.where(lane % group == col, 1.0, y)
        o_ref[...] = y.astype(o_ref.dtype)


def _rms_proj(x, g, w, segs, dtypes, tm=512):
    t = x.shape[0]
    return pl.pallas_call(
        functools.partial(_proj_kernel, segs=segs, tm=tm),
        grid=(t // tm,),
        in_specs=[pl.BlockSpec((tm, D), lambda i: (i, 0)),
                  pl.BlockSpec((1, D), lambda i: (0, 0)),
                  pl.BlockSpec(w.shape, lambda i: (0, 0))],
        out_specs=[pl.BlockSpec((tm, wd), lambda i: (i, 0)) for (_, wd, _) in segs],
        out_shape=[jax.ShapeDtypeStruct((t, wd), dt) for (_, wd, _), dt in zip(segs, dtypes)],
        compiler_params=_params("arbitrary"),
        name="rms_proj",
    )(x, g.reshape(1, D), w)


def _segments(widths_augs):
    segs, off = [], 0
    for wd, aug in widths_augs:
        segs.append((off, wd, aug))
        off += wd
    return segs


def _topk_mask(r, lane, k):
    sel = jnp.zeros(r.shape, jnp.bool_)
    lanef = lane.astype(F32)
    for _ in range(k):
        mx = jnp.max(r, axis=1, keepdims=True)
        first = jnp.min(jnp.where(r == mx, lanef, 1e9), axis=1, keepdims=True)
        pick = lanef == first
        sel = sel | (pick & (mx > NEG / 2))
        r = jnp.where(pick, NEG, r)
    return sel


def _near_bias_tiles(f_ref, tab_ref, heads):
    d0 = (lax.broadcasted_iota(jnp.int32, (TILE, TILE), 0)
          - lax.broadcasted_iota(jnp.int32, (TILE, TILE), 1))
    for i, h in enumerate(heads):
        f_ref[i, 0] = jnp.where(d0 >= 0, LOG2E * _bias_delta(jnp.maximum(d0, 0), tab_ref, h), NEG)
        f_ref[i, 1] = LOG2E * _bias_delta(d0 + TILE, tab_ref, h)


def _slot_bias(j, qi, lo, f_diag, f_prev):
    other = jnp.where((j >= lo) & (j <= qi), 0.0, NEG)
    return jnp.where(j == qi, f_diag, jnp.where(j == qi - 1, f_prev, other))


def _flash_update(s, vt, m_ref, acc_ref, first):
    rm = jnp.max(s, axis=1, keepdims=True)
    if first:
        m_ref[...] = jnp.broadcast_to(rm, m_ref.shape)
        p = jnp.exp2(s - rm)
        acc_ref[...] = jnp.dot(p.astype(BF16), vt, preferred_element_type=F32)
        return
    m_old = m_ref[...]
    m_new = jnp.maximum(m_old, rm)
    alpha = jnp.exp2(m_old - m_new)
    p = jnp.exp2(s - jnp.concatenate([m_new] * (s.shape[1] // m_new.shape[1]), axis=1))
    pv = jnp.dot(p.astype(BF16), vt, preferred_element_type=F32)
    acc_ref[...] = jnp.concatenate([alpha] * (pv.shape[1] // alpha.shape[1]), axis=1) * acc_ref[...] + pv
    m_ref[...] = m_new


def _blockmean_kernel(k_ref, o_ref):
    k = k_ref[...]
    o_ref[...] = jnp.mean(k.reshape(8, MOBA_BLOCK, k.shape[-1]), axis=1)


def _block_mean(k):
    t, w = k.shape
    nb = t // MOBA_BLOCK
    return pl.pallas_call(
        _blockmean_kernel,
        grid=(nb // 8,),
        in_specs=[pl.BlockSpec((8 * MOBA_BLOCK, w), lambda i: (i, 0))],
        out_specs=pl.BlockSpec((8, w), lambda i: (i, 0)),
        out_shape=jax.ShapeDtypeStruct((nb, w), F32),
        compiler_params=_params("arbitrary"),
        name="moba_block_mean",
    )(k)


def _moba_kernel(tab_ref, q_ref, km_ref, ka_ref, va_ref, o_ref, f_ref, m_ref, acc_ref, *, nb, unroll):
    hp = pl.program_id(0)
    qi = pl.program_id(1)
    scale = DHA ** -0.5 * LOG2E
    lane = lax.broadcasted_iota(jnp.int32, (TILE, 128), 1)

    @pl.when(qi == 0)
    def _():
        _near_bias_tiles(f_ref, tab_ref, [2 * hp, 2 * hp + 1])

    q = q_ref[...]
    qas = []
    for hh in range(2):
        head_lanes = (lane < DHA) if hh == 0 else (lane >= DHA)
        qh = jnp.where(head_lanes, q, 0.0)
        route = _dot_t(qh, km_ref[...], precision=HIGHEST)
        blk = lane - DHA if hh == 0 else lane
        in_aug = jnp.logical_not(head_lanes)
        r = jnp.where(in_aug & (blk < qi), route, NEG)
        sel = _topk_mask(r, lane, MOBA_TOPK)
        keep = sel | (in_aug & (blk == qi))
        qas.append(jnp.where(head_lanes, q * scale, jnp.where(keep, 0.0, NEG)).astype(BF16))

    def tiles(j, hh, n):
        off = pl.multiple_of(j * TILE, TILE)
        sl = slice(hh * 128, (hh + 1) * 128)
        return _dot_t(qas[hh], ka_ref[pl.ds(off, n * TILE), sl]), va_ref[pl.ds(off, n * TILE), sl]

    far_tiles = unroll * (jnp.maximum(qi - 1, 0) // unroll)
    start = jnp.minimum(far_tiles, nb - (unroll + 1))
    for hh in range(2):
        s, vt = tiles(start, hh, unroll + 1)
        bias = jnp.concatenate([_slot_bias(start + u, qi, far_tiles, f_ref[hh, 0], f_ref[hh, 1])
                                for u in range(unroll + 1)], axis=1)
        _flash_update(s + bias, vt, m_ref.at[hh], acc_ref.at[hh], True)

    def far(i, c):
        for hh in range(2):
            s, vt = tiles(i * unroll, hh, unroll)
            _flash_update(s, vt, m_ref.at[hh], acc_ref.at[hh], False)
        return c

    lax.fori_loop(0, far_tiles // unroll, far, 0)
    outs = []
    for hh in range(2):
        acc = acc_ref[hh]
        outs.append(acc / acc[:, DHA:DHA + 1])
    o_ref[...] = jnp.where(lane < DHA, outs[0], pltpu.roll(outs[1], DHA, 1))


def _moba(tab, q, kmean2, kaug, vaug):
    t = q.shape[0]
    return pl.pallas_call(
        functools.partial(_moba_kernel, nb=t // TILE, unroll=min(MOBA_UNROLL, t // TILE - 1)),
        grid=(HA // 2, t // TILE),
        in_specs=[_smem(),
                  pl.BlockSpec((TILE, 128), lambda hp, qi: (qi, hp)),
                  pl.BlockSpec((128, 128), lambda hp, qi: (0, hp)),
                  pl.BlockSpec((t, 256), lambda hp, qi: (0, hp)),
                  pl.BlockSpec((t, 256), lambda hp, qi: (0, hp))],
        out_specs=pl.BlockSpec((TILE, 128), lambda hp, qi: (qi, hp)),
        out_shape=jax.ShapeDtypeStruct((t, HA * DHA), F32),
        scratch_shapes=[pltpu.VMEM((2, 2, TILE, TILE), F32),
                        pltpu.VMEM((2, TILE, 128), F32),
                        pltpu.VMEM((2, TILE, 128), F32)],
        compiler_params=_params("arbitrary", "arbitrary"),
        name="moba_attention",
    )(tab, q, kmean2, kaug, vaug)


def _rglru_kernel(xr_ref, gr_ref, cw_ref, cb_ref, wrg_ref, brg_ref, wig_ref, big_ref, lam_ref,
                  o_ref, xbuf, a_s, b_s, hcar, *, tt):
    @pl.when(pl.program_id(0) == 0)
    def _():
        xbuf[0:8, :] = jnp.zeros((8, LRU_W), F32)
        hcar[...] = jnp.zeros_like(hcar)

    xbuf[8:8 + tt, :] = xr_ref[...]
    xc = cb_ref[...]
    for j in range(CONV_W):
        xc = xc + xbuf[8 - (CONV_W - 1) + j:8 - (CONV_W - 1) + j + tt, :] * cw_ref[j:j + 1, :]
    xbuf[0:8, :] = xbuf[tt:tt + 8, :]

    xcb = xc.astype(BF16)
    r = jax.nn.sigmoid(jnp.dot(xcb, wrg_ref[...], preferred_element_type=F32) + brg_ref[...])
    ig = jax.nn.sigmoid(jnp.dot(xcb, wig_ref[...], preferred_element_type=F32) + big_ref[...])
    z = -lam_ref[...]
    softplus = jnp.maximum(z, 0.0) + jnp.log(1.0 + jnp.exp(-jnp.abs(z)))
    a = jnp.exp(-LRU_C * r * softplus)
    a_s[...] = a
    b_s[...] = jnp.sqrt(1.0 - a * a) * ig * xc

    row = lax.broadcasted_iota(jnp.int32, (8, LRU_W), 0)

    def body(gi, h):
        off = pl.multiple_of(gi * 8, 8)
        a8 = a_s[pl.ds(off, 8), :]
        b8 = b_s[pl.ds(off, 8), :]
        for s in (1, 2, 4):
            a_sh = jnp.where(row >= s, pltpu.roll(a8, s, 0), 1.0)
            b_sh = jnp.where(row >= s, pltpu.roll(b8, s, 0), 0.0)
            b8 = a8 * b_sh + b8
            a8 = a8 * a_sh
        h8 = b8 + a8 * h
        b_s[pl.ds(off, 8), :] = h8
        return h8[7:8, :]

    hcar[0:1, :] = lax.fori_loop(0, tt // 8, body, hcar[0:1, :])
    o_ref[...] = b_s[...] * _gelu_tanh(gr_ref[...])


def _rglru(xr, gr, conv_w, conv_b, wrg, b_rg, wig, b_ig, lam, tt=512):
    t = xr.shape[0]
    row = lambda a: a.reshape(1, LRU_W)
    full = lambda shape: pl.BlockSpec(shape, lambda i: (0, 0))
    return pl.pallas_call(
        functools.partial(_rglru_kernel, tt=tt),
        grid=(t // tt,),
        in_specs=[pl.BlockSpec((tt, LRU_W), lambda i: (i, 0)),
                  pl.BlockSpec((tt, LRU_W), lambda i: (i, 0)),
                  full((CONV_W, LRU_W)), full((1, LRU_W)),
                  full((LRU_W, LRU_W)), full((1, LRU_W)),
                  full((LRU_W, LRU_W)), full((1, LRU_W)), full((1, LRU_W))],
        out_specs=pl.BlockSpec((tt, LRU_W), lambda i: (i, 0)),
        out_shape=jax.ShapeDtypeStruct((t, LRU_W), F32),
        scratch_shapes=[pltpu.VMEM((tt + 8, LRU_W), F32),
                        pltpu.VMEM((tt, LRU_W), F32),
                        pltpu.VMEM((tt, LRU_W), F32),
                        pltpu.VMEM((8, LRU_W), F32)],
        compiler_params=_params("arbitrary"),
        name="rglru",
    )(xr, gr, conv_w, row(conv_b), wrg, row(b_rg), wig, row(b_ig), row(lam))


def _post(y, g_ref, x_ref, o_ref):
    ms = jnp.mean(y * y, axis=-1, keepdims=True)
    o_ref[...] = x_ref[...] + y * lax.rsqrt(ms + EPS) * g_ref[...]


def _outproj_ab_kernel(a1_ref, a2_ref, w1_ref, w2_ref, g_ref, x_ref, o_ref):
    y = (jnp.dot(a1_ref[...].astype(BF16), w1_ref[...], preferred_element_type=F32)
         + jnp.dot(a2_ref[...].astype(BF16), w2_ref[...], preferred_element_type=F32))
    _post(y, g_ref, x_ref, o_ref)


def _outproj_ab(a1, a2, w1, w2, g, x, tm=512):
    t = x.shape[0]
    rows = lambda w: pl.BlockSpec((tm, w), lambda i: (i, 0))
    full = lambda shape: pl.BlockSpec(shape, lambda i: (0, 0))
    return pl.pallas_call(
        _outproj_ab_kernel,
        grid=(t // tm,),
        in_specs=[rows(a1.shape[1]), rows(a2.shape[1]), full(w1.shape), full(w2.shape),
                  full((1, D)), rows(D)],
        out_specs=rows(D),
        out_shape=jax.ShapeDtypeStruct((t, D), F32),
        compiler_params=_params("arbitrary"),
        name="outproj_ab",
    )(a1, a2, w1, w2, g.reshape(1, D), x)


def _outproj_c_kernel(oc_ref, os_ref, ow_ref, gl_ref, w_ref, g_ref, x_ref, o_ref):
    gates = jax.nn.sigmoid(gl_ref[...])
    parts = []
    for h in range(HC):
        sl = slice(h * DHC, (h + 1) * DHC)
        parts.append(gates[:, 3 * h:3 * h + 1] * oc_ref[:, sl]
                     + gates[:, 3 * h + 1:3 * h + 2] * os_ref[:, sl]
                     + gates[:, 3 * h + 2:3 * h + 3] * ow_ref[:, sl])
    o = jnp.concatenate(parts, axis=1).astype(BF16)
    _post(jnp.dot(o, w_ref[...], preferred_element_type=F32), g_ref, x_ref, o_ref)


def _outproj_c(oc, os_, ow, gl, w, g, x, tm=512):
    t = x.shape[0]
    rows = lambda w_: pl.BlockSpec((tm, w_), lambda i: (i, 0))
    full = lambda shape: pl.BlockSpec(shape, lambda i: (0, 0))
    return pl.pallas_call(
        _outproj_c_kernel,
        grid=(t // tm,),
        in_specs=[rows(D), rows(D), rows(D), rows(128), full(w.shape), full((1, D)), rows(D)],
        out_specs=rows(D),
        out_shape=jax.ShapeDtypeStruct((t, D), F32),
        compiler_params=_params("arbitrary"),
        name="outproj_c",
    )(oc, os_, ow, gl, w, g.reshape(1, D), x)


def _mlp_kernel(x_ref, gpre_ref, wup_ref, wdn_ref, gpost_ref, o_ref, h_s, acc_s):
    c = pl.program_id(1)

    @pl.when(c == 0)
    def _():
        x = x_ref[...]
        ms = jnp.mean(x * x, axis=-1, keepdims=True)
        h_s[...] = (x * lax.rsqrt(ms + EPS) * gpre_ref[...]).astype(BF16)
        acc_s[...] = jnp.zeros_like(acc_s)

    u = jnp.maximum(jnp.dot(h_s[...], wup_ref[...], preferred_element_type=F32), 0.0)
    acc_s[...] += jnp.dot((u * u).astype(BF16), wdn_ref[...], preferred_element_type=F32)

    @pl.when(c == pl.num_programs(1) - 1)
    def _():
        _post(acc_s[...], gpost_ref, x_ref, o_ref)


def _mlp(x, gpre, wup, wdn, gpost, tm=1024, tf=512):
    t = x.shape[0]
    return pl.pallas_call(
        _mlp_kernel,
        grid=(t // tm, D_FF // tf),
        in_specs=[pl.BlockSpec((tm, D), lambda i, c: (i, 0)),
                  pl.BlockSpec((1, D), lambda i, c: (0, 0)),
                  pl.BlockSpec((D, tf), lambda i, c: (0, c)),
                  pl.BlockSpec((tf, D), lambda i, c: (c, 0)),
                  pl.BlockSpec((1, D), lambda i, c: (0, 0))],
        out_specs=pl.BlockSpec((tm, D), lambda i, c: (i, 0)),
        out_shape=jax.ShapeDtypeStruct((t, D), F32),
        scratch_shapes=[pltpu.VMEM((tm, D), BF16), pltpu.VMEM((tm, D), F32)],
        compiler_params=_params("arbitrary", "arbitrary"),
        name="mlp",
    )(x, gpre.reshape(1, D), wup, wdn, gpost.reshape(1, D))


def _compress_kernel(h_ref, pos_ref, w1_ref, w2_ref, o_ref):
    half = CMP_STRIDE * DHC
    hm = h_ref[0]
    top = jnp.dot((hm + pos_ref[:, 0:half]).astype(BF16), w1_ref[0:half, :], preferred_element_type=F32)
    bot = jnp.dot((hm + pos_ref[:, half:2 * half]).astype(BF16), w1_ref[half:2 * half, :],
                  preferred_element_type=F32)
    hid = top + pltpu.roll(bot, bot.shape[0] - 1, 0)
    o_ref[0] = jnp.dot(_gelu_tanh(hid).astype(BF16), w2_ref[...], preferred_element_type=F32).astype(o_ref.dtype)


def _compress(hmat, pos, w1, w2):
    g, n, wd = hmat.shape
    return pl.pallas_call(
        _compress_kernel,
        grid=(g,),
        in_specs=[pl.BlockSpec((1, n, wd), lambda i: (i, 0, 0)),
                  pl.BlockSpec((1, 2 * wd), lambda i: (0, 0)),
                  pl.BlockSpec(w1.shape, lambda i: (0, 0)),
                  pl.BlockSpec(w2.shape, lambda i: (0, 0))],
        out_specs=pl.BlockSpec((1, n, DHC), lambda i: (i, 0, 0)),
        out_shape=jax.ShapeDtypeStruct((g, n, DHC), BF16),
        compiler_params=_params("arbitrary"),
        name="nsa_compress",
    )(hmat, pos.reshape(1, 2 * wd), w1, w2)


def _stack_heads(q):
    return jnp.concatenate([q[:, p * DHC:(p + 1) * DHC] for p in range(PC)], axis=0)


def _nsa_cmp_kernel(tab_ref, q_ref, kc_ref, vc_ref, ov_ref, oc_ref, sb_ref, g_ref, *, tq, ns):
    g = pl.program_id(0)
    qi = pl.program_id(1)
    ncp = kc_ref.shape[1]
    scale = DHC ** -0.5
    start = qi * tq
    ii = lax.broadcasted_iota(jnp.int32, (tq, ncp), 0)
    nn = lax.broadcasted_iota(jnp.int32, (tq, ncp), 1)

    @pl.when(qi == 0)
    def _():
        dg = ii - (CMP_LEN - 1) + CMP_STRIDE * (ncp - tq // CMP_STRIDE) - CMP_STRIDE * nn
        dg = jnp.maximum(dg, 0)
        for p in range(PC):
            g_ref[p] = _bias_delta(dg, tab_ref, PC * g + p)

    d = start + ii - CMP_STRIDE * nn - (CMP_LEN - 1)
    valid = d >= 0
    q4 = (_stack_heads(q_ref[...]) * scale).astype(BF16)
    s_all = _dot_t(q4, kc_ref[0])
    shift = (qi + 1) * (tq // CMP_STRIDE)
    psum = jnp.zeros((tq, ncp), F32)
    for p in range(PC):
        s = s_all[p * tq:(p + 1) * tq] + pltpu.roll(g_ref[p], shift, 1)
        s = jnp.where(valid, s, NEG)
        m = jnp.max(s, axis=1, keepdims=True)
        e = jnp.where(valid, jnp.exp(s - m), 0.0)
        pc = e / jnp.maximum(jnp.sum(e, axis=1, keepdims=True), 1e-30)
        oc_ref[:, p * DHC:(p + 1) * DHC] = jnp.dot(pc.astype(BF16), vc_ref[0], preferred_element_type=F32)
        psum = psum + pc

    imp = jnp.dot(psum, ov_ref[...], precision=HIGHEST, preferred_element_type=F32)
    nsp = imp.shape[1]
    jj = lax.broadcasted_iota(jnp.int32, (tq, nsp), 1)
    cur = (start + lax.broadcasted_iota(jnp.int32, (tq, nsp), 0)) // SEL_BLOCK
    forced = (jj == 0) | (jj == cur) | (jj == cur - 1)
    imp = jnp.where(forced, imp + SEL_FORCE, imp)
    imp = jnp.where((jj > cur) | (jj >= ns), NEG, imp)
    sel = _topk_mask(imp, jj, min(SEL_TOPK, ns))
    sb_ref[0] = jnp.where(sel, 0.0, NEG).astype(BF16)


def _nsa_cmp(tab, q, kc, vc, overlap, ns, tq=128):
    t = q.shape[0]
    ncp = kc.shape[1]
    nsp = overlap.shape[1]
    return pl.pallas_call(
        functools.partial(_nsa_cmp_kernel, tq=tq, ns=ns),
        grid=(GC, t // tq),
        in_specs=[_smem(),
                  pl.BlockSpec((tq, PC * DHC), lambda g, qi: (qi, g)),
                  pl.BlockSpec((1, ncp, DHC), lambda g, qi: (g, 0, 0)),
                  pl.BlockSpec((1, ncp, DHC), lambda g, qi: (g, 0, 0)),
                  pl.BlockSpec((ncp, nsp), lambda g, qi: (0, 0))],
        out_specs=[pl.BlockSpec((tq, PC * DHC), lambda g, qi: (qi, g)),
                   pl.BlockSpec((1, tq, nsp), lambda g, qi: (g, qi, 0))],
        out_shape=[jax.ShapeDtypeStruct((t, HC * DHC), F32),
                   jax.ShapeDtypeStruct((GC, t, nsp), BF16)],
        scratch_shapes=[pltpu.VMEM((PC, tq, ncp), F32)],
        compiler_params=_params("arbitrary", "arbitrary"),
        name="nsa_compressed_topk",
    )(tab, q, kc, vc, overlap)


def _nsa_sel_kernel(tab_ref, q_ref, sb_ref, ka_ref, va_ref, o_ref, f_ref, m_ref, acc_ref, *, nhalf, nb, unroll):
    g = pl.program_id(0)
    qi = pl.program_id(1)
    scale = DHC ** -0.5 * LOG2E
    tiles_per_half = 128 * SEL_BLOCK // TILE

    @pl.when(qi == 0)
    def _():
        _near_bias_tiles(f_ref, tab_ref, [PC * g + p for p in range(PC)])

    hpc = PC // SEL_CHUNKS
    q = q_ref[...]
    sb = sb_ref[0]
    qas = []
    for c in range(SEL_CHUNKS):
        qc = jnp.concatenate([q[:, p * DHC:(p + 1) * DHC] for p in range(c * hpc, (c + 1) * hpc)], axis=0)
        qc = (qc * scale).astype(BF16)
        qas.append([jnp.concatenate([qc, jnp.concatenate([sb[:, hf * 128:(hf + 1) * 128]] * hpc, axis=0)], axis=1)
                    for hf in range(nhalf)])

    def qa_for(c, j):
        if nhalf == 1:
            return qas[c][0]
        return jnp.where(j >= tiles_per_half, qas[c][1], qas[c][0])

    far_tiles = unroll * (jnp.maximum(qi - 1, 0) // unroll)
    start = jnp.minimum(far_tiles, nb - (unroll + 1))
    vt = va_ref[pl.ds(pl.multiple_of(start * TILE, TILE), (unroll + 1) * TILE), :]
    for c in range(SEL_CHUNKS):
        cols = []
        for u in range(unroll + 1):
            j = start + u
            s = _dot_t(qa_for(c, j), ka_ref[pl.ds(pl.multiple_of(j * TILE, TILE), TILE), :])
            cols.append(jnp.concatenate(
                [s[i * TILE:(i + 1) * TILE]
                 + _slot_bias(j, qi, far_tiles, f_ref[c * hpc + i, 0], f_ref[c * hpc + i, 1])
                 for i in range(hpc)], axis=0))
        _flash_update(jnp.concatenate(cols, axis=1), vt, m_ref.at[c], acc_ref.at[c], True)

    steps_per_half = tiles_per_half // unroll
    nsteps = far_tiles // unroll
    for hf in range(nhalf):
        def far(i, carry, hf=hf):
            off = pl.multiple_of(i * (unroll * TILE), unroll * TILE)
            for c in range(SEL_CHUNKS):
                s = _dot_t(qas[c][hf], ka_ref[pl.ds(off, unroll * TILE), :])
                _flash_update(s, va_ref[pl.ds(off, unroll * TILE), :], m_ref.at[c], acc_ref.at[c], False)
            return carry
        lo = hf * steps_per_half
        hi = jnp.minimum(nsteps, (hf + 1) * steps_per_half) if hf + 1 < nhalf else nsteps
        lax.fori_loop(lo, jnp.maximum(hi, lo), far, 0)

    for c in range(SEL_CHUNKS):
        acc = acc_ref[c]
        o = acc[:, :DHC] / acc[:, DHC:DHC + 1]
        for i in range(hpc):
            p = c * hpc + i
            o_ref[:, p * DHC:(p + 1) * DHC] = o[i * TILE:(i + 1) * TILE]


def _nsa_sel(tab, q, selbias, kaug, vaug):
    t = q.shape[0]
    nsp = selbias.shape[2]
    return pl.pallas_call(
        functools.partial(_nsa_sel_kernel, nhalf=nsp // 128, nb=t // TILE,
                          unroll=min(SEL_UNROLL, t // TILE - 1)),
        grid=(GC, t // TILE),
        in_specs=[_smem(),
                  pl.BlockSpec((TILE, PC * DHC), lambda g, qi: (qi, g)),
                  pl.BlockSpec((1, TILE, nsp), lambda g, qi: (g, qi, 0)),
                  pl.BlockSpec((t, 2 * DHC), lambda g, qi: (0, g)),
                  pl.BlockSpec((t, 2 * DHC), lambda g, qi: (0, g))],
        out_specs=pl.BlockSpec((TILE, PC * DHC), lambda g, qi: (qi, g)),
        out_shape=jax.ShapeDtypeStruct((t, HC * DHC), F32),
        scratch_shapes=[pltpu.VMEM((PC, 2, TILE, TILE), F32),
                        pltpu.VMEM((SEL_CHUNKS, PC // SEL_CHUNKS * TILE, 128), F32),
                        pltpu.VMEM((SEL_CHUNKS, PC // SEL_CHUNKS * TILE, 2 * DHC), F32)],
        compiler_params=_params("arbitrary", "arbitrary"),
        name="nsa_selected",
    )(tab, q, selbias, kaug, vaug)


def _nsa_win_kernel(tab_ref, q_ref, k_ref, v_ref, o_ref, f_ref, *, tq):
    g = pl.program_id(0)
    qi = pl.program_id(1)
    scale = DHC ** -0.5
    nw = WINDOW + tq
    ii = lax.broadcasted_iota(jnp.int32, (tq, nw), 0)
    jj = lax.broadcasted_iota(jnp.int32, (tq, nw), 1)

    @pl.when(qi == 0)
    def _():
        d = WINDOW + ii - jj
        ok = (d >= 0) & (d < WINDOW)
        for p in range(PC):
            f_ref[p] = jnp.where(ok, _bias_delta(jnp.maximum(d, 0), tab_ref, PC * g + p), NEG)

    start = pl.multiple_of(qi * tq, tq)
    pos_ok = jnp.where(jj >= WINDOW - start, 0.0, NEG)
    q4 = (_stack_heads(q_ref[...]) * scale).astype(BF16)
    s_all = _dot_t(q4, k_ref[pl.ds(start, nw), :])
    vt = v_ref[pl.ds(start, nw), :]
    for p in range(PC):
        s = s_all[p * tq:(p + 1) * tq] + f_ref[p] + pos_ok
        m = jnp.max(s, axis=1, keepdims=True)
        e = jnp.exp(s - m)
        pw = e / jnp.sum(e, axis=1, keepdims=True)
        o_ref[:, p * DHC:(p + 1) * DHC] = jnp.dot(pw.astype(BF16), vt, preferred_element_type=F32)


def _nsa_win(tab, q, kwp, vwp, tq=128):
    t = q.shape[0]
    tp = kwp.shape[0]
    return pl.pallas_call(
        functools.partial(_nsa_win_kernel, tq=tq),
        grid=(GC, t // tq),
        in_specs=[_smem(),
                  pl.BlockSpec((tq, PC * DHC), lambda g, qi: (qi, g)),
                  pl.BlockSpec((tp, DHC), lambda g, qi: (0, g)),
                  pl.BlockSpec((tp, DHC), lambda g, qi: (0, g))],
        out_specs=pl.BlockSpec((tq, PC * DHC), lambda g, qi: (qi, g)),
        out_shape=jax.ShapeDtypeStruct((t, HC * DHC), F32),
        scratch_shapes=[pltpu.VMEM((PC, tq, WINDOW + tq), F32)],
        compiler_params=_params("arbitrary", "arbitrary"),
        name="nsa_window",
    )(tab, q, kwp, vwp)


def _layer_ab(x, tab, w_in, conv_w, conv_b, w_rg, b_rg, w_ig, b_ig, lam, w_out, g_pre, g_post):
    t = x.shape[0]
    da = HA * DHA
    wq, wk, wv = (w_in[:, i * da:(i + 1) * da] for i in range(3))
    wx = w_in[:, 3 * da:]
    wk3 = wk.reshape(D, HA, DHA)
    wv3 = wv.reshape(D, HA, DHA)
    zeros = jnp.zeros_like(wk3)
    even = (jnp.arange(HA) % 2 == 0)[None, :, None]
    wk_aug = jnp.concatenate([jnp.where(even, wk3, zeros), jnp.where(even, zeros, wk3)], axis=-1)
    wv_aug = jnp.concatenate([wv3, zeros], axis=-1)
    w = jnp.concatenate([wq, wk, wx, wk_aug.reshape(D, HA * 128), wv_aug.reshape(D, HA * 128)],
                        axis=1).astype(BF16)
    segs = _segments([(da, None), (da, None), (LRU_W, None), (LRU_W, None),
                      (HA * 128, ("onehot", 128, MOBA_BLOCK, True)), (HA * 128, ("ones", 128, DHA))])
    q, k, xr, gr, kaug, vaug = _rms_proj(x, g_pre, w, segs, [F32, F32, F32, F32, BF16, BF16])

    nb = t // MOBA_BLOCK
    kmean = jnp.pad(_block_mean(k), ((0, DHA - nb), (0, 0)))
    o_a = _moba(tab, q, jnp.concatenate([kmean, kmean], axis=0), kaug, vaug)

    eye = jnp.eye(w_rg.shape[0], dtype=F32)
    blockdiag = lambda wb: jnp.einsum('bij,bc->bicj', wb, eye).reshape(LRU_W, LRU_W).astype(BF16)
    o_b = _rglru(xr, gr, conv_w, conv_b, blockdiag(w_rg), b_rg, blockdiag(w_ig), b_ig, lam)
    wo = w_out.astype(BF16)
    return _outproj_ab(o_a, o_b, wo[:da], wo[da:], g_post, x)


def _layer_c(x, tab, w_in, pos_k, pos_v, k_w1, k_w2, v_w1, v_w2, w_out, g_pre, g_post):
    t = x.shape[0]
    dq = HC * DHC
    ckv = GC * DHC
    cols = lambda j: w_in[:, dq + j * ckv: dq + (j + 1) * ckv]
    widen = lambda wc: jnp.concatenate([wc.reshape(D, GC, DHC), jnp.zeros((D, GC, DHC), wc.dtype)],
                                       axis=-1).reshape(D, GC * 2 * DHC)
    wg = jnp.pad(w_in[:, dq + 6 * ckv:], ((0, 0), (0, 128 - 3 * HC)))
    w = jnp.concatenate([w_in[:, :dq], cols(0), cols(1), widen(cols(2)), widen(cols(3)), cols(4), cols(5), wg],
                        axis=1).astype(BF16)
    segs = _segments([(dq, None), (DHC, None), (DHC, None), (DHC, None), (DHC, None),
                      (GC * 2 * DHC, ("onehot", 2 * DHC, SEL_BLOCK, False)),
                      (GC * 2 * DHC, ("ones", 2 * DHC, DHC)),
                      (ckv, None), (ckv, None), (128, None)])
    q, kc0, kc1, vc0, vc1, ks_aug, vs_aug, kw, vw, gl = _rms_proj(
        x, g_pre, w, segs, [F32, F32, F32, F32, F32, BF16, BF16, BF16, BF16, F32])

    n = t // CMP_STRIDE
    hm = lambda a0, a1: jnp.stack([a0.reshape(n, CMP_STRIDE * DHC), a1.reshape(n, CMP_STRIDE * DHC)])
    kc = _compress(hm(kc0, kc1), pos_k, k_w1.astype(BF16), k_w2.astype(BF16))
    vc = _compress(hm(vc0, vc1), pos_v, v_w1.astype(BF16), v_w2.astype(BF16))

    ns = t // SEL_BLOCK
    nsp = -(-ns // 128) * 128
    cs = jnp.arange(n)[:, None] * CMP_STRIDE
    ss = jnp.arange(nsp)[None, :] * SEL_BLOCK
    overlap = ((cs < ss + SEL_BLOCK) & (cs + CMP_LEN > ss) & (jnp.arange(n)[:, None] < n - 1)
               & (jnp.arange(nsp)[None, :] < ns)).astype(F32)
    o_c, selbias = _nsa_cmp(tab, q, kc, vc, overlap, ns)
    o_s = _nsa_sel(tab, q, selbias, ks_aug, vs_aug)
    padw = lambda a: jnp.pad(a, ((WINDOW, 0), (0, 0)))
    o_w = _nsa_win(tab, q, padw(kw), padw(vw))
    return _outproj_c(o_c, o_s, o_w, gl, w_out.astype(BF16), g_post, x)


def kernel(x, rel_bias, ab_w_in, ab_conv_w, ab_conv_b, ab_w_rg, ab_b_rg, ab_w_ig, ab_b_ig, ab_lru_lambda, ab_w_out, c_w_in, c_cmp_pos_k, c_cmp_pos_v, c_cmp_k_w1, c_cmp_k_w2, c_cmp_v_w1, c_cmp_v_w2, c_w_out, norm_mix_pre, norm_mix_post, norm_mlp_pre, norm_mlp_post, w_up, w_down):
    batch = x.shape[0]
    depth = norm_mix_pre.shape[0]
    outs = []
    for b in range(batch):
        xb = x[b]
        for layer in range(depth):
            i = layer // 2
            if layer % 2 == 0:
                xb = _layer_ab(xb, rel_bias, ab_w_in[i], ab_conv_w[i], ab_conv_b[i], ab_w_rg[i], ab_b_rg[i],
                               ab_w_ig[i], ab_b_ig[i], ab_lru_lambda[i], ab_w_out[i],
                               norm_mix_pre[layer], norm_mix_post[layer])
            else:
                xb = _layer_c(xb, rel_bias, c_w_in[i], c_cmp_pos_k[i], c_cmp_pos_v[i], c_cmp_k_w1[i],
                              c_cmp_k_w2[i], c_cmp_v_w1[i], c_cmp_v_w2[i], c_w_out[i],
                              norm_mix_pre[layer], norm_mix_post[layer])
            xb = _mlp(xb, norm_mlp_pre[layer], w_up[layer].astype(BF16), w_down[layer].astype(BF16),
                      norm_mlp_post[layer])
        outs.append(xb)
    return jnp.stack(outs)
```

```python
import functools
import math

import numpy as np
import jax
import jax.numpy as jnp
from jax import lax
from jax.experimental import pallas as pl
from jax.experimental.pallas import tpu as pltpu

F32 = jnp.float32
BF16 = jnp.bfloat16
HIGHEST = lax.Precision.HIGHEST
LOG2E = 1.4426950408889634

D = 1024
EPS = 1e-6
NEG = -1e30
N_BUCKETS = 32
MAX_DISTANCE = 128

HA = 8
DHA = 64
MOBA_BLOCK = 256
MOBA_TOPK = 3
LRU_W = 512
LRU_C = 8.0
CONV_W = 4
HC = 8
DHC = 128
GC = 2
PC = 4
CMP_LEN = 32
CMP_STRIDE = 16
SEL_BLOCK = 64
SEL_TOPK = 16
WINDOW = 512
SEL_FORCE = 1e4
D_FF = 4096

TILE = 256
MOBA_UNROLL = 8
MOBA_HEADS = 4
SEL_UNROLL = 4
SEL_CHUNKS = 2
CMP_TQ = 256
CMP_BIAS_LANES = 32
CMP_MAX_FLOOR = -1e20
VMEM_LIMIT = 56 * 1024 * 1024


def _bucket_np(d):
    n = np.maximum(d, 0)
    exact = N_BUCKETS // 2
    nf = np.maximum(n, 1).astype(np.float32)
    large = exact + (np.log(nf / np.float32(exact)) / np.float32(math.log(MAX_DISTANCE / exact))
                     * np.float32(N_BUCKETS - exact)).astype(np.int32)
    return np.where(n < exact, n, np.minimum(large, N_BUCKETS - 1))


_THR = [int(np.argmax(_bucket_np(np.arange(0, 4 * MAX_DISTANCE)) >= k)) for k in range(1, N_BUCKETS)]
assert _THR[-1] <= MAX_DISTANCE


def _bias_delta(d, tab_ref, h):
    acc = jnp.full(d.shape, tab_ref[0, h] - tab_ref[N_BUCKETS - 1, h], F32)
    for k in range(1, N_BUCKETS):
        acc = acc + jnp.where(d >= _THR[k - 1], tab_ref[k, h] - tab_ref[k - 1, h], 0.0)
    return acc


def _dot_t(a, b, **kw):
    return lax.dot_general(a, b, (((1,), (1,)), ((), ())), preferred_element_type=F32, **kw)


def _gelu_tanh(x):
    return 0.5 * x * (1.0 + jnp.tanh(0.7978845608028654 * (x + 0.044715 * x * x * x)))


def _params(*sem):
    return pltpu.CompilerParams(dimension_semantics=sem, vmem_limit_bytes=VMEM_LIMIT)


def _smem():
    return pl.BlockSpec(memory_space=pltpu.SMEM)


def _proj_kernel(x_ref, g_ref, w_ref, *out_refs, segs, tm):
    x = x_ref[...]
    ms = jnp.mean(x * x, axis=-1, keepdims=True)
    hn = (x * lax.rsqrt(ms + EPS) * g_ref[...]).astype(BF16)
    row0 = pl.program_id(0) * tm
    for o_ref, (start, width, aug) in zip(out_refs, segs):
        y = jnp.dot(hn, w_ref[:, start:start + width], preferred_element_type=F32)
        if aug is not None:
            lane = lax.broadcasted_iota(jnp.int32, y.shape, 1)
            if aug[0] == "onehot":
                _, group, rows_per_block, lo_first = aug
                half = group // 2
                row = row0 + lax.broadcasted_iota(jnp.int32, y.shape, 0)
                which_half = (lane // half) % 2
                data_half = (lane // group) % 2 if lo_first else jnp.zeros_like(lane)
                blk = (row // rows_per_block) % half
                y = jnp.where((which_half != data_half) & (lane % half == blk), 1.0, y)
            else:
                _, group, col = aug
                y = jnp.where(lane % group == col, 1.0, y)
        o_ref[...] = y.astype(o_ref.dtype)


def _rms_proj(x, g, w, segs, dtypes, tm=512):
    t = x.shape[0]
    return pl.pallas_call(
        functools.partial(_proj_kernel, segs=segs, tm=tm),
        grid=(t // tm,),
        in_specs=[pl.BlockSpec((tm, D), lambda i: (i, 0)),
                  pl.BlockSpec((1, D), lambda i: (0, 0)),
                  pl.BlockSpec(w.shape, lambda i: (0, 0))],
        out_specs=[pl.BlockSpec((tm, wd), lambda i: (i, 0)) for (_, wd, _) in segs],
        out_shape=[jax.ShapeDtypeStruct((t, wd), dt) for (_, wd, _), dt in zip(segs, dtypes)],
        compiler_params=_params("arbitrary"),
        name="rms_proj",
    )(x, g.reshape(1, D), w)


def _segments(widths_augs):
    segs, off = [], 0
    for wd, aug in widths_augs:
        segs.append((off, wd, aug))
        off += wd
    return segs


def _topk_mask(r, lane, k):
    sel = jnp.zeros(r.shape, jnp.bool_)
    lanef = lane.astype(F32)
    for _ in range(k):
        mx = jnp.max(r, axis=1, keepdims=True)
        first = jnp.min(jnp.where(r == mx, lanef, 1e9), axis=1, keepdims=True)
        pick = lanef == first
        sel = sel | (pick & (mx > NEG / 2))
        r = jnp.where(pick, NEG, r)
    return sel


def _near_bias_tiles(f_ref, tab_ref, heads):
    d0 = (lax.broadcasted_iota(jnp.int32, (TILE, TILE), 0)
          - lax.broadcasted_iota(jnp.int32, (TILE, TILE), 1))
    for i, h in enumerate(heads):
        f_ref[i, 0] = jnp.where(d0 >= 0, LOG2E * _bias_delta(jnp.maximum(d0, 0), tab_ref, h), NEG)
        f_ref[i, 1] = LOG2E * _bias_delta(d0 + TILE, tab_ref, h)


def _slot_bias(j, qi, lo, f_diag, f_prev):
    other = jnp.where((j >= lo) & (j <= qi), 0.0, NEG)
    return jnp.where(j == qi, f_diag, jnp.where(j == qi - 1, f_prev, other))


def _flash_update(s, vt, m_ref, acc_ref, first):
    rm = jnp.max(s, axis=1, keepdims=True)
    if first:
        m_ref[...] = jnp.broadcast_to(rm, m_ref.shape)
        p = jnp.exp2(s - rm)
        acc_ref[...] = jnp.dot(p.astype(BF16), vt, preferred_element_type=F32)
        return
    m_old = m_ref[...]
    m_new = jnp.maximum(m_old, rm)
    alpha = jnp.exp2(m_old - m_new)
    p = jnp.exp2(s - jnp.concatenate([m_new] * (s.shape[1] // m_new.shape[1]), axis=1))
    pv = jnp.dot(p.astype(BF16), vt, preferred_element_type=F32)
    acc_ref[...] = jnp.concatenate([alpha] * (pv.shape[1] // alpha.shape[1]), axis=1) * acc_ref[...] + pv
    m_ref[...] = m_new


def _blockmean_kernel(k_ref, o_ref):
    k = k_ref[...]
    o_ref[...] = jnp.mean(k.reshape(8, MOBA_BLOCK, k.shape[-1]), axis=1)


def _block_mean(k):
    t, w = k.shape
    nb = t // MOBA_BLOCK
    return pl.pallas_call(
        _blockmean_kernel,
        grid=(nb // 8,),
        in_specs=[pl.BlockSpec((8 * MOBA_BLOCK, w), lambda i: (i, 0))],
        out_specs=pl.BlockSpec((8, w), lambda i: (i, 0)),
        out_shape=jax.ShapeDtypeStruct((nb, w), F32),
        compiler_params=_params("arbitrary"),
        name="moba_block_mean",
    )(k)


def _moba_kernel(tab_ref, q_ref, km_ref, ka_ref, va_ref, o_ref, f_ref, m_ref, acc_ref, *, nb, unroll, heads):
    h0 = pl.program_id(0) * heads
    qi = pl.program_id(1)
    scale = DHA ** -0.5 * LOG2E
    lane = lax.broadcasted_iota(jnp.int32, (TILE, 128), 1)

    @pl.when(qi == 0)
    def _():
        _near_bias_tiles(f_ref, tab_ref, [h0 + hh for hh in range(heads)])

    qas = []
    for hh in range(heads):
        pair = slice(hh // 2 * 128, (hh // 2 + 1) * 128)
        q = q_ref[:, pair]
        head_lanes = (lane < DHA) if hh % 2 == 0 else (lane >= DHA)
        qh = jnp.where(head_lanes, q, 0.0)
        route = _dot_t(qh, km_ref[:, pair], precision=HIGHEST)
        blk = lane - DHA if hh % 2 == 0 else lane
        in_aug = jnp.logical_not(head_lanes)
        r = jnp.where(in_aug & (blk < qi), route, NEG)
        sel = _topk_mask(r, lane, MOBA_TOPK)
        keep = sel | (in_aug & (blk == qi))
        qas.append(jnp.where(head_lanes, q * scale, jnp.where(keep, 0.0, NEG)).astype(BF16))

    def tiles(j, hh, n):
        off = pl.multiple_of(j * TILE, TILE)
        sl = slice(hh * 128, (hh + 1) * 128)
        return _dot_t(qas[hh], ka_ref[pl.ds(off, n * TILE), sl]), va_ref[pl.ds(off, n * TILE), sl]

    far_tiles = unroll * (jnp.maximum(qi - 1, 0) // unroll)
    start = jnp.minimum(far_tiles, nb - (unroll + 1))
    for hh in range(heads):
        s, vt = tiles(start, hh, unroll + 1)
        bias = jnp.concatenate([_slot_bias(start + u, qi, far_tiles, f_ref[hh, 0], f_ref[hh, 1])
                                for u in range(unroll + 1)], axis=1)
        _flash_update(s + bias, vt, m_ref.at[hh], acc_ref.at[hh], True)

    def far(i, c):
        for hh in range(heads):
            s, vt = tiles(i * unroll, hh, unroll)
            _flash_update(s, vt, m_ref.at[hh], acc_ref.at[hh], False)
        return c

    lax.fori_loop(0, far_tiles // unroll, far, 0)
    for pp in range(heads // 2):
        outs = []
        for hh in (2 * pp, 2 * pp + 1):
            acc = acc_ref[hh]
            outs.append(acc / acc[:, DHA:DHA + 1])
        o_ref[:, pp * 128:(pp + 1) * 128] = jnp.where(lane < DHA, outs[0], pltpu.roll(outs[1], DHA, 1))


def _moba(tab, q, kmean2, kaug, vaug, heads=MOBA_HEADS):
    t = q.shape[0]
    once = pl.Buffered(1)
    return pl.pallas_call(
        functools.partial(_moba_kernel, nb=t // TILE, unroll=min(MOBA_UNROLL, t // TILE - 1), heads=heads),
        grid=(HA // heads, t // TILE),
        in_specs=[_smem(),
                  pl.BlockSpec((TILE, heads * DHA), lambda hg, qi: (qi, hg)),
                  pl.BlockSpec((128, heads * DHA), lambda hg, qi: (0, hg)),
                  pl.BlockSpec((t, heads * 128), lambda hg, qi: (0, hg), pipeline_mode=once),
                  pl.BlockSpec((t, heads * 128), lambda hg, qi: (0, hg), pipeline_mode=once)],
        out_specs=pl.BlockSpec((TILE, heads * DHA), lambda hg, qi: (qi, hg)),
        out_shape=jax.ShapeDtypeStruct((t, HA * DHA), F32),
        scratch_shapes=[pltpu.VMEM((heads, 2, TILE, TILE), F32),
                        pltpu.VMEM((heads, TILE, 128), F32),
                        pltpu.VMEM((heads, TILE, 128), F32)],
        compiler_params=_params("arbitrary", "arbitrary"),
        name="moba_attention",
    )(tab, q, kmean2, kaug, vaug)


def _rglru_kernel(xr_ref, gr_ref, cw_ref, cb_ref, wrg_ref, brg_ref, wig_ref, big_ref, lam_ref,
                  o_ref, xbuf, a_s, b_s, hcar, *, tt):
    @pl.when(pl.program_id(0) == 0)
    def _():
        xbuf[0:8, :] = jnp.zeros((8, LRU_W), F32)
        hcar[...] = jnp.zeros_like(hcar)

    xbuf[8:8 + tt, :] = xr_ref[...]
    xc = cb_ref[...]
    for j in range(CONV_W):
        xc = xc + xbuf[8 - (CONV_W - 1) + j:8 - (CONV_W - 1) + j + tt, :] * cw_ref[j:j + 1, :]
    xbuf[0:8, :] = xbuf[tt:tt + 8, :]

    xcb = xc.astype(BF16)
    r = jax.nn.sigmoid(jnp.dot(xcb, wrg_ref[...], preferred_element_type=F32) + brg_ref[...])
    ig = jax.nn.sigmoid(jnp.dot(xcb, wig_ref[...], preferred_element_type=F32) + big_ref[...])
    z = -lam_ref[...]
    softplus = jnp.maximum(z, 0.0) + jnp.log(1.0 + jnp.exp(-jnp.abs(z)))
    a = jnp.exp(-LRU_C * r * softplus)
    a_s[...] = a
    b_s[...] = jnp.sqrt(1.0 - a * a) * ig * xc

    row = lax.broadcasted_iota(jnp.int32, (8, LRU_W), 0)

    def body(gi, h):
        off = pl.multiple_of(gi * 8, 8)
        a8 = a_s[pl.ds(off, 8), :]
        b8 = b_s[pl.ds(off, 8), :]
        for s in (1, 2, 4):
            a_sh = jnp.where(row >= s, pltpu.roll(a8, s, 0), 1.0)
            b_sh = jnp.where(row >= s, pltpu.roll(b8, s, 0), 0.0)
            b8 = a8 * b_sh + b8
            a8 = a8 * a_sh
        h8 = b8 + a8 * h
        b_s[pl.ds(off, 8), :] = h8
        return h8[7:8, :]

    hcar[0:1, :] = lax.fori_loop(0, tt // 8, body, hcar[0:1, :])
    o_ref[...] = b_s[...] * _gelu_tanh(gr_ref[...])


def _rglru(xr, gr, conv_w, conv_b, wrg, b_rg, wig, b_ig, lam, tt=512):
    t = xr.shape[0]
    row = lambda a: a.reshape(1, LRU_W)
    full = lambda shape: pl.BlockSpec(shape, lambda i: (0, 0))
    return pl.pallas_call(
        functools.partial(_rglru_kernel, tt=tt),
        grid=(t // tt,),
        in_specs=[pl.BlockSpec((tt, LRU_W), lambda i: (i, 0)),
                  pl.BlockSpec((tt, LRU_W), lambda i: (i, 0)),
                  full((CONV_W, LRU_W)), full((1, LRU_W)),
                  full((LRU_W, LRU_W)), full((1, LRU_W)),
                  full((LRU_W, LRU_W)), full((1, LRU_W)), full((1, LRU_W))],
        out_specs=pl.BlockSpec((tt, LRU_W), lambda i: (i, 0)),
        out_shape=jax.ShapeDtypeStruct((t, LRU_W), F32),
        scratch_shapes=[pltpu.VMEM((tt + 8, LRU_W), F32),
                        pltpu.VMEM((tt, LRU_W), F32),
                        pltpu.VMEM((tt, LRU_W), F32),
                        pltpu.VMEM((8, LRU_W), F32)],
        compiler_params=_params("arbitrary"),
        name="rglru",
    )(xr, gr, conv_w, row(conv_b), wrg, row(b_rg), wig, row(b_ig), row(lam))


def _post(y, g_ref, x_ref, o_ref):
    ms = jnp.mean(y * y, axis=-1, keepdims=True)
    o_ref[...] = x_ref[...] + y * lax.rsqrt(ms + EPS) * g_ref[...]


def _outproj_ab_kernel(a1_ref, a2_ref, w1_ref, w2_ref, g_ref, x_ref, o_ref):
    y = (jnp.dot(a1_ref[...].astype(BF16), w1_ref[...], preferred_element_type=F32)
         + jnp.dot(a2_ref[...].astype(BF16), w2_ref[...], preferred_element_type=F32))
    _post(y, g_ref, x_ref, o_ref)


def _outproj_ab(a1, a2, w1, w2, g, x, tm=512):
    t = x.shape[0]
    rows = lambda w: pl.BlockSpec((tm, w), lambda i: (i, 0))
    full = lambda shape: pl.BlockSpec(shape, lambda i: (0, 0))
    return pl.pallas_call(
        _outproj_ab_kernel,
        grid=(t // tm,),
        in_specs=[rows(a1.shape[1]), rows(a2.shape[1]), full(w1.shape), full(w2.shape),
                  full((1, D)), rows(D)],
        out_specs=rows(D),
        out_shape=jax.ShapeDtypeStruct((t, D), F32),
        compiler_params=_params("arbitrary"),
        name="outproj_ab",
    )(a1, a2, w1, w2, g.reshape(1, D), x)


def _outproj_c_kernel(oc_ref, os_ref, ow_ref, gl_ref, w_ref, g_ref, x_ref, o_ref):
    gates = jax.nn.sigmoid(gl_ref[...])
    parts = []
    for h in range(HC):
        sl = slice(h * DHC, (h + 1) * DHC)
        parts.append(gates[:, 3 * h:3 * h + 1] * oc_ref[:, sl]
                     + gates[:, 3 * h + 1:3 * h + 2] * os_ref[:, sl]
                     + gates[:, 3 * h + 2:3 * h + 3] * ow_ref[:, sl])
    o = jnp.concatenate(parts, axis=1).astype(BF16)
    _post(jnp.dot(o, w_ref[...], preferred_element_type=F32), g_ref, x_ref, o_ref)


def _outproj_c(oc, os_, ow, gl, w, g, x, tm=512):
    t = x.shape[0]
    rows = lambda w_: pl.BlockSpec((tm, w_), lambda i: (i, 0))
    full = lambda shape: pl.BlockSpec(shape, lambda i: (0, 0))
    return pl.pallas_call(
        _outproj_c_kernel,
        grid=(t // tm,),
        in_specs=[rows(D), rows(D), rows(D), rows(128), full(w.shape), full((1, D)), rows(D)],
        out_specs=rows(D),
        out_shape=jax.ShapeDtypeStruct((t, D), F32),
        compiler_params=_params("arbitrary"),
        name="outproj_c",
    )(oc, os_, ow, gl, w, g.reshape(1, D), x)


def _mlp_kernel(x_ref, gpre_ref, wup_ref, wdn_ref, gpost_ref, o_ref, h_s, acc_s):
    c = pl.program_id(1)

    @pl.when(c == 0)
    def _():
        x = x_ref[...]
        ms = jnp.mean(x * x, axis=-1, keepdims=True)
        h_s[...] = (x * lax.rsqrt(ms + EPS) * gpre_ref[...]).astype(BF16)
        acc_s[...] = jnp.zeros_like(acc_s)

    u = jnp.maximum(jnp.dot(h_s[...], wup_ref[...], preferred_element_type=F32), 0.0)
    acc_s[...] += jnp.dot((u * u).astype(BF16), wdn_ref[...], preferred_element_type=F32)

    @pl.when(c == pl.num_programs(1) - 1)
    def _():
        _post(acc_s[...], gpost_ref, x_ref, o_ref)


def _mlp(x, gpre, wup, wdn, gpost, tm=1024, tf=512):
    t = x.shape[0]
    return pl.pallas_call(
        _mlp_kernel,
        grid=(t // tm, D_FF // tf),
        in_specs=[pl.BlockSpec((tm, D), lambda i, c: (i, 0)),
                  pl.BlockSpec((1, D), lambda i, c: (0, 0)),
                  pl.BlockSpec((D, tf), lambda i, c: (0, c)),
                  pl.BlockSpec((tf, D), lambda i, c: (c, 0)),
                  pl.BlockSpec((1, D), lambda i, c: (0, 0))],
        out_specs=pl.BlockSpec((tm, D), lambda i, c: (i, 0)),
        out_shape=jax.ShapeDtypeStruct((t, D), F32),
        scratch_shapes=[pltpu.VMEM((tm, D), BF16), pltpu.VMEM((tm, D), F32)],
        compiler_params=_params("arbitrary", "arbitrary"),
        name="mlp",
    )(x, gpre.reshape(1, D), wup, wdn, gpost.reshape(1, D))


def _compress_kernel(h_ref, pos_ref, w1_ref, w2_ref, o_ref):
    half = CMP_STRIDE * DHC
    hm = h_ref[0]
    top = jnp.dot((hm + pos_ref[:, 0:half]).astype(BF16), w1_ref[0:half, :], preferred_element_type=F32)
    bot = jnp.dot((hm + pos_ref[:, half:2 * half]).astype(BF16), w1_ref[half:2 * half, :],
                  preferred_element_type=F32)
    hid = top + pltpu.roll(bot, bot.shape[0] - 1, 0)
    o_ref[0] = jnp.dot(_gelu_tanh(hid).astype(BF16), w2_ref[...], preferred_element_type=F32).astype(o_ref.dtype)


def _compress(hmat, pos, w1, w2):
    g, n, wd = hmat.shape
    return pl.pallas_call(
        _compress_kernel,
        grid=(g,),
        in_specs=[pl.BlockSpec((1, n, wd), lambda i: (i, 0, 0)),
                  pl.BlockSpec((1, 2 * wd), lambda i: (0, 0)),
                  pl.BlockSpec(w1.shape, lambda i: (0, 0)),
                  pl.BlockSpec(w2.shape, lambda i: (0, 0))],
        out_specs=pl.BlockSpec((1, n, DHC), lambda i: (i, 0, 0)),
        out_shape=jax.ShapeDtypeStruct((g, n, DHC), BF16),
        compiler_params=_params("arbitrary"),
        name="nsa_compress",
    )(hmat, pos.reshape(1, 2 * wd), w1, w2)


def _stack_heads(q):
    return jnp.concatenate([q[:, p * DHC:(p + 1) * DHC] for p in range(PC)], axis=0)


def _removed_topk(r0, k):
    r = r0
    for _ in range(k):
        r = jnp.where(r == jnp.max(r, axis=1, keepdims=True), NEG, r)
    return (r0 > NEG / 2) & (r < NEG / 2)


def _nsa_cmp_kernel(tab_ref, q_ref, kc_ref, vc_ref, ov_ref, oc_ref, sb_ref, e_ref, qb_ref, *, tq, ns):
    qi = pl.program_id(0)
    ncp = kc_ref.shape[1]
    nsp = ov_ref.shape[1]
    scale = DHC ** -0.5 * LOG2E
    cpq = tq // CMP_STRIDE
    k = min(SEL_TOPK, ns)

    @pl.when(qi == 0)
    def _():
        u = lax.broadcasted_iota(jnp.int32, (2 * ncp, 128), 0)
        lane = lax.broadcasted_iota(jnp.int32, (2 * ncp, 128), 1)
        hit = (lane < 2 * CMP_BIAS_LANES) & (u + lane % CMP_BIAS_LANES == ncp - 1)
        after = (lane == 2 * CMP_BIAS_LANES) & (u >= ncp)
        e_ref[...] = (jnp.where(hit, 1.0, 0.0) + jnp.where(after, 1.0, 0.0)).astype(BF16)
        i = lax.broadcasted_iota(jnp.int32, (tq, 128), 0)
        lane = lax.broadcasted_iota(jnp.int32, (tq, 128), 1)
        d = i + CMP_STRIDE * (lane % CMP_BIAS_LANES) - (tq - CMP_STRIDE + CMP_LEN - 1)
        for h in range(HC):
            b = jnp.where(d >= 0, LOG2E * _bias_delta(jnp.maximum(d, 0), tab_ref, h), NEG)
            hi = b.astype(BF16).astype(F32)
            qb = jnp.where(lane < CMP_BIAS_LANES, hi, jnp.where(lane < 2 * CMP_BIAS_LANES, b - hi, 0.0))
            qb_ref[h] = jnp.where(lane == 2 * CMP_BIAS_LANES, NEG, qb).astype(BF16)

    off = pl.multiple_of(ncp - cpq * (qi + 1), 16)
    onehots = e_ref[pl.ds(off, ncp), :]
    jj = lax.broadcasted_iota(jnp.int32, (tq, nsp), 1)
    cur = (qi * tq + lax.broadcasted_iota(jnp.int32, (tq, nsp), 0)) // SEL_BLOCK
    forced = (jj == 0) | (jj == cur) | (jj == cur - 1)
    dead = (jj > cur) | (jj >= ns)
    q = q_ref[...]
    fallbacks = []
    for g in range(GC):
        heads = range(g * PC, (g + 1) * PC)
        qa = jnp.concatenate(
            [jnp.concatenate([(q[:, h * DHC:(h + 1) * DHC] * scale).astype(BF16) for h in heads], axis=0),
             jnp.concatenate([qb_ref[h] for h in heads], axis=0)], axis=1)
        s_all = _dot_t(qa, jnp.concatenate([kc_ref[g], onehots], axis=1))
        psum = jnp.zeros((tq, ncp), F32)
        for p, h in enumerate(heads):
            s = s_all[p * tq:(p + 1) * tq]
            m = jnp.maximum(jnp.max(s, axis=1, keepdims=True), CMP_MAX_FLOOR)
            e = jnp.exp2(s - m)
            pc = e * (1.0 / jnp.maximum(jnp.sum(e, axis=1, keepdims=True), 1e-30))
            oc_ref[:, h * DHC:(h + 1) * DHC] = jnp.dot(pc.astype(BF16), vc_ref[g], preferred_element_type=F32)
            psum = psum + pc

        hi = psum.astype(BF16)
        lo = (psum - hi.astype(F32)).astype(BF16)
        imp = (jnp.dot(hi, ov_ref[...], preferred_element_type=F32)
               + jnp.dot(lo, ov_ref[...], preferred_element_type=F32))
        imp = jnp.where(forced, imp + SEL_FORCE, imp)
        imp = jnp.where(dead, NEG, imp)
        sel = _removed_topk(imp, k)
        sb_ref[g] = jnp.where(sel, 0.0, NEG).astype(BF16)
        count = jnp.sum(jnp.where(sel, 1.0, 0.0), axis=1, keepdims=True)
        alive = jnp.sum(jnp.where(dead, 0.0, 1.0), axis=1, keepdims=True)
        tied = jnp.max(jnp.where(count != jnp.minimum(alive, float(k)), 1.0, 0.0))
        fallbacks.append((g, imp, tied))

    for g, imp, tied in fallbacks:
        @pl.when(tied > 0.0)
        def _(g=g, imp=imp):
            sb_ref[g] = jnp.where(_topk_mask(imp, jj, k), 0.0, NEG).astype(BF16)


def _nsa_cmp(tab, q, kc, vc, overlap, ns, tq=CMP_TQ):
    t = q.shape[0]
    ncp = kc.shape[1]
    nsp = overlap.shape[1]
    assert (tq - CMP_STRIDE + CMP_LEN - 1 + MAX_DISTANCE) // CMP_STRIDE < CMP_BIAS_LANES
    full = lambda shape: pl.BlockSpec(shape, lambda qi: (0,) * len(shape))
    return pl.pallas_call(
        functools.partial(_nsa_cmp_kernel, tq=tq, ns=ns),
        grid=(t // tq,),
        in_specs=[_smem(),
                  pl.BlockSpec((tq, HC * DHC), lambda qi: (qi, 0)),
                  full((GC, ncp, DHC)), full((GC, ncp, DHC)), full((ncp, nsp))],
        out_specs=[pl.BlockSpec((tq, HC * DHC), lambda qi: (qi, 0)),
                   pl.BlockSpec((GC, tq, nsp), lambda qi: (0, qi, 0))],
        out_shape=[jax.ShapeDtypeStruct((t, HC * DHC), F32),
                   jax.ShapeDtypeStruct((GC, t, nsp), BF16)],
        scratch_shapes=[pltpu.VMEM((2 * ncp, 128), BF16), pltpu.VMEM((HC, tq, 128), BF16)],
        compiler_params=_params("arbitrary"),
        name="nsa_compressed_topk",
    )(tab, q, kc, vc, overlap)


def _nsa_sel_kernel(tab_ref, q_ref, sb_ref, ka_ref, va_ref, o_ref, f_ref, m_ref, acc_ref, *, nhalf, nb, unroll):
    qi = pl.program_id(0)
    scale = DHC ** -0.5 * LOG2E
    tiles_per_half = 128 * SEL_BLOCK // TILE

    @pl.when(qi == 0)
    def _():
        _near_bias_tiles(f_ref, tab_ref, list(range(HC)))

    hpc = PC // SEL_CHUNKS
    chunks = [(g, list(range(g * PC + c * hpc, g * PC + (c + 1) * hpc)))
              for g in range(GC) for c in range(SEL_CHUNKS)]
    q = q_ref[...]
    qas = []
    for g, heads in chunks:
        qc = jnp.concatenate([q[:, h * DHC:(h + 1) * DHC] for h in heads], axis=0)
        qc = (qc * scale).astype(BF16)
        sb = sb_ref[g]
        qas.append([jnp.concatenate([qc, jnp.concatenate([sb[:, hf * 128:(hf + 1) * 128]] * hpc, axis=0)], axis=1)
                    for hf in range(nhalf)])

    def qa_for(c, j):
        if nhalf == 1:
            return qas[c][0]
        return jnp.where(j >= tiles_per_half, qas[c][1], qas[c][0])

    def keys(g, off, n):
        return ka_ref[pl.ds(off, n), g * 2 * DHC:(g + 1) * 2 * DHC]

    def values(g, off, n):
        return va_ref[pl.ds(off, n), g * 2 * DHC:(g + 1) * 2 * DHC]

    far_tiles = unroll * (jnp.maximum(qi - 1, 0) // unroll)
    start = jnp.minimum(far_tiles, nb - (unroll + 1))
    for c, (g, heads) in enumerate(chunks):
        cols = []
        for u in range(unroll + 1):
            j = start + u
            s = _dot_t(qa_for(c, j), keys(g, pl.multiple_of(j * TILE, TILE), TILE))
            cols.append(jnp.concatenate(
                [s[i * TILE:(i + 1) * TILE] + _slot_bias(j, qi, far_tiles, f_ref[h, 0], f_ref[h, 1])
                 for i, h in enumerate(heads)], axis=0))
        vt = values(g, pl.multiple_of(start * TILE, TILE), (unroll + 1) * TILE)
        _flash_update(jnp.concatenate(cols, axis=1), vt, m_ref.at[c], acc_ref.at[c], True)

    steps_per_half = tiles_per_half // unroll
    nsteps = far_tiles // unroll
    for hf in range(nhalf):
        def far(i, carry, hf=hf):
            off = pl.multiple_of(i * (unroll * TILE), unroll * TILE)
            for c, (g, _) in enumerate(chunks):
                s = _dot_t(qas[c][hf], keys(g, off, unroll * TILE))
                _flash_update(s, values(g, off, unroll * TILE), m_ref.at[c], acc_ref.at[c], False)
            return carry
        lo = hf * steps_per_half
        hi = jnp.minimum(nsteps, (hf + 1) * steps_per_half) if hf + 1 < nhalf else nsteps
        lax.fori_loop(lo, jnp.maximum(hi, lo), far, 0)

    for c, (g, heads) in enumerate(chunks):
        acc = acc_ref[c]
        o = acc[:, :DHC] / acc[:, DHC:DHC + 1]
        for i, h in enumerate(heads):
            o_ref[:, h * DHC:(h + 1) * DHC] = o[i * TILE:(i + 1) * TILE]


def _nsa_sel(tab, q, selbias, kaug, vaug):
    t = q.shape[0]
    nsp = selbias.shape[2]
    nchunks = GC * SEL_CHUNKS
    rows = HC // nchunks * TILE
    once = pl.Buffered(1)
    return pl.pallas_call(
        functools.partial(_nsa_sel_kernel, nhalf=nsp // 128, nb=t // TILE,
                          unroll=min(SEL_UNROLL, t // TILE - 1)),
        grid=(t // TILE,),
        in_specs=[_smem(),
                  pl.BlockSpec((TILE, HC * DHC), lambda qi: (qi, 0)),
                  pl.BlockSpec((GC, TILE, nsp), lambda qi: (0, qi, 0)),
                  pl.BlockSpec((t, GC * 2 * DHC), lambda qi: (0, 0), pipeline_mode=once),
                  pl.BlockSpec((t, GC * 2 * DHC), lambda qi: (0, 0), pipeline_mode=once)],
        out_specs=pl.BlockSpec((TILE, HC * DHC), lambda qi: (qi, 0)),
        out_shape=jax.ShapeDtypeStruct((t, HC * DHC), F32),
        scratch_shapes=[pltpu.VMEM((HC, 2, TILE, TILE), F32),
                        pltpu.VMEM((nchunks, rows, 128), F32),
                        pltpu.VMEM((nchunks, rows, 2 * DHC), F32)],
        compiler_params=_params("arbitrary"),
        name="nsa_selected",
    )(tab, q, selbias, kaug, vaug)


def _nsa_win_kernel(tab_ref, q_ref, k_ref, v_ref, o_ref, f_ref, *, tq):
    qi = pl.program_id(0)
    scale = DHC ** -0.5 * LOG2E
    nw = WINDOW + tq
    jj = lax.broadcasted_iota(jnp.int32, (tq, nw), 1)

    @pl.when(qi == 0)
    def _():
        d = WINDOW + lax.broadcasted_iota(jnp.int32, (tq, nw), 0) - jj
        ok = (d >= 0) & (d < WINDOW)
        for h in range(HC):
            f_ref[h] = jnp.where(ok, LOG2E * _bias_delta(jnp.maximum(d, 0), tab_ref, h), NEG)

    start = pl.multiple_of(qi * tq, tq)

    def body(front):
        pos_ok = jnp.where(jj >= WINDOW - start, 0.0, NEG) if front else None
        for h in range(HC):
            g = h // PC
            q = (q_ref[:, h * DHC:(h + 1) * DHC] * scale).astype(BF16)
            s = _dot_t(q, k_ref[pl.ds(start, nw), g * DHC:(g + 1) * DHC]) + f_ref[h]
            if front:
                s = s + pos_ok
            e = jnp.exp2(s - jnp.max(s, axis=1, keepdims=True))
            o = jnp.dot(e.astype(BF16), v_ref[pl.ds(start, nw), g * DHC:(g + 1) * DHC],
                        preferred_element_type=F32)
            o_ref[:, h * DHC:(h + 1) * DHC] = o * (1.0 / jnp.sum(e, axis=1, keepdims=True))

    @pl.when(start < WINDOW)
    def _():
        body(True)

    @pl.when(start >= WINDOW)
    def _():
        body(False)


def _nsa_win(tab, q, kwp, vwp, tq=TILE):
    t = q.shape[0]
    once = pl.Buffered(1)
    return pl.pallas_call(
        functools.partial(_nsa_win_kernel, tq=tq),
        grid=(t // tq,),
        in_specs=[_smem(),
                  pl.BlockSpec((tq, HC * DHC), lambda qi: (qi, 0)),
                  pl.BlockSpec(kwp.shape, lambda qi: (0, 0), pipeline_mode=once),
                  pl.BlockSpec(vwp.shape, lambda qi: (0, 0), pipeline_mode=once)],
        out_specs=pl.BlockSpec((tq, HC * DHC), lambda qi: (qi, 0)),
        out_shape=jax.ShapeDtypeStruct((t, HC * DHC), F32),
        scratch_shapes=[pltpu.VMEM((HC, tq, WINDOW + tq), F32)],
        compiler_params=_params("arbitrary"),
        name="nsa_window",
    )(tab, q, kwp, vwp)


def _layer_ab(x, tab, w_in, conv_w, conv_b, w_rg, b_rg, w_ig, b_ig, lam, w_out, g_pre, g_post):
    t = x.shape[0]
    da = HA * DHA
    wq, wk, wv = (w_in[:, i * da:(i + 1) * da] for i in range(3))
    wx = w_in[:, 3 * da:]
    wk3 = wk.reshape(D, HA, DHA)
    wv3 = wv.reshape(D, HA, DHA)
    zeros = jnp.zeros_like(wk3)
    even = (jnp.arange(HA) % 2 == 0)[None, :, None]
    wk_aug = jnp.concatenate([jnp.where(even, wk3, zeros), jnp.where(even, zeros, wk3)], axis=-1)
    wv_aug = jnp.concatenate([wv3, zeros], axis=-1)
    w = jnp.concatenate([wq, wk, wx, wk_aug.reshape(D, HA * 128), wv_aug.reshape(D, HA * 128)],
                        axis=1).astype(BF16)
    segs = _segments([(da, None), (da, None), (LRU_W, None), (LRU_W, None),
                      (HA * 128, ("onehot", 128, MOBA_BLOCK, True)), (HA * 128, ("ones", 128, DHA))])
    q, k, xr, gr, kaug, vaug = _rms_proj(x, g_pre, w, segs, [F32, F32, F32, F32, BF16, BF16])

    nb = t // MOBA_BLOCK
    kmean = jnp.pad(_block_mean(k), ((0, DHA - nb), (0, 0)))
    o_a = _moba(tab, q, jnp.concatenate([kmean, kmean], axis=0), kaug, vaug)

    eye = jnp.eye(w_rg.shape[0], dtype=F32)
    blockdiag = lambda wb: jnp.einsum('bij,bc->bicj', wb, eye).reshape(LRU_W, LRU_W).astype(BF16)
    o_b = _rglru(xr, gr, conv_w, conv_b, blockdiag(w_rg), b_rg, blockdiag(w_ig), b_ig, lam)
    wo = w_out.astype(BF16)
    return _outproj_ab(o_a, o_b, wo[:da], wo[da:], g_post, x)


def _layer_c(x, tab, w_in, pos_k, pos_v, k_w1, k_w2, v_w1, v_w2, w_out, g_pre, g_post):
    t = x.shape[0]
    dq = HC * DHC
    ckv = GC * DHC
    cols = lambda j: w_in[:, dq + j * ckv: dq + (j + 1) * ckv]
    widen = lambda wc: jnp.concatenate([wc.reshape(D, GC, DHC), jnp.zeros((D, GC, DHC), wc.dtype)],
                                       axis=-1).reshape(D, GC * 2 * DHC)
    wg = jnp.pad(w_in[:, dq + 6 * ckv:], ((0, 0), (0, 128 - 3 * HC)))
    w = jnp.concatenate([w_in[:, :dq], cols(0), cols(1), widen(cols(2)), widen(cols(3)), cols(4), cols(5), wg],
                        axis=1).astype(BF16)
    segs = _segments([(dq, None), (DHC, None), (DHC, None), (DHC, None), (DHC, None),
                      (GC * 2 * DHC, ("onehot", 2 * DHC, SEL_BLOCK, False)),
                      (GC * 2 * DHC, ("ones", 2 * DHC, DHC)),
                      (ckv, None), (ckv, None), (128, None)])
    q, kc0, kc1, vc0, vc1, ks_aug, vs_aug, kw, vw, gl = _rms_proj(
        x, g_pre, w, segs, [F32, F32, F32, F32, F32, BF16, BF16, BF16, BF16, F32])

    n = t // CMP_STRIDE
    hm = lambda a0, a1: jnp.stack([a0.reshape(n, CMP_STRIDE * DHC), a1.reshape(n, CMP_STRIDE * DHC)])
    kc = _compress(hm(kc0, kc1), pos_k, k_w1.astype(BF16), k_w2.astype(BF16))
    vc = _compress(hm(vc0, vc1), pos_v, v_w1.astype(BF16), v_w2.astype(BF16))

    ns = t // SEL_BLOCK
    nsp = -(-ns // 128) * 128
    cs = jnp.arange(n)[:, None] * CMP_STRIDE
    ss = jnp.arange(nsp)[None, :] * SEL_BLOCK
    overlap = ((cs < ss + SEL_BLOCK) & (cs + CMP_LEN > ss) & (jnp.arange(n)[:, None] < n - 1)
               & (jnp.arange(nsp)[None, :] < ns)).astype(BF16)
    o_c, selbias = _nsa_cmp(tab, q, kc, vc, overlap, ns)
    o_s = _nsa_sel(tab, q, selbias, ks_aug, vs_aug)
    padw = lambda a: jnp.pad(a, ((WINDOW, 0), (0, 0)))
    o_w = _nsa_win(tab, q, padw(kw), padw(vw))
    return _outproj_c(o_c, o_s, o_w, gl, w_out.astype(BF16), g_post, x)


def kernel(x, rel_bias, ab_w_in, ab_conv_w, ab_conv_b, ab_w_rg, ab_b_rg, ab_w_ig, ab_b_ig, ab_lru_lambda, ab_w_out, c_w_in, c_cmp_pos_k, c_cmp_pos_v, c_cmp_k_w1, c_cmp_k_w2, c_cmp_v_w1, c_cmp_v_w2, c_w_out, norm_mix_pre, norm_mix_post, norm_mlp_pre, norm_mlp_post, w_up, w_down):
    batch = x.shape[0]
    depth = norm_mix_pre.shape[0]
    outs = []
    for b in range(batch):
        xb = x[b]
        for layer in range(depth):
            i = layer // 2
            if layer % 2 == 0:
                xb = _layer_ab(xb, rel_bias, ab_w_in[i], ab_conv_w[i], ab_conv_b[i], ab_w_rg[i], ab_b_rg[i],
                               ab_w_ig[i], ab_b_ig[i], ab_lru_lambda[i], ab_w_out[i],
                               norm_mix_pre[layer], norm_mix_post[layer])
            else:
                xb = _layer_c(xb, rel_bias, c_w_in[i], c_cmp_pos_k[i], c_cmp_pos_v[i], c_cmp_k_w1[i],
                              c_cmp_k_w2[i], c_cmp_v_w1[i], c_cmp_v_w2[i], c_w_out[i],
                              norm_mix_pre[layer], norm_mix_post[layer])
            xb = _mlp(xb, norm_mlp_pre[layer], w_up[layer].astype(BF16), w_down[layer].astype(BF16),
                      norm_mlp_post[layer])
        outs.append(xb)
    return jnp.stack(outs)
```

```python
import functools
import math

import numpy as np
import jax
import jax.numpy as jnp
from jax import lax
from jax.experimental import pallas as pl
from jax.experimental.pallas import tpu as pltpu

F32 = jnp.float32
BF16 = jnp.bfloat16
HIGHEST = lax.Precision.HIGHEST
LOG2E = 1.4426950408889634

D = 1024
EPS = 1e-6
NEG = -1e30
N_BUCKETS = 32
MAX_DISTANCE = 128

HA = 8
DHA = 64
MOBA_BLOCK = 256
MOBA_TOPK = 3
LRU_W = 512
LRU_C = 8.0
CONV_W = 4
HC = 8
DHC = 128
GC = 2
PC = 4
CMP_LEN = 32
CMP_STRIDE = 16
SEL_BLOCK = 64
SEL_TOPK = 16
WINDOW = 512
SEL_FORCE = 1e4
D_FF = 4096

TILE = 256
MOBA_UNROLL = 8
MOBA_HEADS = 4
SEL_UNROLL = 4
SEL_CHUNKS = 2
CMP_TQ = 256
CMP_BIAS_LANES = 32
CMP_MAX_FLOOR = -1e20
VMEM_LIMIT = 56 * 1024 * 1024


def _bucket_np(d):
    n = np.maximum(d, 0)
    exact = N_BUCKETS // 2
    nf = np.maximum(n, 1).astype(np.float32)
    large = exact + (np.log(nf / np.float32(exact)) / np.float32(math.log(MAX_DISTANCE / exact))
                     * np.float32(N_BUCKETS - exact)).astype(np.int32)
    return np.where(n < exact, n, np.minimum(large, N_BUCKETS - 1))


_THR = [int(np.argmax(_bucket_np(np.arange(0, 4 * MAX_DISTANCE)) >= k)) for k in range(1, N_BUCKETS)]
assert _THR[-1] <= MAX_DISTANCE


def _bias_delta(d, tab_ref, h):
    acc = jnp.full(d.shape, tab_ref[0, h] - tab_ref[N_BUCKETS - 1, h], F32)
    for k in range(1, N_BUCKETS):
        acc = acc + jnp.where(d >= _THR[k - 1], tab_ref[k, h] - tab_ref[k - 1, h], 0.0)
    return acc


def _dot_t(a, b, **kw):
    return lax.dot_general(a, b, (((1,), (1,)), ((), ())), preferred_element_type=F32, **kw)


def _gelu_tanh(x):
    return 0.5 * x * (1.0 + jnp.tanh(0.7978845608028654 * (x + 0.044715 * x * x * x)))


def _params(*sem):
    return pltpu.CompilerParams(dimension_semantics=sem, vmem_limit_bytes=VMEM_LIMIT)


def _smem():
    return pl.BlockSpec(memory_space=pltpu.SMEM)


def _proj_kernel(x_ref, g_ref, w_ref, *out_refs, segs, tm):
    x = x_ref[...]
    ms = jnp.mean(x * x, axis=-1, keepdims=True)
    hn = (x * lax.rsqrt(ms + EPS) * g_ref[...]).astype(BF16)
    row0 = pl.program_id(0) * tm
    for o_ref, (start, width, aug) in zip(out_refs, segs):
        y = jnp.dot(hn, w_ref[:, start:start + width], preferred_element_type=F32)
        if aug is not None:
            lane = lax.broadcasted_iota(jnp.int32, y.shape, 1)
            if aug[0] == "onehot":
                _, group, rows_per_block, lo_first = aug
                half = group // 2
                row = row0 + lax.broadcasted_iota(jnp.int32, y.shape, 0)
                which_half = (lane // half) % 2
                data_half = (lane // group) % 2 if lo_first else jnp.zeros_like(lane)
                blk = (row // rows_per_block) % half
                y = jnp.where((which_half != data_half) & (lane % half == blk), 1.0, y)
            else:
                _, group, col = aug
                y = jnp.where(lane % group == col, 1.0, y)
        o_ref[...] = y.astype(o_ref.dtype)


def _rms_proj(x, g, w, segs, dtypes, tm=512):
    t = x.shape[0]
    return pl.pallas_call(
        functools.partial(_proj_kernel, segs=segs, tm=tm),
        grid=(t // tm,),
        in_specs=[pl.BlockSpec((tm, D), lambda i: (i, 0)),
                  pl.BlockSpec((1, D), lambda i: (0, 0)),
                  pl.BlockSpec(w.shape, lambda i: (0, 0))],
        out_specs=[pl.BlockSpec((tm, wd), lambda i: (i, 0)) for (_, wd, _) in segs],
        out_shape=[jax.ShapeDtypeStruct((t, wd), dt) for (_, wd, _), dt in zip(segs, dtypes)],
        compiler_params=_params("arbitrary"),
        name="rms_proj",
    )(x, g.reshape(1, D), w)


def _segments(widths_augs):
    segs, off = [], 0
    for wd, aug in widths_augs:
        segs.append((off, wd, aug))
        off += wd
    return segs


def _topk_mask(r, lane, k):
    sel = jnp.zeros(r.shape, jnp.bool_)
    lanef = lane.astype(F32)
    for _ in range(k):
        mx = jnp.max(r, axis=1, keepdims=True)
        first = jnp.min(jnp.where(r == mx, lanef, 1e9), axis=1, keepdims=True)
        pick = lanef == first
        sel = sel | (pick & (mx > NEG / 2))
        r = jnp.where(pick, NEG, r)
    return sel


def _removed_topk(r0, k):
    r = r0
    for _ in range(k):
        r = jnp.where(r == jnp.max(r, axis=1, keepdims=True), NEG, r)
    return (r0 > NEG / 2) & (r < NEG / 2)


def _near_bias_tiles(f_ref, tab_ref, heads):
    d0 = (lax.broadcasted_iota(jnp.int32, (TILE, TILE), 0)
          - lax.broadcasted_iota(jnp.int32, (TILE, TILE), 1))
    for i, h in enumerate(heads):
        f_ref[i, 0] = jnp.where(d0 >= 0, LOG2E * _bias_delta(jnp.maximum(d0, 0), tab_ref, h), NEG)
        f_ref[i, 1] = LOG2E * _bias_delta(d0 + TILE, tab_ref, h)


def _slot_bias(j, qi, lo, f_diag, f_prev):
    other = jnp.where((j >= lo) & (j <= qi), 0.0, NEG)
    return jnp.where(j == qi, f_diag, jnp.where(j == qi - 1, f_prev, other))


def _flash_update(s, vt, m_ref, acc_ref, first):
    rm = jnp.max(s, axis=1, keepdims=True)
    if first:
        m_ref[...] = jnp.broadcast_to(rm, m_ref.shape)
        p = jnp.exp2(s - rm)
        acc_ref[...] = jnp.dot(p.astype(BF16), vt, preferred_element_type=F32)
        return
    m_old = m_ref[...]
    m_new = jnp.maximum(m_old, rm)
    alpha = jnp.exp2(m_old - m_new)
    p = jnp.exp2(s - jnp.concatenate([m_new] * (s.shape[1] // m_new.shape[1]), axis=1))
    pv = jnp.dot(p.astype(BF16), vt, preferred_element_type=F32)
    acc_ref[...] = jnp.concatenate([alpha] * (pv.shape[1] // alpha.shape[1]), axis=1) * acc_ref[...] + pv
    m_ref[...] = m_new


def _blockmean_kernel(k_ref, o_ref):
    k = k_ref[...]
    o_ref[...] = jnp.mean(k.reshape(8, MOBA_BLOCK, k.shape[-1]), axis=1)


def _block_mean(k):
    t, w = k.shape
    nb = t // MOBA_BLOCK
    return pl.pallas_call(
        _blockmean_kernel,
        grid=(nb // 8,),
        in_specs=[pl.BlockSpec((8 * MOBA_BLOCK, w), lambda i: (i, 0))],
        out_specs=pl.BlockSpec((8, w), lambda i: (i, 0)),
        out_shape=jax.ShapeDtypeStruct((nb, w), F32),
        compiler_params=_params("arbitrary"),
        name="moba_block_mean",
    )(k)


def _moba_kernel(tab_ref, q_ref, km_ref, ka_ref, va_ref, o_ref, f_ref, m_ref, acc_ref, qa_ref, *, nb, unroll,
                 heads):
    h0 = pl.program_id(0) * heads
    qi = pl.program_id(1)
    scale = DHA ** -0.5 * LOG2E
    lane = lax.broadcasted_iota(jnp.int32, (TILE, 128), 1)

    @pl.when(qi == 0)
    def _():
        _near_bias_tiles(f_ref, tab_ref, [h0 + hh for hh in range(heads)])

    def augmented_query(q, head_lanes, in_aug, blk, sel):
        keep = sel | (in_aug & (blk == qi))
        return jnp.where(head_lanes, q * scale, jnp.where(keep, 0.0, NEG)).astype(BF16)

    fallbacks = []
    for hh in range(heads):
        pair = slice(hh // 2 * 128, (hh // 2 + 1) * 128)
        q = q_ref[:, pair]
        head_lanes = (lane < DHA) if hh % 2 == 0 else (lane >= DHA)
        qh = jnp.where(head_lanes, q, 0.0)
        route = _dot_t(qh, km_ref[:, pair], precision=HIGHEST)
        blk = lane - DHA if hh % 2 == 0 else lane
        in_aug = jnp.logical_not(head_lanes)
        cand = in_aug & (blk < qi)
        r = jnp.where(cand, route, NEG)
        sel = _removed_topk(r, MOBA_TOPK)
        qa_ref[hh] = augmented_query(q, head_lanes, in_aug, blk, sel)
        count = jnp.sum(jnp.where(sel, 1.0, 0.0), axis=1, keepdims=True)
        ncand = jnp.sum(jnp.where(cand, 1.0, 0.0), axis=1, keepdims=True)
        tied = jnp.max(jnp.where(count != jnp.minimum(ncand, float(MOBA_TOPK)), 1.0, 0.0))
        fallbacks.append((hh, tied, (q, head_lanes, in_aug, blk), r))

    for hh, tied, args, r in fallbacks:
        @pl.when(tied > 0.0)
        def _(hh=hh, args=args, r=r):
            qa_ref[hh] = augmented_query(*args, _topk_mask(r, lane, MOBA_TOPK))

    def tiles(j, hh, n):
        off = pl.multiple_of(j * TILE, TILE)
        sl = slice(hh * 128, (hh + 1) * 128)
        return _dot_t(qa_ref[hh], ka_ref[pl.ds(off, n * TILE), sl]), va_ref[pl.ds(off, n * TILE), sl]

    far_tiles = unroll * (jnp.maximum(qi - 1, 0) // unroll)
    start = jnp.minimum(far_tiles, nb - (unroll + 1))
    for hh in range(heads):
        s, vt = tiles(start, hh, unroll + 1)
        bias = jnp.concatenate([_slot_bias(start + u, qi, far_tiles, f_ref[hh, 0], f_ref[hh, 1])
                                for u in range(unroll + 1)], axis=1)
        _flash_update(s + bias, vt, m_ref.at[hh], acc_ref.at[hh], True)

    def far(i, c):
        for hh in range(heads):
            s, vt = tiles(i * unroll, hh, unroll)
            _flash_update(s, vt, m_ref.at[hh], acc_ref.at[hh], False)
        return c

    lax.fori_loop(0, far_tiles // unroll, far, 0)
    for pp in range(heads // 2):
        outs = []
        for hh in (2 * pp, 2 * pp + 1):
            acc = acc_ref[hh]
            outs.append(acc / acc[:, DHA:DHA + 1])
        o_ref[:, pp * 128:(pp + 1) * 128] = jnp.where(lane < DHA, outs[0], pltpu.roll(outs[1], DHA, 1))


def _moba(tab, q, kmean2, kaug, vaug, heads=MOBA_HEADS):
    t = q.shape[0]
    once = pl.Buffered(1)
    return pl.pallas_call(
        functools.partial(_moba_kernel, nb=t // TILE, unroll=min(MOBA_UNROLL, t // TILE - 1), heads=heads),
        grid=(HA // heads, t // TILE),
        in_specs=[_smem(),
                  pl.BlockSpec((TILE, heads * DHA), lambda hg, qi: (qi, hg)),
                  pl.BlockSpec((128, heads * DHA), lambda hg, qi: (0, hg)),
                  pl.BlockSpec((t, heads * 128), lambda hg, qi: (0, hg), pipeline_mode=once),
                  pl.BlockSpec((t, heads * 128), lambda hg, qi: (0, hg), pipeline_mode=once)],
        out_specs=pl.BlockSpec((TILE, heads * DHA), lambda hg, qi: (qi, hg)),
        out_shape=jax.ShapeDtypeStruct((t, HA * DHA), F32),
        scratch_shapes=[pltpu.VMEM((heads, 2, TILE, TILE), F32),
                        pltpu.VMEM((heads, TILE, 128), F32),
                        pltpu.VMEM((heads, TILE, 128), F32),
                        pltpu.VMEM((heads, TILE, 128), BF16)],
        compiler_params=_params("arbitrary", "arbitrary"),
        name="moba_attention",
    )(tab, q, kmean2, kaug, vaug)


def _rglru_kernel(xr_ref, gr_ref, cw_ref, cb_ref, wrg_ref, brg_ref, wig_ref, big_ref, lam_ref,
                  o_ref, xbuf, a_s, b_s, hcar, *, tt):
    @pl.when(pl.program_id(0) == 0)
    def _():
        xbuf[0:8, :] = jnp.zeros((8, LRU_W), F32)
        hcar[...] = jnp.zeros_like(hcar)

    xbuf[8:8 + tt, :] = xr_ref[...]
    xc = cb_ref[...]
    for j in range(CONV_W):
        xc = xc + xbuf[8 - (CONV_W - 1) + j:8 - (CONV_W - 1) + j + tt, :] * cw_ref[j:j + 1, :]
    xbuf[0:8, :] = xbuf[tt:tt + 8, :]

    xcb = xc.astype(BF16)
    r = jax.nn.sigmoid(jnp.dot(xcb, wrg_ref[...], preferred_element_type=F32) + brg_ref[...])
    ig = jax.nn.sigmoid(jnp.dot(xcb, wig_ref[...], preferred_element_type=F32) + big_ref[...])
    z = -lam_ref[...]
    softplus = jnp.maximum(z, 0.0) + jnp.log(1.0 + jnp.exp(-jnp.abs(z)))
    a = jnp.exp(-LRU_C * r * softplus)
    a_s[...] = a
    b_s[...] = jnp.sqrt(1.0 - a * a) * ig * xc

    row = lax.broadcasted_iota(jnp.int32, (8, LRU_W), 0)

    def body(gi, h):
        off = pl.multiple_of(gi * 8, 8)
        a8 = a_s[pl.ds(off, 8), :]
        b8 = b_s[pl.ds(off, 8), :]
        for s in (1, 2, 4):
            a_sh = jnp.where(row >= s, pltpu.roll(a8, s, 0), 1.0)
            b_sh = jnp.where(row >= s, pltpu.roll(b8, s, 0), 0.0)
            b8 = a8 * b_sh + b8
            a8 = a8 * a_sh
        h8 = b8 + a8 * h
        b_s[pl.ds(off, 8), :] = h8
        return h8[7:8, :]

    hcar[0:1, :] = lax.fori_loop(0, tt // 8, body, hcar[0:1, :])
    o_ref[...] = b_s[...] * _gelu_tanh(gr_ref[...])


def _rglru(xr, gr, conv_w, conv_b, wrg, b_rg, wig, b_ig, lam, tt=512):
    t = xr.shape[0]
    row = lambda a: a.reshape(1, LRU_W)
    full = lambda shape: pl.BlockSpec(shape, lambda i: (0, 0))
    return pl.pallas_call(
        functools.partial(_rglru_kernel, tt=tt),
        grid=(t // tt,),
        in_specs=[pl.BlockSpec((tt, LRU_W), lambda i: (i, 0)),
                  pl.BlockSpec((tt, LRU_W), lambda i: (i, 0)),
                  full((CONV_W, LRU_W)), full((1, LRU_W)),
                  full((LRU_W, LRU_W)), full((1, LRU_W)),
                  full((LRU_W, LRU_W)), full((1, LRU_W)), full((1, LRU_W))],
        out_specs=pl.BlockSpec((tt, LRU_W), lambda i: (i, 0)),
        out_shape=jax.ShapeDtypeStruct((t, LRU_W), F32),
        scratch_shapes=[pltpu.VMEM((tt + 8, LRU_W), F32),
                        pltpu.VMEM((tt, LRU_W), F32),
                        pltpu.VMEM((tt, LRU_W), F32),
                        pltpu.VMEM((8, LRU_W), F32)],
        compiler_params=_params("arbitrary"),
        name="rglru",
    )(xr, gr, conv_w, row(conv_b), wrg, row(b_rg), wig, row(b_ig), row(lam))


def _post(y, g_ref, x_ref, o_ref):
    ms = jnp.mean(y * y, axis=-1, keepdims=True)
    o_ref[...] = x_ref[...] + y * lax.rsqrt(ms + EPS) * g_ref[...]


def _outproj_ab_kernel(a1_ref, a2_ref, w1_ref, w2_ref, g_ref, x_ref, o_ref):
    y = (jnp.dot(a1_ref[...].astype(BF16), w1_ref[...], preferred_element_type=F32)
         + jnp.dot(a2_ref[...].astype(BF16), w2_ref[...], preferred_element_type=F32))
    _post(y, g_ref, x_ref, o_ref)


def _outproj_ab(a1, a2, w1, w2, g, x, tm=512):
    t = x.shape[0]
    rows = lambda w: pl.BlockSpec((tm, w), lambda i: (i, 0))
    full = lambda shape: pl.BlockSpec(shape, lambda i: (0, 0))
    return pl.pallas_call(
        _outproj_ab_kernel,
        grid=(t // tm,),
        in_specs=[rows(a1.shape[1]), rows(a2.shape[1]), full(w1.shape), full(w2.shape),
                  full((1, D)), rows(D)],
        out_specs=rows(D),
        out_shape=jax.ShapeDtypeStruct((t, D), F32),
        compiler_params=_params("arbitrary"),
        name="outproj_ab",
    )(a1, a2, w1, w2, g.reshape(1, D), x)


def _outproj_c_kernel(oc_ref, os_ref, ow_ref, gl_ref, w_ref, g_ref, x_ref, o_ref):
    gates = jax.nn.sigmoid(gl_ref[...])
    parts = []
    for h in range(HC):
        sl = slice(h * DHC, (h + 1) * DHC)
        parts.append(gates[:, 3 * h:3 * h + 1] * oc_ref[:, sl]
                     + gates[:, 3 * h + 1:3 * h + 2] * os_ref[:, sl]
                     + gates[:, 3 * h + 2:3 * h + 3] * ow_ref[:, sl])
    o = jnp.concatenate(parts, axis=1).astype(BF16)
    _post(jnp.dot(o, w_ref[...], preferred_element_type=F32), g_ref, x_ref, o_ref)


def _outproj_c(oc, os_, ow, gl, w, g, x, tm=512):
    t = x.shape[0]
    rows = lambda w_: pl.BlockSpec((tm, w_), lambda i: (i, 0))
    full = lambda shape: pl.BlockSpec(shape, lambda i: (0, 0))
    return pl.pallas_call(
        _outproj_c_kernel,
        grid=(t // tm,),
        in_specs=[rows(D), rows(D), rows(D), rows(128), full(w.shape), full((1, D)), rows(D)],
        out_specs=rows(D),
        out_shape=jax.ShapeDtypeStruct((t, D), F32),
        compiler_params=_params("arbitrary"),
        name="outproj_c",
    )(oc, os_, ow, gl, w, g.reshape(1, D), x)


def _mlp_kernel(x_ref, gpre_ref, wup_ref, wdn_ref, gpost_ref, o_ref, h_s, acc_s):
    c = pl.program_id(1)

    @pl.when(c == 0)
    def _():
        x = x_ref[...]
        ms = jnp.mean(x * x, axis=-1, keepdims=True)
        h_s[...] = (x * lax.rsqrt(ms + EPS) * gpre_ref[...]).astype(BF16)
        acc_s[...] = jnp.zeros_like(acc_s)

    u = jnp.maximum(jnp.dot(h_s[...], wup_ref[...], preferred_element_type=F32), 0.0)
    acc_s[...] += jnp.dot((u * u).astype(BF16), wdn_ref[...], preferred_element_type=F32)

    @pl.when(c == pl.num_programs(1) - 1)
    def _():
        _post(acc_s[...], gpost_ref, x_ref, o_ref)


def _mlp(x, gpre, wup, wdn, gpost, tm=1024, tf=512):
    t = x.shape[0]
    return pl.pallas_call(
        _mlp_kernel,
        grid=(t // tm, D_FF // tf),
        in_specs=[pl.BlockSpec((tm, D), lambda i, c: (i, 0)),
                  pl.BlockSpec((1, D), lambda i, c: (0, 0)),
                  pl.BlockSpec((D, tf), lambda i, c: (0, c)),
                  pl.BlockSpec((tf, D), lambda i, c: (c, 0)),
                  pl.BlockSpec((1, D), lambda i, c: (0, 0))],
        out_specs=pl.BlockSpec((tm, D), lambda i, c: (i, 0)),
        out_shape=jax.ShapeDtypeStruct((t, D), F32),
        scratch_shapes=[pltpu.VMEM((tm, D), BF16), pltpu.VMEM((tm, D), F32)],
        compiler_params=_params("arbitrary", "arbitrary"),
        name="mlp",
    )(x, gpre.reshape(1, D), wup, wdn, gpost.reshape(1, D))


def _compress_kernel(h_ref, pos_ref, w1_ref, w2_ref, o_ref):
    half = CMP_STRIDE * DHC
    hm = h_ref[0]
    top = jnp.dot((hm + pos_ref[:, 0:half]).astype(BF16), w1_ref[0:half, :], preferred_element_type=F32)
    bot = jnp.dot((hm + pos_ref[:, half:2 * half]).astype(BF16), w1_ref[half:2 * half, :],
                  preferred_element_type=F32)
    hid = top + pltpu.roll(bot, bot.shape[0] - 1, 0)
    o_ref[0] = jnp.dot(_gelu_tanh(hid).astype(BF16), w2_ref[...], preferred_element_type=F32).astype(o_ref.dtype)


def _compress(hmat, pos, w1, w2):
    g, n, wd = hmat.shape
    return pl.pallas_call(
        _compress_kernel,
        grid=(g,),
        in_specs=[pl.BlockSpec((1, n, wd), lambda i: (i, 0, 0)),
                  pl.BlockSpec((1, 2 * wd), lambda i: (0, 0)),
                  pl.BlockSpec(w1.shape, lambda i: (0, 0)),
                  pl.BlockSpec(w2.shape, lambda i: (0, 0))],
        out_specs=pl.BlockSpec((1, n, DHC), lambda i: (i, 0, 0)),
        out_shape=jax.ShapeDtypeStruct((g, n, DHC), BF16),
        compiler_params=_params("arbitrary"),
        name="nsa_compress",
    )(hmat, pos.reshape(1, 2 * wd), w1, w2)


def _stack_heads(q):
    return jnp.concatenate([q[:, p * DHC:(p + 1) * DHC] for p in range(PC)], axis=0)


def _nsa_cmp_kernel(tab_ref, q_ref, kc_ref, vc_ref, ov_ref, oc_ref, sb_ref, e_ref, qb_ref, *, tq, ns):
    qi = pl.program_id(0)
    ncp = kc_ref.shape[1]
    nsp = ov_ref.shape[1]
    scale = DHC ** -0.5 * LOG2E
    cpq = tq // CMP_STRIDE
    k = min(SEL_TOPK, ns)

    @pl.when(qi == 0)
    def _():
        u = lax.broadcasted_iota(jnp.int32, (2 * ncp, 128), 0)
        lane = lax.broadcasted_iota(jnp.int32, (2 * ncp, 128), 1)
        hit = (lane < 2 * CMP_BIAS_LANES) & (u + lane % CMP_BIAS_LANES == ncp - 1)
        after = (lane == 2 * CMP_BIAS_LANES) & (u >= ncp)
        e_ref[...] = (jnp.where(hit, 1.0, 0.0) + jnp.where(after, 1.0, 0.0)).astype(BF16)
        i = lax.broadcasted_iota(jnp.int32, (tq, 128), 0)
        lane = lax.broadcasted_iota(jnp.int32, (tq, 128), 1)
        d = i + CMP_STRIDE * (lane % CMP_BIAS_LANES) - (tq - CMP_STRIDE + CMP_LEN - 1)
        for h in range(HC):
            b = jnp.where(d >= 0, LOG2E * _bias_delta(jnp.maximum(d, 0), tab_ref, h), NEG)
            hi = b.astype(BF16).astype(F32)
            qb = jnp.where(lane < CMP_BIAS_LANES, hi, jnp.where(lane < 2 * CMP_BIAS_LANES, b - hi, 0.0))
            qb_ref[h] = jnp.where(lane == 2 * CMP_BIAS_LANES, NEG, qb).astype(BF16)

    off = pl.multiple_of(ncp - cpq * (qi + 1), 16)
    onehots = e_ref[pl.ds(off, ncp), :]
    jj = lax.broadcasted_iota(jnp.int32, (tq, nsp), 1)
    cur = (qi * tq + lax.broadcasted_iota(jnp.int32, (tq, nsp), 0)) // SEL_BLOCK
    forced = (jj == 0) | (jj == cur) | (jj == cur - 1)
    dead = (jj > cur) | (jj >= ns)
    q = q_ref[...]
    fallbacks = []
    for g in range(GC):
        heads = range(g * PC, (g + 1) * PC)
        qa = jnp.concatenate(
            [jnp.concatenate([(q[:, h * DHC:(h + 1) * DHC] * scale).astype(BF16) for h in heads], axis=0),
             jnp.concatenate([qb_ref[h] for h in heads], axis=0)], axis=1)
        s_all = _dot_t(qa, jnp.concatenate([kc_ref[g], onehots], axis=1))
        psum = jnp.zeros((tq, ncp), F32)
        for p, h in enumerate(heads):
            s = s_all[p * tq:(p + 1) * tq]
            m = jnp.maximum(jnp.max(s, axis=1, keepdims=True), CMP_MAX_FLOOR)
            e = jnp.exp2(s - m)
            pc = e * (1.0 / jnp.maximum(jnp.sum(e, axis=1, keepdims=True), 1e-30))
            oc_ref[:, h * DHC:(h + 1) * DHC] = jnp.dot(pc.astype(BF16), vc_ref[g], preferred_element_type=F32)
            psum = psum + pc

        hi = psum.astype(BF16)
        lo = (psum - hi.astype(F32)).astype(BF16)
        imp = (jnp.dot(hi, ov_ref[...], preferred_element_type=F32)
               + jnp.dot(lo, ov_ref[...], preferred_element_type=F32))
        rest = jnp.where(forced | dead, NEG, imp)
        sel_rest = _removed_topk(rest, k - 3)
        sel = sel_rest | (forced & jnp.logical_not(dead))
        sb_ref[g] = jnp.where(sel, 0.0, NEG).astype(BF16)
        count = jnp.sum(jnp.where(sel_rest, 1.0, 0.0), axis=1, keepdims=True)
        cand = jnp.sum(jnp.where(forced | dead, 0.0, 1.0), axis=1, keepdims=True)
        tied = jnp.max(jnp.where(count != jnp.minimum(cand, float(k - 3)), 1.0, 0.0))
        fallbacks.append((g, imp, tied))

    for g, imp, tied in fallbacks:
        @pl.when(tied > 0.0)
        def _(g=g, imp=imp):
            full = jnp.where(dead, NEG, jnp.where(forced, imp + SEL_FORCE, imp))
            sb_ref[g] = jnp.where(_topk_mask(full, jj, k), 0.0, NEG).astype(BF16)


def _nsa_cmp(tab, q, kc, vc, overlap, ns, tq=CMP_TQ):
    t = q.shape[0]
    ncp = kc.shape[1]
    nsp = overlap.shape[1]
    assert (tq - CMP_STRIDE + CMP_LEN - 1 + MAX_DISTANCE) // CMP_STRIDE < CMP_BIAS_LANES
    full = lambda shape: pl.BlockSpec(shape, lambda qi: (0,) * len(shape))
    return pl.pallas_call(
        functools.partial(_nsa_cmp_kernel, tq=tq, ns=ns),
        grid=(t // tq,),
        in_specs=[_smem(),
                  pl.BlockSpec((tq, HC * DHC), lambda qi: (qi, 0)),
                  full((GC, ncp, DHC)), full((GC, ncp, DHC)), full((ncp, nsp))],
        out_specs=[pl.BlockSpec((tq, HC * DHC), lambda qi: (qi, 0)),
                   pl.BlockSpec((GC, tq, nsp), lambda qi: (0, qi, 0))],
        out_shape=[jax.ShapeDtypeStruct((t, HC * DHC), F32),
                   jax.ShapeDtypeStruct((GC, t, nsp), BF16)],
        scratch_shapes=[pltpu.VMEM((2 * ncp, 128), BF16), pltpu.VMEM((HC, tq, 128), BF16)],
        compiler_params=_params("arbitrary"),
        name="nsa_compressed_topk",
    )(tab, q, kc, vc, overlap)


def _nsa_sel_kernel(tab_ref, q_ref, sb_ref, ka_ref, va_ref, o_ref, f_ref, m_ref, acc_ref, *, nhalf, nb, unroll):
    qi = pl.program_id(0)
    scale = DHC ** -0.5 * LOG2E
    tiles_per_half = 128 * SEL_BLOCK // TILE

    @pl.when(qi == 0)
    def _():
        _near_bias_tiles(f_ref, tab_ref, list(range(HC)))

    hpc = PC // SEL_CHUNKS
    chunks = [(g, list(range(g * PC + c * hpc, g * PC + (c + 1) * hpc)))
              for g in range(GC) for c in range(SEL_CHUNKS)]
    q = q_ref[...]
    qas = []
    for g, heads in chunks:
        qc = jnp.concatenate([q[:, h * DHC:(h + 1) * DHC] for h in heads], axis=0)
        qc = (qc * scale).astype(BF16)
        sb = sb_ref[g]
        qas.append([jnp.concatenate([qc, jnp.concatenate([sb[:, hf * 128:(hf + 1) * 128]] * hpc, axis=0)], axis=1)
                    for hf in range(nhalf)])

    def qa_for(c, j):
        if nhalf == 1:
            return qas[c][0]
        return jnp.where(j >= tiles_per_half, qas[c][1], qas[c][0])

    def keys(g, off, n):
        return ka_ref[pl.ds(off, n), g * 2 * DHC:(g + 1) * 2 * DHC]

    def values(g, off, n):
        return va_ref[pl.ds(off, n), g * 2 * DHC:(g + 1) * 2 * DHC]

    far_tiles = unroll * (jnp.maximum(qi - 1, 0) // unroll)
    start = jnp.minimum(far_tiles, nb - (unroll + 1))
    for c, (g, heads) in enumerate(chunks):
        cols = []
        for u in range(unroll + 1):
            j = start + u
            s = _dot_t(qa_for(c, j), keys(g, pl.multiple_of(j * TILE, TILE), TILE))
            cols.append(jnp.concatenate(
                [s[i * TILE:(i + 1) * TILE] + _slot_bias(j, qi, far_tiles, f_ref[h, 0], f_ref[h, 1])
                 for i, h in enumerate(heads)], axis=0))
        vt = values(g, pl.multiple_of(start * TILE, TILE), (unroll + 1) * TILE)
        _flash_update(jnp.concatenate(cols, axis=1), vt, m_ref.at[c], acc_ref.at[c], True)

    steps_per_half = tiles_per_half // unroll
    nsteps = far_tiles // unroll
    for hf in range(nhalf):
        def far(i, carry, hf=hf):
            off = pl.multiple_of(i * (unroll * TILE), unroll * TILE)
            for c, (g, _) in enumerate(chunks):
                s = _dot_t(qas[c][hf], keys(g, off, unroll * TILE))
                _flash_update(s, values(g, off, unroll * TILE), m_ref.at[c], acc_ref.at[c], False)
            return carry
        lo = hf * steps_per_half
        hi = jnp.minimum(nsteps, (hf + 1) * steps_per_half) if hf + 1 < nhalf else nsteps
        lax.fori_loop(lo, jnp.maximum(hi, lo), far, 0)

    for c, (g, heads) in enumerate(chunks):
        acc = acc_ref[c]
        o = acc[:, :DHC] / acc[:, DHC:DHC + 1]
        for i, h in enumerate(heads):
            o_ref[:, h * DHC:(h + 1) * DHC] = o[i * TILE:(i + 1) * TILE]


def _nsa_sel(tab, q, selbias, kaug, vaug):
    t = q.shape[0]
    nsp = selbias.shape[2]
    nchunks = GC * SEL_CHUNKS
    rows = HC // nchunks * TILE
    once = pl.Buffered(1)
    return pl.pallas_call(
        functools.partial(_nsa_sel_kernel, nhalf=nsp // 128, nb=t // TILE,
                          unroll=min(SEL_UNROLL, t // TILE - 1)),
        grid=(t // TILE,),
        in_specs=[_smem(),
                  pl.BlockSpec((TILE, HC * DHC), lambda qi: (qi, 0)),
                  pl.BlockSpec((GC, TILE, nsp), lambda qi: (0, qi, 0)),
                  pl.BlockSpec((t, GC * 2 * DHC), lambda qi: (0, 0), pipeline_mode=once),
                  pl.BlockSpec((t, GC * 2 * DHC), lambda qi: (0, 0), pipeline_mode=once)],
        out_specs=pl.BlockSpec((TILE, HC * DHC), lambda qi: (qi, 0)),
        out_shape=jax.ShapeDtypeStruct((t, HC * DHC), F32),
        scratch_shapes=[pltpu.VMEM((HC, 2, TILE, TILE), F32),
                        pltpu.VMEM((nchunks, rows, 128), F32),
                        pltpu.VMEM((nchunks, rows, 2 * DHC), F32)],
        compiler_params=_params("arbitrary"),
        name="nsa_selected",
    )(tab, q, selbias, kaug, vaug)


def _nsa_win_kernel(tab_ref, q_ref, k_ref, v_ref, o_ref, f_ref, *, tq):
    qi = pl.program_id(0)
    scale = DHC ** -0.5 * LOG2E
    nw = WINDOW + tq
    jj = lax.broadcasted_iota(jnp.int32, (tq, nw), 1)

    @pl.when(qi == 0)
    def _():
        d = WINDOW + lax.broadcasted_iota(jnp.int32, (tq, nw), 0) - jj
        ok = (d >= 0) & (d < WINDOW)
        for h in range(HC):
            f_ref[h] = jnp.where(ok, LOG2E * _bias_delta(jnp.maximum(d, 0), tab_ref, h), NEG)

    start = pl.multiple_of(qi * tq, tq)

    def body(front):
        pos_ok = jnp.where(jj >= WINDOW - start, 0.0, NEG) if front else None
        for h in range(HC):
            g = h // PC
            q = (q_ref[:, h * DHC:(h + 1) * DHC] * scale).astype(BF16)
            s = _dot_t(q, k_ref[pl.ds(start, nw), g * DHC:(g + 1) * DHC]) + f_ref[h]
            if front:
                s = s + pos_ok
            e = jnp.exp2(s - jnp.max(s, axis=1, keepdims=True))
            o = jnp.dot(e.astype(BF16), v_ref[pl.ds(start, nw), g * DHC:(g + 1) * DHC],
                        preferred_element_type=F32)
            o_ref[:, h * DHC:(h + 1) * DHC] = o * (1.0 / jnp.sum(e, axis=1, keepdims=True))

    @pl.when(start < WINDOW)
    def _():
        body(True)

    @pl.when(start >= WINDOW)
    def _():
        body(False)


def _nsa_win(tab, q, kwp, vwp, tq=TILE):
    t = q.shape[0]
    once = pl.Buffered(1)
    return pl.pallas_call(
        functools.partial(_nsa_win_kernel, tq=tq),
        grid=(t // tq,),
        in_specs=[_smem(),
                  pl.BlockSpec((tq, HC * DHC), lambda qi: (qi, 0)),
                  pl.BlockSpec(kwp.shape, lambda qi: (0, 0), pipeline_mode=once),
                  pl.BlockSpec(vwp.shape, lambda qi: (0, 0), pipeline_mode=once)],
        out_specs=pl.BlockSpec((tq, HC * DHC), lambda qi: (qi, 0)),
        out_shape=jax.ShapeDtypeStruct((t, HC * DHC), F32),
        scratch_shapes=[pltpu.VMEM((HC, tq, WINDOW + tq), F32)],
        compiler_params=_params("arbitrary"),
        name="nsa_window",
    )(tab, q, kwp, vwp)


def _layer_ab(x, tab, w_in, conv_w, conv_b, w_rg, b_rg, w_ig, b_ig, lam, w_out, g_pre, g_post):
    t = x.shape[0]
    da = HA * DHA
    wq, wk, wv = (w_in[:, i * da:(i + 1) * da] for i in range(3))
    wx = w_in[:, 3 * da:]
    wk3 = wk.reshape(D, HA, DHA)
    wv3 = wv.reshape(D, HA, DHA)
    zeros = jnp.zeros_like(wk3)
    even = (jnp.arange(HA) % 2 == 0)[None, :, None]
    wk_aug = jnp.concatenate([jnp.where(even, wk3, zeros), jnp.where(even, zeros, wk3)], axis=-1)
    wv_aug = jnp.concatenate([wv3, zeros], axis=-1)
    w = jnp.concatenate([wq, wk, wx, wk_aug.reshape(D, HA * 128), wv_aug.reshape(D, HA * 128)],
                        axis=1).astype(BF16)
    segs = _segments([(da, None), (da, None), (LRU_W, None), (LRU_W, None),
                      (HA * 128, ("onehot", 128, MOBA_BLOCK, True)), (HA * 128, ("ones", 128, DHA))])
    q, k, xr, gr, kaug, vaug = _rms_proj(x, g_pre, w, segs, [F32, F32, F32, F32, BF16, BF16])

    nb = t // MOBA_BLOCK
    kmean = jnp.pad(_block_mean(k), ((0, DHA - nb), (0, 0)))
    o_a = _moba(tab, q, jnp.concatenate([kmean, kmean], axis=0), kaug, vaug)

    eye = jnp.eye(w_rg.shape[0], dtype=F32)
    blockdiag = lambda wb: jnp.einsum('bij,bc->bicj', wb, eye).reshape(LRU_W, LRU_W).astype(BF16)
    o_b = _rglru(xr, gr, conv_w, conv_b, blockdiag(w_rg), b_rg, blockdiag(w_ig), b_ig, lam)
    wo = w_out.astype(BF16)
    return _outproj_ab(o_a, o_b, wo[:da], wo[da:], g_post, x)


def _layer_c(x, tab, w_in, pos_k, pos_v, k_w1, k_w2, v_w1, v_w2, w_out, g_pre, g_post):
    t = x.shape[0]
    dq = HC * DHC
    ckv = GC * DHC
    cols = lambda j: w_in[:, dq + j * ckv: dq + (j + 1) * ckv]
    widen = lambda wc: jnp.concatenate([wc.reshape(D, GC, DHC), jnp.zeros((D, GC, DHC), wc.dtype)],
                                       axis=-1).reshape(D, GC * 2 * DHC)
    wg = jnp.pad(w_in[:, dq + 6 * ckv:], ((0, 0), (0, 128 - 3 * HC)))
    w = jnp.concatenate([w_in[:, :dq], cols(0), cols(1), widen(cols(2)), widen(cols(3)), cols(4), cols(5), wg],
                        axis=1).astype(BF16)
    segs = _segments([(dq, None), (DHC, None), (DHC, None), (DHC, None), (DHC, None),
                      (GC * 2 * DHC, ("onehot", 2 * DHC, SEL_BLOCK, False)),
                      (GC * 2 * DHC, ("ones", 2 * DHC, DHC)),
                      (ckv, None), (ckv, None), (128, None)])
    q, kc0, kc1, vc0, vc1, ks_aug, vs_aug, kw, vw, gl = _rms_proj(
        x, g_pre, w, segs, [F32, F32, F32, F32, F32, BF16, BF16, BF16, BF16, F32])

    n = t // CMP_STRIDE
    hm = lambda a0, a1: jnp.stack([a0.reshape(n, CMP_STRIDE * DHC), a1.reshape(n, CMP_STRIDE * DHC)])
    kc = _compress(hm(kc0, kc1), pos_k, k_w1.astype(BF16), k_w2.astype(BF16))
    vc = _compress(hm(vc0, vc1), pos_v, v_w1.astype(BF16), v_w2.astype(BF16))

    ns = t // SEL_BLOCK
    nsp = -(-ns // 128) * 128
    cs = jnp.arange(n)[:, None] * CMP_STRIDE
    ss = jnp.arange(nsp)[None, :] * SEL_BLOCK
    overlap = ((cs < ss + SEL_BLOCK) & (cs + CMP_LEN > ss) & (jnp.arange(n)[:, None] < n - 1)
               & (jnp.arange(nsp)[None, :] < ns)).astype(BF16)
    o_c, selbias = _nsa_cmp(tab, q, kc, vc, overlap, ns)
    o_s = _nsa_sel(tab, q, selbias, ks_aug, vs_aug)
    padw = lambda a: jnp.pad(a, ((WINDOW, 0), (0, 0)))
    o_w = _nsa_win(tab, q, padw(kw), padw(vw))
    return _outproj_c(o_c, o_s, o_w, gl, w_out.astype(BF16), g_post, x)


def kernel(x, rel_bias, ab_w_in, ab_conv_w, ab_conv_b, ab_w_rg, ab_b_rg, ab_w_ig, ab_b_ig, ab_lru_lambda, ab_w_out, c_w_in, c_cmp_pos_k, c_cmp_pos_v, c_cmp_k_w1, c_cmp_k_w2, c_cmp_v_w1, c_cmp_v_w2, c_w_out, norm_mix_pre, norm_mix_post, norm_mlp_pre, norm_mlp_post, w_up, w_down):
    batch = x.shape[0]
    depth = norm_mix_pre.shape[0]
    outs = []
    for b in range(batch):
        xb = x[b]
        for layer in range(depth):
            i = layer // 2
            if layer % 2 == 0:
                xb = _layer_ab(xb, rel_bias, ab_w_in[i], ab_conv_w[i], ab_conv_b[i], ab_w_rg[i], ab_b_rg[i],
                               ab_w_ig[i], ab_b_ig[i], ab_lru_lambda[i], ab_w_out[i],
                               norm_mix_pre[layer], norm_mix_post[layer])
            else:
                xb = _layer_c(xb, rel_bias, c_w_in[i], c_cmp_pos_k[i], c_cmp_pos_v[i], c_cmp_k_w1[i],
                              c_cmp_k_w2[i], c_cmp_v_w1[i], c_cmp_v_w2[i], c_w_out[i],
                              norm_mix_pre[layer], norm_mix_post[layer])
            xb = _mlp(xb, norm_mlp_pre[layer], w_up[layer].astype(BF16), w_down[layer].astype(BF16),
                      norm_mlp_post[layer])
        outs.append(xb)
    return jnp.stack(outs)
```

```python
import functools
import math

import numpy as np
import jax
import jax.numpy as jnp
from jax import lax
from jax.experimental import pallas as pl
from jax.experimental.pallas import tpu as pltpu

F32 = jnp.float32
BF16 = jnp.bfloat16
HIGHEST = lax.Precision.HIGHEST
LOG2E = 1.4426950408889634

D = 1024
EPS = 1e-6
NEG = -1e30
N_BUCKETS = 32
MAX_DISTANCE = 128

HA = 8
DHA = 64
MOBA_BLOCK = 256
MOBA_TOPK = 3
LRU_W = 512
LRU_C = 8.0
CONV_W = 4
HC = 8
DHC = 128
GC = 2
PC = 4
CMP_LEN = 32
CMP_STRIDE = 16
SEL_BLOCK = 64
SEL_TOPK = 16
WINDOW = 512
SEL_FORCE = 1e4
D_FF = 4096

TILE = 256
MOBA_UNROLL = 4
MOBA_HEADS = 4
SEL_UNROLL = 2
SEL_CHUNKS = 2
CMP_TQ = 256
CMP_BIAS_LANES = 32
CMP_MAX_FLOOR = -1e20
VMEM_LIMIT = 56 * 1024 * 1024
VMEM_LIMIT_RESIDENT = 60 * 1024 * 1024


def _bucket_np(d):
    n = np.maximum(d, 0)
    exact = N_BUCKETS // 2
    nf = np.maximum(n, 1).astype(np.float32)
    large = exact + (np.log(nf / np.float32(exact)) / np.float32(math.log(MAX_DISTANCE / exact))
                     * np.float32(N_BUCKETS - exact)).astype(np.int32)
    return np.where(n < exact, n, np.minimum(large, N_BUCKETS - 1))


_THR = [int(np.argmax(_bucket_np(np.arange(0, 4 * MAX_DISTANCE)) >= k)) for k in range(1, N_BUCKETS)]
assert _THR[-1] <= MAX_DISTANCE


def _bias_delta(d, tab_ref, h):
    acc = jnp.full(d.shape, tab_ref[0, h] - tab_ref[N_BUCKETS - 1, h], F32)
    for k in range(1, N_BUCKETS):
        acc = acc + jnp.where(d >= _THR[k - 1], tab_ref[k, h] - tab_ref[k - 1, h], 0.0)
    return acc


def _dot_t(a, b, **kw):
    return lax.dot_general(a, b, (((1,), (1,)), ((), ())), preferred_element_type=F32, **kw)


def _gelu_tanh(x):
    return 0.5 * x * (1.0 + jnp.tanh(0.7978845608028654 * (x + 0.044715 * x * x * x)))


def _params(*sem, vmem_limit=VMEM_LIMIT):
    return pltpu.CompilerParams(dimension_semantics=sem, vmem_limit_bytes=vmem_limit)


def _smem():
    return pl.BlockSpec(memory_space=pltpu.SMEM)


def _proj_kernel(x_ref, g_ref, w_ref, *out_refs, segs, tm):
    x = x_ref[...]
    ms = jnp.mean(x * x, axis=-1, keepdims=True)
    hn = (x * lax.rsqrt(ms + EPS) * g_ref[...]).astype(BF16)
    row0 = pl.program_id(0) * tm
    for o_ref, (start, width, aug) in zip(out_refs, segs):
        y = jnp.dot(hn, w_ref[:, start:start + width], preferred_element_type=F32)
        if aug is not None:
            lane = lax.broadcasted_iota(jnp.int32, y.shape, 1)
            if aug[0] == "onehot":
                _, group, rows_per_block, lo_first = aug
                half = group // 2
                row = row0 + lax.broadcasted_iota(jnp.int32, y.shape, 0)
                which_half = (lane // half) % 2
                data_half = (lane // group) % 2 if lo_first else jnp.zeros_like(lane)
                blk = (row // rows_per_block) % half
                y = jnp.where((which_half != data_half) & (lane % half == blk), 1.0, y)
            else:
                _, group, col = aug
                y = jnp.where(lane % group == col, 1.0, y)
        o_ref[...] = y.astype(o_ref.dtype)


def _rms_proj(x, g, w, segs, dtypes, tm=512):
    t = x.shape[0]
    return pl.pallas_call(
        functools.partial(_proj_kernel, segs=segs, tm=tm),
        grid=(t // tm,),
        in_specs=[pl.BlockSpec((tm, D), lambda i: (i, 0)),
                  pl.BlockSpec((1, D), lambda i: (0, 0)),
                  pl.BlockSpec(w.shape, lambda i: (0, 0))],
        out_specs=[pl.BlockSpec((tm, wd), lambda i: (i, 0)) for (_, wd, _) in segs],
        out_shape=[jax.ShapeDtypeStruct((t, wd), dt) for (_, wd, _), dt in zip(segs, dtypes)],
        compiler_params=_params("arbitrary"),
        name="rms_proj",
    )(x, g.reshape(1, D), w)


def _segments(widths_augs):
    segs, off = [], 0
    for wd, aug in widths_augs:
        segs.append((off, wd, aug))
        off += wd
    return segs


def _topk_mask(r, lane, k):
    sel = jnp.zeros(r.shape, jnp.bool_)
    lanef = lane.astype(F32)
    for _ in range(k):
        mx = jnp.max(r, axis=1, keepdims=True)
        first = jnp.min(jnp.where(r == mx, lanef, 1e9), axis=1, keepdims=True)
        pick = lanef == first
        sel = sel | (pick & (mx > NEG / 2))
        r = jnp.where(pick, NEG, r)
    return sel


def _removed_topk(r0, k):
    r = r0
    for _ in range(k):
        r = jnp.where(r == jnp.max(r, axis=1, keepdims=True), NEG, r)
    return (r0 > NEG / 2) & (r < NEG / 2)


def _near_bias_tiles(f_ref, tab_ref, heads):
    d0 = (lax.broadcasted_iota(jnp.int32, (TILE, TILE), 0)
          - lax.broadcasted_iota(jnp.int32, (TILE, TILE), 1))
    for i, h in enumerate(heads):
        f_ref[i, 0] = jnp.where(d0 >= 0, LOG2E * _bias_delta(jnp.maximum(d0, 0), tab_ref, h), NEG)
        f_ref[i, 1] = LOG2E * _bias_delta(d0 + TILE, tab_ref, h)


def _slot_bias(j, qi, lo, f_diag, f_prev):
    other = jnp.where((j >= lo) & (j <= qi), 0.0, NEG)
    return jnp.where(j == qi, f_diag, jnp.where(j == qi - 1, f_prev, other))


def _flash_update(s, vt, m_ref, acc_ref, first):
    rm = jnp.max(s, axis=1, keepdims=True)
    if first:
        m_ref[...] = jnp.broadcast_to(rm, m_ref.shape)
        p = jnp.exp2(s - rm)
        acc_ref[...] = jnp.dot(p.astype(BF16), vt, preferred_element_type=F32)
        return
    m_old = m_ref[...]
    m_new = jnp.maximum(m_old, rm)
    alpha = jnp.exp2(m_old - m_new)
    p = jnp.exp2(s - jnp.concatenate([m_new] * (s.shape[1] // m_new.shape[1]), axis=1))
    pv = jnp.dot(p.astype(BF16), vt, preferred_element_type=F32)
    acc_ref[...] = jnp.concatenate([alpha] * (pv.shape[1] // alpha.shape[1]), axis=1) * acc_ref[...] + pv
    m_ref[...] = m_new


def _blockmean_kernel(k_ref, o_ref):
    k = k_ref[...]
    o_ref[...] = jnp.mean(k.reshape(8, MOBA_BLOCK, k.shape[-1]), axis=1)


def _block_mean(k):
    t, w = k.shape
    nb = t // MOBA_BLOCK
    return pl.pallas_call(
        _blockmean_kernel,
        grid=(nb // 8,),
        in_specs=[pl.BlockSpec((8 * MOBA_BLOCK, w), lambda i: (i, 0))],
        out_specs=pl.BlockSpec((8, w), lambda i: (i, 0)),
        out_shape=jax.ShapeDtypeStruct((nb, w), F32),
        compiler_params=_params("arbitrary"),
        name="moba_block_mean",
    )(k)


def _moba_kernel(tab_ref, q_ref, km_ref, ka_ref, va_ref, o_ref, f_ref, m_ref, acc_ref, qa_ref, s_ref, *, nb,
                 unroll, heads):
    h0 = pl.program_id(0) * heads
    qi = pl.program_id(1)
    scale = DHA ** -0.5 * LOG2E
    lane = lax.broadcasted_iota(jnp.int32, (TILE, 128), 1)

    @pl.when(qi == 0)
    def _():
        _near_bias_tiles(f_ref, tab_ref, [h0 + hh for hh in range(heads)])

    def augmented_query(q, head_lanes, in_aug, blk, sel):
        keep = sel | (in_aug & (blk == qi))
        return jnp.where(head_lanes, q * scale, jnp.where(keep, 0.0, NEG)).astype(BF16)

    fallbacks = []
    for hh in range(heads):
        pair = slice(hh // 2 * 128, (hh // 2 + 1) * 128)
        q = q_ref[:, pair]
        head_lanes = (lane < DHA) if hh % 2 == 0 else (lane >= DHA)
        qh = jnp.where(head_lanes, q, 0.0)
        route = _dot_t(qh, km_ref[:, pair], precision=HIGHEST)
        blk = lane - DHA if hh % 2 == 0 else lane
        in_aug = jnp.logical_not(head_lanes)
        cand = in_aug & (blk < qi)
        r = jnp.where(cand, route, NEG)
        sel = _removed_topk(r, MOBA_TOPK)
        qa_ref[hh] = augmented_query(q, head_lanes, in_aug, blk, sel)
        count = jnp.sum(jnp.where(sel, 1.0, 0.0), axis=1, keepdims=True)
        ncand = jnp.sum(jnp.where(cand, 1.0, 0.0), axis=1, keepdims=True)
        tied = jnp.max(jnp.where(count != jnp.minimum(ncand, float(MOBA_TOPK)), 1.0, 0.0))
        fallbacks.append((hh, tied, (q, head_lanes, in_aug, blk), r))

    for hh, tied, args, r in fallbacks:
        @pl.when(tied > 0.0)
        def _(hh=hh, args=args, r=r):
            qa_ref[hh] = augmented_query(*args, _topk_mask(r, lane, MOBA_TOPK))

    def scores(j, hh, n):
        off = pl.multiple_of(j * TILE, TILE)
        return _dot_t(qa_ref[hh], ka_ref[pl.ds(off, n * TILE), hh * 128:(hh + 1) * 128])

    def values(j, hh, n):
        off = pl.multiple_of(j * TILE, TILE)
        return va_ref[pl.ds(off, n * TILE), hh * 128:(hh + 1) * 128]

    far_tiles = unroll * (jnp.maximum(qi - 1, 0) // unroll)
    groups = far_tiles // unroll
    last = jnp.maximum(groups - 1, 0)

    def produce(buf, grp):
        for hh in range(heads):
            s_ref[buf, hh] = scores(grp * unroll, hh, unroll)

    def consume(buf, grp):
        for hh in range(heads):
            _flash_update(s_ref[buf, hh], values(grp * unroll, hh, unroll), m_ref.at[hh], acc_ref.at[hh], False)

    produce(0, 0)
    start = jnp.minimum(far_tiles, nb - (unroll + 1))
    for hh in range(heads):
        bias = jnp.concatenate([_slot_bias(start + u, qi, far_tiles, f_ref[hh, 0], f_ref[hh, 1])
                                for u in range(unroll + 1)], axis=1)
        _flash_update(scores(start, hh, unroll + 1) + bias, values(start, hh, unroll + 1),
                      m_ref.at[hh], acc_ref.at[hh], True)

    def pair(k, c):
        produce(1, 2 * k + 1)
        consume(0, 2 * k)
        produce(0, jnp.minimum(2 * k + 2, last))
        consume(1, 2 * k + 1)
        return c

    lax.fori_loop(0, groups // 2, pair, 0)

    @pl.when(groups % 2 == 1)
    def _():
        consume(0, groups - 1)

    for pp in range(heads // 2):
        outs = []
        for hh in (2 * pp, 2 * pp + 1):
            acc = acc_ref[hh]
            outs.append(acc / acc[:, DHA:DHA + 1])
        o_ref[:, pp * 128:(pp + 1) * 128] = jnp.where(lane < DHA, outs[0], pltpu.roll(outs[1], DHA, 1))


def _moba(tab, q, kmean2, kaug, vaug, heads=MOBA_HEADS):
    t = q.shape[0]
    once = pl.Buffered(1)
    unroll = min(MOBA_UNROLL, t // TILE - 1)
    return pl.pallas_call(
        functools.partial(_moba_kernel, nb=t // TILE, unroll=unroll, heads=heads),
        grid=(HA // heads, t // TILE),
        in_specs=[_smem(),
                  pl.BlockSpec((TILE, heads * DHA), lambda hg, qi: (qi, hg)),
                  pl.BlockSpec((128, heads * DHA), lambda hg, qi: (0, hg)),
                  pl.BlockSpec((t, heads * 128), lambda hg, qi: (0, hg), pipeline_mode=once),
                  pl.BlockSpec((t, heads * 128), lambda hg, qi: (0, hg), pipeline_mode=once)],
        out_specs=pl.BlockSpec((TILE, heads * DHA), lambda hg, qi: (qi, hg)),
        out_shape=jax.ShapeDtypeStruct((t, HA * DHA), F32),
        scratch_shapes=[pltpu.VMEM((heads, 2, TILE, TILE), F32),
                        pltpu.VMEM((heads, TILE, 128), F32),
                        pltpu.VMEM((heads, TILE, 128), F32),
                        pltpu.VMEM((heads, TILE, 128), BF16),
                        pltpu.VMEM((2, heads, TILE, unroll * TILE), F32)],
        compiler_params=_params("arbitrary", "arbitrary"),
        name="moba_attention",
    )(tab, q, kmean2, kaug, vaug)


def _rglru_kernel(xr_ref, gr_ref, cw_ref, cb_ref, wrg_ref, brg_ref, wig_ref, big_ref, lam_ref,
                  o_ref, xbuf, a_s, b_s, hcar, *, tt):
    @pl.when(pl.program_id(0) == 0)
    def _():
        xbuf[0:8, :] = jnp.zeros((8, LRU_W), F32)
        hcar[...] = jnp.zeros_like(hcar)

    xbuf[8:8 + tt, :] = xr_ref[...]
    xc = cb_ref[...]
    for j in range(CONV_W):
        xc = xc + xbuf[8 - (CONV_W - 1) + j:8 - (CONV_W - 1) + j + tt, :] * cw_ref[j:j + 1, :]
    xbuf[0:8, :] = xbuf[tt:tt + 8, :]

    xcb = xc.astype(BF16)
    r = jax.nn.sigmoid(jnp.dot(xcb, wrg_ref[...], preferred_element_type=F32) + brg_ref[...])
    ig = jax.nn.sigmoid(jnp.dot(xcb, wig_ref[...], preferred_element_type=F32) + big_ref[...])
    z = -lam_ref[...]
    softplus = jnp.maximum(z, 0.0) + jnp.log(1.0 + jnp.exp(-jnp.abs(z)))
    a = jnp.exp(-LRU_C * r * softplus)
    a_s[...] = a
    b_s[...] = jnp.sqrt(1.0 - a * a) * ig * xc

    row = lax.broadcasted_iota(jnp.int32, (8, LRU_W), 0)

    def body(gi, h):
        off = pl.multiple_of(gi * 8, 8)
        a8 = a_s[pl.ds(off, 8), :]
        b8 = b_s[pl.ds(off, 8), :]
        for s in (1, 2, 4):
            a_sh = jnp.where(row >= s, pltpu.roll(a8, s, 0), 1.0)
            b_sh = jnp.where(row >= s, pltpu.roll(b8, s, 0), 0.0)
            b8 = a8 * b_sh + b8
            a8 = a8 * a_sh
        h8 = b8 + a8 * h
        b_s[pl.ds(off, 8), :] = h8
        return h8[7:8, :]

    hcar[0:1, :] = lax.fori_loop(0, tt // 8, body, hcar[0:1, :])
    o_ref[...] = b_s[...] * _gelu_tanh(gr_ref[...])


def _rglru(xr, gr, conv_w, conv_b, wrg, b_rg, wig, b_ig, lam, tt=512):
    t = xr.shape[0]
    row = lambda a: a.reshape(1, LRU_W)
    full = lambda shape: pl.BlockSpec(shape, lambda i: (0, 0))
    return pl.pallas_call(
        functools.partial(_rglru_kernel, tt=tt),
        grid=(t // tt,),
        in_specs=[pl.BlockSpec((tt, LRU_W), lambda i: (i, 0)),
                  pl.BlockSpec((tt, LRU_W), lambda i: (i, 0)),
                  full((CONV_W, LRU_W)), full((1, LRU_W)),
                  full((LRU_W, LRU_W)), full((1, LRU_W)),
                  full((LRU_W, LRU_W)), full((1, LRU_W)), full((1, LRU_W))],
        out_specs=pl.BlockSpec((tt, LRU_W), lambda i: (i, 0)),
        out_shape=jax.ShapeDtypeStruct((t, LRU_W), F32),
        scratch_shapes=[pltpu.VMEM((tt + 8, LRU_W), F32),
                        pltpu.VMEM((tt, LRU_W), F32),
                        pltpu.VMEM((tt, LRU_W), F32),
                        pltpu.VMEM((8, LRU_W), F32)],
        compiler_params=_params("arbitrary"),
        name="rglru",
    )(xr, gr, conv_w, row(conv_b), wrg, row(b_rg), wig, row(b_ig), row(lam))


def _post(y, g_ref, x_ref, o_ref):
    ms = jnp.mean(y * y, axis=-1, keepdims=True)
    o_ref[...] = x_ref[...] + y * lax.rsqrt(ms + EPS) * g_ref[...]


def _outproj_ab_kernel(a1_ref, a2_ref, w1_ref, w2_ref, g_ref, x_ref, o_ref):
    y = (jnp.dot(a1_ref[...].astype(BF16), w1_ref[...], preferred_element_type=F32)
         + jnp.dot(a2_ref[...].astype(BF16), w2_ref[...], preferred_element_type=F32))
    _post(y, g_ref, x_ref, o_ref)


def _outproj_ab(a1, a2, w1, w2, g, x, tm=512):
    t = x.shape[0]
    rows = lambda w: pl.BlockSpec((tm, w), lambda i: (i, 0))
    full = lambda shape: pl.BlockSpec(shape, lambda i: (0, 0))
    return pl.pallas_call(
        _outproj_ab_kernel,
        grid=(t // tm,),
        in_specs=[rows(a1.shape[1]), rows(a2.shape[1]), full(w1.shape), full(w2.shape),
                  full((1, D)), rows(D)],
        out_specs=rows(D),
        out_shape=jax.ShapeDtypeStruct((t, D), F32),
        compiler_params=_params("arbitrary"),
        name="outproj_ab",
    )(a1, a2, w1, w2, g.reshape(1, D), x)


def _outproj_c_kernel(oc_ref, os_ref, ow_ref, gl_ref, w_ref, g_ref, x_ref, o_ref):
    gates = jax.nn.sigmoid(gl_ref[...])
    parts = []
    for h in range(HC):
        sl = slice(h * DHC, (h + 1) * DHC)
        parts.append(gates[:, 3 * h:3 * h + 1] * oc_ref[:, sl]
                     + gates[:, 3 * h + 1:3 * h + 2] * os_ref[:, sl]
                     + gates[:, 3 * h + 2:3 * h + 3] * ow_ref[:, sl])
    o = jnp.concatenate(parts, axis=1).astype(BF16)
    _post(jnp.dot(o, w_ref[...], preferred_element_type=F32), g_ref, x_ref, o_ref)


def _outproj_c(oc, os_, ow, gl, w, g, x, tm=512):
    t = x.shape[0]
    rows = lambda w_: pl.BlockSpec((tm, w_), lambda i: (i, 0))
    full = lambda shape: pl.BlockSpec(shape, lambda i: (0, 0))
    return pl.pallas_call(
        _outproj_c_kernel,
        grid=(t // tm,),
        in_specs=[rows(D), rows(D), rows(D), rows(128), full(w.shape), full((1, D)), rows(D)],
        out_specs=rows(D),
        out_shape=jax.ShapeDtypeStruct((t, D), F32),
        compiler_params=_params("arbitrary"),
        name="outproj_c",
    )(oc, os_, ow, gl, w, g.reshape(1, D), x)


def _mlp_kernel(x_ref, gpre_ref, wup_ref, wdn_ref, gpost_ref, o_ref, h_s, acc_s):
    c = pl.program_id(1)

    @pl.when(c == 0)
    def _():
        x = x_ref[...]
        ms = jnp.mean(x * x, axis=-1, keepdims=True)
        h_s[...] = (x * lax.rsqrt(ms + EPS) * gpre_ref[...]).astype(BF16)
        acc_s[...] = jnp.zeros_like(acc_s)

    u = jnp.maximum(jnp.dot(h_s[...], wup_ref[...], preferred_element_type=F32), 0.0)
    acc_s[...] += jnp.dot((u * u).astype(BF16), wdn_ref[...], preferred_element_type=F32)

    @pl.when(c == pl.num_programs(1) - 1)
    def _():
        _post(acc_s[...], gpost_ref, x_ref, o_ref)


def _mlp(x, gpre, wup, wdn, gpost, tm=1024, tf=512):
    t = x.shape[0]
    return pl.pallas_call(
        _mlp_kernel,
        grid=(t // tm, D_FF // tf),
        in_specs=[pl.BlockSpec((tm, D), lambda i, c: (i, 0)),
                  pl.BlockSpec((1, D), lambda i, c: (0, 0)),
                  pl.BlockSpec((D, tf), lambda i, c: (0, c)),
                  pl.BlockSpec((tf, D), lambda i, c: (c, 0)),
                  pl.BlockSpec((1, D), lambda i, c: (0, 0))],
        out_specs=pl.BlockSpec((tm, D), lambda i, c: (i, 0)),
        out_shape=jax.ShapeDtypeStruct((t, D), F32),
        scratch_shapes=[pltpu.VMEM((tm, D), BF16), pltpu.VMEM((tm, D), F32)],
        compiler_params=_params("arbitrary", "arbitrary"),
        name="mlp",
    )(x, gpre.reshape(1, D), wup, wdn, gpost.reshape(1, D))


def _compress_kernel(h_ref, pos_ref, w1_ref, w2_ref, o_ref):
    half = CMP_STRIDE * DHC
    hm = h_ref[0]
    top = jnp.dot((hm + pos_ref[:, 0:half]).astype(BF16), w1_ref[0:half, :], preferred_element_type=F32)
    bot = jnp.dot((hm + pos_ref[:, half:2 * half]).astype(BF16), w1_ref[half:2 * half, :],
                  preferred_element_type=F32)
    hid = top + pltpu.roll(bot, bot.shape[0] - 1, 0)
    o_ref[0] = jnp.dot(_gelu_tanh(hid).astype(BF16), w2_ref[...], preferred_element_type=F32).astype(o_ref.dtype)


def _compress(hmat, pos, w1, w2):
    g, n, wd = hmat.shape
    return pl.pallas_call(
        _compress_kernel,
        grid=(g,),
        in_specs=[pl.BlockSpec((1, n, wd), lambda i: (i, 0, 0)),
                  pl.BlockSpec((1, 2 * wd), lambda i: (0, 0)),
                  pl.BlockSpec(w1.shape, lambda i: (0, 0)),
                  pl.BlockSpec(w2.shape, lambda i: (0, 0))],
        out_specs=pl.BlockSpec((1, n, DHC), lambda i: (i, 0, 0)),
        out_shape=jax.ShapeDtypeStruct((g, n, DHC), BF16),
        compiler_params=_params("arbitrary"),
        name="nsa_compress",
    )(hmat, pos.reshape(1, 2 * wd), w1, w2)


def _stack_heads(q):
    return jnp.concatenate([q[:, p * DHC:(p + 1) * DHC] for p in range(PC)], axis=0)


def _nsa_cmp_kernel(tab_ref, q_ref, kc_ref, vc_ref, ov_ref, oc_ref, sb_ref, e_ref, qb_ref, *, tq, ns):
    qi = pl.program_id(0)
    ncp = kc_ref.shape[1]
    nsp = ov_ref.shape[1]
    scale = DHC ** -0.5 * LOG2E
    cpq = tq // CMP_STRIDE
    k = min(SEL_TOPK, ns)

    @pl.when(qi == 0)
    def _():
        u = lax.broadcasted_iota(jnp.int32, (2 * ncp, 128), 0)
        lane = lax.broadcasted_iota(jnp.int32, (2 * ncp, 128), 1)
        hit = (lane < 2 * CMP_BIAS_LANES) & (u + lane % CMP_BIAS_LANES == ncp - 1)
        after = (lane == 2 * CMP_BIAS_LANES) & (u >= ncp)
        e_ref[...] = (jnp.where(hit, 1.0, 0.0) + jnp.where(after, 1.0, 0.0)).astype(BF16)
        i = lax.broadcasted_iota(jnp.int32, (tq, 128), 0)
        lane = lax.broadcasted_iota(jnp.int32, (tq, 128), 1)
        d = i + CMP_STRIDE * (lane % CMP_BIAS_LANES) - (tq - CMP_STRIDE + CMP_LEN - 1)
        for h in range(HC):
            b = jnp.where(d >= 0, LOG2E * _bias_delta(jnp.maximum(d, 0), tab_ref, h), NEG)
            hi = b.astype(BF16).astype(F32)
            qb = jnp.where(lane < CMP_BIAS_LANES, hi, jnp.where(lane < 2 * CMP_BIAS_LANES, b - hi, 0.0))
            qb_ref[h] = jnp.where(lane == 2 * CMP_BIAS_LANES, NEG, qb).astype(BF16)

    off = pl.multiple_of(ncp - cpq * (qi + 1), 16)
    onehots = e_ref[pl.ds(off, ncp), :]
    jj = lax.broadcasted_iota(jnp.int32, (tq, nsp), 1)
    cur = (qi * tq + lax.broadcasted_iota(jnp.int32, (tq, nsp), 0)) // SEL_BLOCK
    forced = (jj == 0) | (jj == cur) | (jj == cur - 1)
    dead = (jj > cur) | (jj >= ns)
    q = q_ref[...]
    fallbacks = []
    for g in range(GC):
        heads = range(g * PC, (g + 1) * PC)
        qa = jnp.concatenate(
            [jnp.concatenate([(q[:, h * DHC:(h + 1) * DHC] * scale).astype(BF16) for h in heads], axis=0),
             jnp.concatenate([qb_ref[h] for h in heads], axis=0)], axis=1)
        s_all = _dot_t(qa, jnp.concatenate([kc_ref[g], onehots], axis=1))
        psum = jnp.zeros((tq, ncp), F32)
        for p, h in enumerate(heads):
            s = s_all[p * tq:(p + 1) * tq]
            m = jnp.maximum(jnp.max(s, axis=1, keepdims=True), CMP_MAX_FLOOR)
            e = jnp.exp2(s - m)
            pc = e * (1.0 / jnp.maximum(jnp.sum(e, axis=1, keepdims=True), 1e-30))
            oc_ref[:, h * DHC:(h + 1) * DHC] = jnp.dot(pc.astype(BF16), vc_ref[g], preferred_element_type=F32)
            psum = psum + pc

        hi = psum.astype(BF16)
        lo = (psum - hi.astype(F32)).astype(BF16)
        imp = (jnp.dot(hi, ov_ref[...], preferred_element_type=F32)
               + jnp.dot(lo, ov_ref[...], preferred_element_type=F32))
        rest = jnp.where(forced | dead, NEG, imp)
        sel_rest = _removed_topk(rest, k - 3)
        sel = sel_rest | (forced & jnp.logical_not(dead))
        sb_ref[g] = jnp.where(sel, 0.0, NEG).astype(BF16)
        count = jnp.sum(jnp.where(sel_rest, 1.0, 0.0), axis=1, keepdims=True)
        cand = jnp.sum(jnp.where(forced | dead, 0.0, 1.0), axis=1, keepdims=True)
        tied = jnp.max(jnp.where(count != jnp.minimum(cand, float(k - 3)), 1.0, 0.0))
        fallbacks.append((g, imp, tied))

    for g, imp, tied in fallbacks:
        @pl.when(tied > 0.0)
        def _(g=g, imp=imp):
            full = jnp.where(dead, NEG, jnp.where(forced, imp + SEL_FORCE, imp))
            sb_ref[g] = jnp.where(_topk_mask(full, jj, k), 0.0, NEG).astype(BF16)


def _nsa_cmp(tab, q, kc, vc, overlap, ns, tq=CMP_TQ):
    t = q.shape[0]
    ncp = kc.shape[1]
    nsp = overlap.shape[1]
    assert (tq - CMP_STRIDE + CMP_LEN - 1 + MAX_DISTANCE) // CMP_STRIDE < CMP_BIAS_LANES
    full = lambda shape: pl.BlockSpec(shape, lambda qi: (0,) * len(shape))
    return pl.pallas_call(
        functools.partial(_nsa_cmp_kernel, tq=tq, ns=ns),
        grid=(t // tq,),
        in_specs=[_smem(),
                  pl.BlockSpec((tq, HC * DHC), lambda qi: (qi, 0)),
                  full((GC, ncp, DHC)), full((GC, ncp, DHC)), full((ncp, nsp))],
        out_specs=[pl.BlockSpec((tq, HC * DHC), lambda qi: (qi, 0)),
                   pl.BlockSpec((GC, tq, nsp), lambda qi: (0, qi, 0))],
        out_shape=[jax.ShapeDtypeStruct((t, HC * DHC), F32),
                   jax.ShapeDtypeStruct((GC, t, nsp), BF16)],
        scratch_shapes=[pltpu.VMEM((2 * ncp, 128), BF16), pltpu.VMEM((HC, tq, 128), BF16)],
        compiler_params=_params("arbitrary"),
        name="nsa_compressed_topk",
    )(tab, q, kc, vc, overlap)


def _nsa_sel_kernel(tab_ref, q_ref, sb_ref, ka_ref, va_ref, o_ref, f_ref, m_ref, acc_ref, s_ref, *, nhalf, nb,
                    unroll):
    qi = pl.program_id(0)
    scale = DHC ** -0.5 * LOG2E
    tiles_per_half = 128 * SEL_BLOCK // TILE

    @pl.when(qi == 0)
    def _():
        _near_bias_tiles(f_ref, tab_ref, list(range(HC)))

    hpc = PC // SEL_CHUNKS
    chunks = [(g, list(range(g * PC + c * hpc, g * PC + (c + 1) * hpc)))
              for g in range(GC) for c in range(SEL_CHUNKS)]
    q = q_ref[...]
    qas = []
    for g, heads in chunks:
        qc = jnp.concatenate([q[:, h * DHC:(h + 1) * DHC] for h in heads], axis=0)
        qc = (qc * scale).astype(BF16)
        sb = sb_ref[g]
        qas.append([jnp.concatenate([qc, jnp.concatenate([sb[:, hf * 128:(hf + 1) * 128]] * hpc, axis=0)], axis=1)
                    for hf in range(nhalf)])

    def qa_for(c, j):
        if nhalf == 1:
            return qas[c][0]
        return jnp.where(j >= tiles_per_half, qas[c][1], qas[c][0])

    def keys(g, off, n):
        return ka_ref[pl.ds(off, n), g * 2 * DHC:(g + 1) * 2 * DHC]

    def values(g, off, n):
        return va_ref[pl.ds(off, n), g * 2 * DHC:(g + 1) * 2 * DHC]

    far_tiles = unroll * (jnp.maximum(qi - 1, 0) // unroll)
    groups = far_tiles // unroll
    last = jnp.maximum(groups - 1, 0)

    def produce(buf, grp):
        off = pl.multiple_of(grp * (unroll * TILE), unroll * TILE)
        for c, (g, _) in enumerate(chunks):
            s_ref[buf, c] = _dot_t(qa_for(c, grp * unroll), keys(g, off, unroll * TILE))

    def consume(buf, grp):
        off = pl.multiple_of(grp * (unroll * TILE), unroll * TILE)
        for c, (g, _) in enumerate(chunks):
            _flash_update(s_ref[buf, c], values(g, off, unroll * TILE), m_ref.at[c], acc_ref.at[c], False)

    produce(0, 0)
    start = jnp.minimum(far_tiles, nb - (unroll + 1))
    for c, (g, heads) in enumerate(chunks):
        cols = []
        for u in range(unroll + 1):
            j = start + u
            s = _dot_t(qa_for(c, j), keys(g, pl.multiple_of(j * TILE, TILE), TILE))
            cols.append(jnp.concatenate(
                [s[i * TILE:(i + 1) * TILE] + _slot_bias(j, qi, far_tiles, f_ref[h, 0], f_ref[h, 1])
                 for i, h in enumerate(heads)], axis=0))
        vt = values(g, pl.multiple_of(start * TILE, TILE), (unroll + 1) * TILE)
        _flash_update(jnp.concatenate(cols, axis=1), vt, m_ref.at[c], acc_ref.at[c], True)

    def pair(k, carry):
        produce(1, 2 * k + 1)
        consume(0, 2 * k)
        produce(0, jnp.minimum(2 * k + 2, last))
        consume(1, 2 * k + 1)
        return carry

    lax.fori_loop(0, groups // 2, pair, 0)

    @pl.when(groups % 2 == 1)
    def _():
        consume(0, groups - 1)


    for c, (g, heads) in enumerate(chunks):
        acc = acc_ref[c]
        o = acc[:, :DHC] / acc[:, DHC:DHC + 1]
        for i, h in enumerate(heads):
            o_ref[:, h * DHC:(h + 1) * DHC] = o[i * TILE:(i + 1) * TILE]


def _nsa_sel(tab, q, selbias, kaug, vaug):
    t = q.shape[0]
    nsp = selbias.shape[2]
    nchunks = GC * SEL_CHUNKS
    rows = HC // nchunks * TILE
    once = pl.Buffered(1)
    unroll = min(SEL_UNROLL, t // TILE - 1)
    return pl.pallas_call(
        functools.partial(_nsa_sel_kernel, nhalf=nsp // 128, nb=t // TILE, unroll=unroll),
        grid=(t // TILE,),
        in_specs=[_smem(),
                  pl.BlockSpec((TILE, HC * DHC), lambda qi: (qi, 0)),
                  pl.BlockSpec((GC, TILE, nsp), lambda qi: (0, qi, 0)),
                  pl.BlockSpec((t, GC * 2 * DHC), lambda qi: (0, 0), pipeline_mode=once),
                  pl.BlockSpec((t, GC * 2 * DHC), lambda qi: (0, 0), pipeline_mode=once)],
        out_specs=pl.BlockSpec((TILE, HC * DHC), lambda qi: (qi, 0)),
        out_shape=jax.ShapeDtypeStruct((t, HC * DHC), F32),
        scratch_shapes=[pltpu.VMEM((HC, 2, TILE, TILE), F32),
                        pltpu.VMEM((nchunks, rows, 128), F32),
                        pltpu.VMEM((nchunks, rows, 2 * DHC), F32),
                        pltpu.VMEM((2, nchunks, rows, unroll * TILE), F32)],
        compiler_params=_params("arbitrary", vmem_limit=VMEM_LIMIT_RESIDENT),
        name="nsa_selected",
    )(tab, q, selbias, kaug, vaug)


def _nsa_win_kernel(tab_ref, q_ref, k_ref, v_ref, o_ref, f_ref, *, tq):
    qi = pl.program_id(0)
    scale = DHC ** -0.5 * LOG2E
    nw = WINDOW + tq
    jj = lax.broadcasted_iota(jnp.int32, (tq, nw), 1)

    @pl.when(qi == 0)
    def _():
        d = WINDOW + lax.broadcasted_iota(jnp.int32, (tq, nw), 0) - jj
        ok = (d >= 0) & (d < WINDOW)
        for h in range(HC):
            f_ref[h] = jnp.where(ok, LOG2E * _bias_delta(jnp.maximum(d, 0), tab_ref, h), NEG)

    start = pl.multiple_of(qi * tq, tq)

    def body(front):
        pos_ok = jnp.where(jj >= WINDOW - start, 0.0, NEG) if front else None
        for h in range(HC):
            g = h // PC
            q = (q_ref[:, h * DHC:(h + 1) * DHC] * scale).astype(BF16)
            s = _dot_t(q, k_ref[pl.ds(start, nw), g * DHC:(g + 1) * DHC]) + f_ref[h]
            if front:
                s = s + pos_ok
            e = jnp.exp2(s - jnp.max(s, axis=1, keepdims=True))
            o = jnp.dot(e.astype(BF16), v_ref[pl.ds(start, nw), g * DHC:(g + 1) * DHC],
                        preferred_element_type=F32)
            o_ref[:, h * DHC:(h + 1) * DHC] = o * (1.0 / jnp.sum(e, axis=1, keepdims=True))

    @pl.when(start < WINDOW)
    def _():
        body(True)

    @pl.when(start >= WINDOW)
    def _():
        body(False)


def _nsa_win(tab, q, kwp, vwp, tq=TILE):
    t = q.shape[0]
    once = pl.Buffered(1)
    return pl.pallas_call(
        functools.partial(_nsa_win_kernel, tq=tq),
        grid=(t // tq,),
        in_specs=[_smem(),
                  pl.BlockSpec((tq, HC * DHC), lambda qi: (qi, 0)),
                  pl.BlockSpec(kwp.shape, lambda qi: (0, 0), pipeline_mode=once),
                  pl.BlockSpec(vwp.shape, lambda qi: (0, 0), pipeline_mode=once)],
        out_specs=pl.BlockSpec((tq, HC * DHC), lambda qi: (qi, 0)),
        out_shape=jax.ShapeDtypeStruct((t, HC * DHC), F32),
        scratch_shapes=[pltpu.VMEM((HC, tq, WINDOW + tq), F32)],
        compiler_params=_params("arbitrary"),
        name="nsa_window",
    )(tab, q, kwp, vwp)


def _layer_ab(x, tab, w_in, conv_w, conv_b, w_rg, b_rg, w_ig, b_ig, lam, w_out, g_pre, g_post):
    t = x.shape[0]
    da = HA * DHA
    wq, wk, wv = (w_in[:, i * da:(i + 1) * da] for i in range(3))
    wx = w_in[:, 3 * da:]
    wk3 = wk.reshape(D, HA, DHA)
    wv3 = wv.reshape(D, HA, DHA)
    zeros = jnp.zeros_like(wk3)
    even = (jnp.arange(HA) % 2 == 0)[None, :, None]
    wk_aug = jnp.concatenate([jnp.where(even, wk3, zeros), jnp.where(even, zeros, wk3)], axis=-1)
    wv_aug = jnp.concatenate([wv3, zeros], axis=-1)
    w = jnp.concatenate([wq, wk, wx, wk_aug.reshape(D, HA * 128), wv_aug.reshape(D, HA * 128)],
                        axis=1).astype(BF16)
    segs = _segments([(da, None), (da, None), (LRU_W, None), (LRU_W, None),
                      (HA * 128, ("onehot", 128, MOBA_BLOCK, True)), (HA * 128, ("ones", 128, DHA))])
    q, k, xr, gr, kaug, vaug = _rms_proj(x, g_pre, w, segs, [F32, F32, F32, F32, BF16, BF16])

    nb = t // MOBA_BLOCK
    kmean = jnp.pad(_block_mean(k), ((0, DHA - nb), (0, 0)))
    o_a = _moba(tab, q, jnp.concatenate([kmean, kmean], axis=0), kaug, vaug)

    eye = jnp.eye(w_rg.shape[0], dtype=F32)
    blockdiag = lambda wb: jnp.einsum('bij,bc->bicj', wb, eye).reshape(LRU_W, LRU_W).astype(BF16)
    o_b = _rglru(xr, gr, conv_w, conv_b, blockdiag(w_rg), b_rg, blockdiag(w_ig), b_ig, lam)
    wo = w_out.astype(BF16)
    return _outproj_ab(o_a, o_b, wo[:da], wo[da:], g_post, x)


def _layer_c(x, tab, w_in, pos_k, pos_v, k_w1, k_w2, v_w1, v_w2, w_out, g_pre, g_post):
    t = x.shape[0]
    dq = HC * DHC
    ckv = GC * DHC
    cols = lambda j: w_in[:, dq + j * ckv: dq + (j + 1) * ckv]
    widen = lambda wc: jnp.concatenate([wc.reshape(D, GC, DHC), jnp.zeros((D, GC, DHC), wc.dtype)],
                                       axis=-1).reshape(D, GC * 2 * DHC)
    wg = jnp.pad(w_in[:, dq + 6 * ckv:], ((0, 0), (0, 128 - 3 * HC)))
    w = jnp.concatenate([w_in[:, :dq], cols(0), cols(1), widen(cols(2)), widen(cols(3)), cols(4), cols(5), wg],
                        axis=1).astype(BF16)
    segs = _segments([(dq, None), (DHC, None), (DHC, None), (DHC, None), (DHC, None),
                      (GC * 2 * DHC, ("onehot", 2 * DHC, SEL_BLOCK, False)),
                      (GC * 2 * DHC, ("ones", 2 * DHC, DHC)),
                      (ckv, None), (ckv, None), (128, None)])
    q, kc0, kc1, vc0, vc1, ks_aug, vs_aug, kw, vw, gl = _rms_proj(
        x, g_pre, w, segs, [F32, F32, F32, F32, F32, BF16, BF16, BF16, BF16, F32])

    n = t // CMP_STRIDE
    hm = lambda a0, a1: jnp.stack([a0.reshape(n, CMP_STRIDE * DHC), a1.reshape(n, CMP_STRIDE * DHC)])
    kc = _compress(hm(kc0, kc1), pos_k, k_w1.astype(BF16), k_w2.astype(BF16))
    vc = _compress(hm(vc0, vc1), pos_v, v_w1.astype(BF16), v_w2.astype(BF16))

    ns = t // SEL_BLOCK
    nsp = -(-ns // 128) * 128
    cs = jnp.arange(n)[:, None] * CMP_STRIDE
    ss = jnp.arange(nsp)[None, :] * SEL_BLOCK
    overlap = ((cs < ss + SEL_BLOCK) & (cs + CMP_LEN > ss) & (jnp.arange(n)[:, None] < n - 1)
               & (jnp.arange(nsp)[None, :] < ns)).astype(BF16)
    o_c, selbias = _nsa_cmp(tab, q, kc, vc, overlap, ns)
    o_s = _nsa_sel(tab, q, selbias, ks_aug, vs_aug)
    padw = lambda a: jnp.pad(a, ((WINDOW, 0), (0, 0)))
    o_w = _nsa_win(tab, q, padw(kw), padw(vw))
    return _outproj_c(o_c, o_s, o_w, gl, w_out.astype(BF16), g_post, x)


def kernel(x, rel_bias, ab_w_in, ab_conv_w, ab_conv_b, ab_w_rg, ab_b_rg, ab_w_ig, ab_b_ig, ab_lru_lambda, ab_w_out, c_w_in, c_cmp_pos_k, c_cmp_pos_v, c_cmp_k_w1, c_cmp_k_w2, c_cmp_v_w1, c_cmp_v_w2, c_w_out, norm_mix_pre, norm_mix_post, norm_mlp_pre, norm_mlp_post, w_up, w_down):
    batch = x.shape[0]
    depth = norm_mix_pre.shape[0]
    outs = []
    for b in range(batch):
        xb = x[b]
        for layer in range(depth):
            i = layer // 2
            if layer % 2 == 0:
                xb = _layer_ab(xb, rel_bias, ab_w_in[i], ab_conv_w[i], ab_conv_b[i], ab_w_rg[i], ab_b_rg[i],
                               ab_w_ig[i], ab_b_ig[i], ab_lru_lambda[i], ab_w_out[i],
                               norm_mix_pre[layer], norm_mix_post[layer])
            else:
                xb = _layer_c(xb, rel_bias, c_w_in[i], c_cmp_pos_k[i], c_cmp_pos_v[i], c_cmp_k_w1[i],
                              c_cmp_k_w2[i], c_cmp_v_w1[i], c_cmp_v_w2[i], c_w_out[i],
                              norm_mix_pre[layer], norm_mix_post[layer])
            xb = _mlp(xb, norm_mlp_pre[layer], w_up[layer].astype(BF16), w_down[layer].astype(BF16),
                      norm_mlp_post[layer])
        outs.append(xb)
    return jnp.stack(outs)
```

```python
import functools
import math

import numpy as np
import jax
import jax.numpy as jnp
from jax import lax
from jax.experimental import pallas as pl
from jax.experimental.pallas import tpu as pltpu

F32 = jnp.float32
BF16 = jnp.bfloat16
HIGHEST = lax.Precision.HIGHEST
LOG2E = 1.4426950408889634

D = 1024
EPS = 1e-6
NEG = -1e30
N_BUCKETS = 32
MAX_DISTANCE = 128

HA = 8
DHA = 64
MOBA_BLOCK = 256
MOBA_TOPK = 3
LRU_W = 512
LRU_C = 8.0
CONV_W = 4
HC = 8
DHC = 128
GC = 2
PC = 4
CMP_LEN = 32
CMP_STRIDE = 16
SEL_BLOCK = 64
SEL_TOPK = 16
WINDOW = 512
SEL_FORCE = 1e4
D_FF = 4096

TILE = 256
MOBA_UNROLL = 4
MOBA_HEADS = 4
SEL_UNROLL = 2
SEL_CHUNKS = 2
CMP_BIAS_LANES = 32
CMP_MAX_FLOOR = -1e20
VMEM_LIMIT = 56 * 1024 * 1024
VMEM_LIMIT_RESIDENT = 60 * 1024 * 1024


def _bucket_np(d):
    n = np.maximum(d, 0)
    exact = N_BUCKETS // 2
    nf = np.maximum(n, 1).astype(np.float32)
    large = exact + (np.log(nf / np.float32(exact)) / np.float32(math.log(MAX_DISTANCE / exact))
                     * np.float32(N_BUCKETS - exact)).astype(np.int32)
    return np.where(n < exact, n, np.minimum(large, N_BUCKETS - 1))


_THR = [int(np.argmax(_bucket_np(np.arange(0, 4 * MAX_DISTANCE)) >= k)) for k in range(1, N_BUCKETS)]
assert _THR[-1] <= MAX_DISTANCE


def _bias_delta(d, tab_ref, h):
    acc = jnp.full(d.shape, tab_ref[0, h] - tab_ref[N_BUCKETS - 1, h], F32)
    for k in range(1, N_BUCKETS):
        acc = acc + jnp.where(d >= _THR[k - 1], tab_ref[k, h] - tab_ref[k - 1, h], 0.0)
    return acc


def _dot_t(a, b, **kw):
    return lax.dot_general(a, b, (((1,), (1,)), ((), ())), preferred_element_type=F32, **kw)


def _gelu_tanh(x):
    return 0.5 * x * (1.0 + jnp.tanh(0.7978845608028654 * (x + 0.044715 * x * x * x)))


def _params(*sem, vmem_limit=VMEM_LIMIT):
    return pltpu.CompilerParams(dimension_semantics=sem, vmem_limit_bytes=vmem_limit)


def _smem():
    return pl.BlockSpec(memory_space=pltpu.SMEM)


def _proj_kernel(x_ref, g_ref, w_ref, *out_refs, segs, tm):
    x = x_ref[...]
    ms = jnp.mean(x * x, axis=-1, keepdims=True)
    hn = (x * lax.rsqrt(ms + EPS) * g_ref[...]).astype(BF16)
    row0 = pl.program_id(0) * tm
    for o_ref, (start, width, aug) in zip(out_refs, segs):
        y = jnp.dot(hn, w_ref[:, start:start + width], preferred_element_type=F32)
        if aug is not None:
            lane = lax.broadcasted_iota(jnp.int32, y.shape, 1)
            if aug[0] == "onehot":
                _, group, rows_per_block, lo_first = aug
                half = group // 2
                row = row0 + lax.broadcasted_iota(jnp.int32, y.shape, 0)
                which_half = (lane // half) % 2
                data_half = (lane // group) % 2 if lo_first else jnp.zeros_like(lane)
                blk = (row // rows_per_block) % half
                y = jnp.where((which_half != data_half) & (lane % half == blk), 1.0, y)
            else:
                _, group, col = aug
                y = jnp.where(lane % group == col, 1.0, y)
        o_ref[...] = y.astype(o_ref.dtype)


def _rms_proj(x, g, w, segs, dtypes, tm=512):
    t = x.shape[0]
    return pl.pallas_call(
        functools.partial(_proj_kernel, segs=segs, tm=tm),
        grid=(t // tm,),
        in_specs=[pl.BlockSpec((tm, D), lambda i: (i, 0)),
                  pl.BlockSpec((1, D), lambda i: (0, 0)),
                  pl.BlockSpec(w.shape, lambda i: (0, 0))],
        out_specs=[pl.BlockSpec((tm, wd), lambda i: (i, 0)) for (_, wd, _) in segs],
        out_shape=[jax.ShapeDtypeStruct((t, wd), dt) for (_, wd, _), dt in zip(segs, dtypes)],
        compiler_params=_params("arbitrary"),
        name="rms_proj",
    )(x, g.reshape(1, D), w)


def _segments(widths_augs):
    segs, off = [], 0
    for wd, aug in widths_augs:
        segs.append((off, wd, aug))
        off += wd
    return segs


def _topk_mask(r, lane, k):
    sel = jnp.zeros(r.shape, jnp.bool_)
    lanef = lane.astype(F32)
    for _ in range(k):
        mx = jnp.max(r, axis=1, keepdims=True)
        first = jnp.min(jnp.where(r == mx, lanef, 1e9), axis=1, keepdims=True)
        pick = lanef == first
        sel = sel | (pick & (mx > NEG / 2))
        r = jnp.where(pick, NEG, r)
    return sel


def _removed_topk(r0, k):
    r = r0
    for _ in range(k):
        r = jnp.where(r == jnp.max(r, axis=1, keepdims=True), NEG, r)
    return (r0 > NEG / 2) & (r < NEG / 2)


def _near_bias_tiles(f_ref, tab_ref, heads):
    d0 = (lax.broadcasted_iota(jnp.int32, (TILE, TILE), 0)
          - lax.broadcasted_iota(jnp.int32, (TILE, TILE), 1))
    for i, h in enumerate(heads):
        f_ref[i, 0] = jnp.where(d0 >= 0, LOG2E * _bias_delta(jnp.maximum(d0, 0), tab_ref, h), NEG)
        f_ref[i, 1] = LOG2E * _bias_delta(d0 + TILE, tab_ref, h)


def _slot_bias(j, qi, lo, f_diag, f_prev):
    other = jnp.where((j >= lo) & (j <= qi), 0.0, NEG)
    return jnp.where(j == qi, f_diag, jnp.where(j == qi - 1, f_prev, other))


def _flash_update(s, vt, m_ref, acc_ref, first):
    rm = jnp.max(s, axis=1, keepdims=True)
    if first:
        m_ref[...] = jnp.broadcast_to(rm, m_ref.shape)
        p = jnp.exp2(s - rm)
        acc_ref[...] = jnp.dot(p.astype(BF16), vt, preferred_element_type=F32)
        return
    m_old = m_ref[...]
    m_new = jnp.maximum(m_old, rm)
    alpha = jnp.exp2(m_old - m_new)
    p = jnp.exp2(s - jnp.concatenate([m_new] * (s.shape[1] // m_new.shape[1]), axis=1))
    pv = jnp.dot(p.astype(BF16), vt, preferred_element_type=F32)
    acc_ref[...] = jnp.concatenate([alpha] * (pv.shape[1] // alpha.shape[1]), axis=1) * acc_ref[...] + pv
    m_ref[...] = m_new


def _blockmean_kernel(k_ref, o_ref):
    k = k_ref[...]
    o_ref[...] = jnp.mean(k.reshape(8, MOBA_BLOCK, k.shape[-1]), axis=1)


def _block_mean(k):
    t, w = k.shape
    nb = t // MOBA_BLOCK
    return pl.pallas_call(
        _blockmean_kernel,
        grid=(nb // 8,),
        in_specs=[pl.BlockSpec((8 * MOBA_BLOCK, w), lambda i: (i, 0))],
        out_specs=pl.BlockSpec((8, w), lambda i: (i, 0)),
        out_shape=jax.ShapeDtypeStruct((nb, w), F32),
        compiler_params=_params("arbitrary"),
        name="moba_block_mean",
    )(k)


def _moba_kernel(tab_ref, q_ref, km_ref, ka_ref, va_ref, o_ref, f_ref, m_ref, acc_ref, qa_ref, s_ref, *, nb,
                 unroll, heads):
    h0 = pl.program_id(0) * heads
    qi = pl.program_id(1)
    scale = DHA ** -0.5 * LOG2E
    lane = lax.broadcasted_iota(jnp.int32, (TILE, 128), 1)

    @pl.when(qi == 0)
    def _():
        _near_bias_tiles(f_ref, tab_ref, [h0 + hh for hh in range(heads)])

    def augmented_query(q, head_lanes, in_aug, blk, sel):
        keep = sel | (in_aug & (blk == qi))
        return jnp.where(head_lanes, q * scale, jnp.where(keep, 0.0, NEG)).astype(BF16)

    fallbacks = []
    for hh in range(heads):
        pair = slice(hh // 2 * 128, (hh // 2 + 1) * 128)
        q = q_ref[:, pair]
        head_lanes = (lane < DHA) if hh % 2 == 0 else (lane >= DHA)
        qh = jnp.where(head_lanes, q, 0.0)
        route = _dot_t(qh, km_ref[:, pair], precision=HIGHEST)
        blk = lane - DHA if hh % 2 == 0 else lane
        in_aug = jnp.logical_not(head_lanes)
        cand = in_aug & (blk < qi)
        r = jnp.where(cand, route, NEG)
        sel = _removed_topk(r, MOBA_TOPK)
        qa_ref[hh] = augmented_query(q, head_lanes, in_aug, blk, sel)
        count = jnp.sum(jnp.where(sel, 1.0, 0.0), axis=1, keepdims=True)
        ncand = jnp.sum(jnp.where(cand, 1.0, 0.0), axis=1, keepdims=True)
        tied = jnp.max(jnp.where(count != jnp.minimum(ncand, float(MOBA_TOPK)), 1.0, 0.0))
        fallbacks.append((hh, tied, (q, head_lanes, in_aug, blk), r))

    for hh, tied, args, r in fallbacks:
        @pl.when(tied > 0.0)
        def _(hh=hh, args=args, r=r):
            qa_ref[hh] = augmented_query(*args, _topk_mask(r, lane, MOBA_TOPK))

    def scores(j, hh, n):
        off = pl.multiple_of(j * TILE, TILE)
        return _dot_t(qa_ref[hh], ka_ref[pl.ds(off, n * TILE), hh * 128:(hh + 1) * 128])

    def values(j, hh, n):
        off = pl.multiple_of(j * TILE, TILE)
        return va_ref[pl.ds(off, n * TILE), hh * 128:(hh + 1) * 128]

    far_tiles = unroll * (jnp.maximum(qi - 1, 0) // unroll)
    groups = far_tiles // unroll
    last = jnp.maximum(groups - 1, 0)

    def produce(buf, grp):
        for hh in range(heads):
            s_ref[buf, hh] = scores(grp * unroll, hh, unroll)

    def consume(buf, grp):
        for hh in range(heads):
            _flash_update(s_ref[buf, hh], values(grp * unroll, hh, unroll), m_ref.at[hh], acc_ref.at[hh], False)

    produce(0, 0)
    start = jnp.minimum(far_tiles, nb - (unroll + 1))
    for hh in range(heads):
        bias = jnp.concatenate([_slot_bias(start + u, qi, far_tiles, f_ref[hh, 0], f_ref[hh, 1])
                                for u in range(unroll + 1)], axis=1)
        _flash_update(scores(start, hh, unroll + 1) + bias, values(start, hh, unroll + 1),
                      m_ref.at[hh], acc_ref.at[hh], True)

    def pair(k, c):
        produce(1, 2 * k + 1)
        consume(0, 2 * k)
        produce(0, jnp.minimum(2 * k + 2, last))
        consume(1, 2 * k + 1)
        return c

    lax.fori_loop(0, groups // 2, pair, 0)

    @pl.when(groups % 2 == 1)
    def _():
        consume(0, groups - 1)

    for pp in range(heads // 2):
        outs = []
        for hh in (2 * pp, 2 * pp + 1):
            acc = acc_ref[hh]
            outs.append(acc / acc[:, DHA:DHA + 1])
        o_ref[:, pp * 128:(pp + 1) * 128] = jnp.where(lane < DHA, outs[0], pltpu.roll(outs[1], DHA, 1))


def _moba(tab, q, kmean2, kaug, vaug, heads=MOBA_HEADS):
    t = q.shape[0]
    once = pl.Buffered(1)
    unroll = min(MOBA_UNROLL, t // TILE - 1)
    return pl.pallas_call(
        functools.partial(_moba_kernel, nb=t // TILE, unroll=unroll, heads=heads),
        grid=(HA // heads, t // TILE),
        in_specs=[_smem(),
                  pl.BlockSpec((TILE, heads * DHA), lambda hg, qi: (qi, hg)),
                  pl.BlockSpec((128, heads * DHA), lambda hg, qi: (0, hg)),
                  pl.BlockSpec((t, heads * 128), lambda hg, qi: (0, hg), pipeline_mode=once),
                  pl.BlockSpec((t, heads * 128), lambda hg, qi: (0, hg), pipeline_mode=once)],
        out_specs=pl.BlockSpec((TILE, heads * DHA), lambda hg, qi: (qi, hg)),
        out_shape=jax.ShapeDtypeStruct((t, HA * DHA), F32),
        scratch_shapes=[pltpu.VMEM((heads, 2, TILE, TILE), F32),
                        pltpu.VMEM((heads, TILE, 128), F32),
                        pltpu.VMEM((heads, TILE, 128), F32),
                        pltpu.VMEM((heads, TILE, 128), BF16),
                        pltpu.VMEM((2, heads, TILE, unroll * TILE), F32)],
        compiler_params=_params("arbitrary", "arbitrary"),
        name="moba_attention",
    )(tab, q, kmean2, kaug, vaug)


def _rglru_kernel(xr_ref, gr_ref, cw_ref, cb_ref, wrg_ref, brg_ref, wig_ref, big_ref, lam_ref,
                  o_ref, xbuf, a_s, b_s, hcar, *, tt):
    @pl.when(pl.program_id(0) == 0)
    def _():
        xbuf[0:8, :] = jnp.zeros((8, LRU_W), F32)
        hcar[...] = jnp.zeros_like(hcar)

    xbuf[8:8 + tt, :] = xr_ref[...]
    xc = cb_ref[...]
    for j in range(CONV_W):
        xc = xc + xbuf[8 - (CONV_W - 1) + j:8 - (CONV_W - 1) + j + tt, :] * cw_ref[j:j + 1, :]
    xbuf[0:8, :] = xbuf[tt:tt + 8, :]

    xcb = xc.astype(BF16)
    r = jax.nn.sigmoid(jnp.dot(xcb, wrg_ref[...], preferred_element_type=F32) + brg_ref[...])
    ig = jax.nn.sigmoid(jnp.dot(xcb, wig_ref[...], preferred_element_type=F32) + big_ref[...])
    z = -lam_ref[...]
    softplus = jnp.maximum(z, 0.0) + jnp.log(1.0 + jnp.exp(-jnp.abs(z)))
    a = jnp.exp(-LRU_C * r * softplus)
    a_s[...] = a
    b_s[...] = jnp.sqrt(1.0 - a * a) * ig * xc

    row = lax.broadcasted_iota(jnp.int32, (8, LRU_W), 0)

    def body(gi, h):
        off = pl.multiple_of(gi * 8, 8)
        a8 = a_s[pl.ds(off, 8), :]
        b8 = b_s[pl.ds(off, 8), :]
        for s in (1, 2, 4):
            a_sh = jnp.where(row >= s, pltpu.roll(a8, s, 0), 1.0)
            b_sh = jnp.where(row >= s, pltpu.roll(b8, s, 0), 0.0)
            b8 = a8 * b_sh + b8
            a8 = a8 * a_sh
        h8 = b8 + a8 * h
        b_s[pl.ds(off, 8), :] = h8
        return h8[7:8, :]

    hcar[0:1, :] = lax.fori_loop(0, tt // 8, body, hcar[0:1, :])
    o_ref[...] = b_s[...] * _gelu_tanh(gr_ref[...])


def _rglru(xr, gr, conv_w, conv_b, wrg, b_rg, wig, b_ig, lam, tt=512):
    t = xr.shape[0]
    row = lambda a: a.reshape(1, LRU_W)
    full = lambda shape: pl.BlockSpec(shape, lambda i: (0, 0))
    return pl.pallas_call(
        functools.partial(_rglru_kernel, tt=tt),
        grid=(t // tt,),
        in_specs=[pl.BlockSpec((tt, LRU_W), lambda i: (i, 0)),
                  pl.BlockSpec((tt, LRU_W), lambda i: (i, 0)),
                  full((CONV_W, LRU_W)), full((1, LRU_W)),
                  full((LRU_W, LRU_W)), full((1, LRU_W)),
                  full((LRU_W, LRU_W)), full((1, LRU_W)), full((1, LRU_W))],
        out_specs=pl.BlockSpec((tt, LRU_W), lambda i: (i, 0)),
        out_shape=jax.ShapeDtypeStruct((t, LRU_W), F32),
        scratch_shapes=[pltpu.VMEM((tt + 8, LRU_W), F32),
                        pltpu.VMEM((tt, LRU_W), F32),
                        pltpu.VMEM((tt, LRU_W), F32),
                        pltpu.VMEM((8, LRU_W), F32)],
        compiler_params=_params("arbitrary"),
        name="rglru",
    )(xr, gr, conv_w, row(conv_b), wrg, row(b_rg), wig, row(b_ig), row(lam))


def _post(y, g_ref, x_ref, o_ref):
    ms = jnp.mean(y * y, axis=-1, keepdims=True)
    o_ref[...] = x_ref[...] + y * lax.rsqrt(ms + EPS) * g_ref[...]


def _outproj_ab_kernel(a1_ref, a2_ref, w1_ref, w2_ref, g_ref, x_ref, o_ref):
    y = (jnp.dot(a1_ref[...].astype(BF16), w1_ref[...], preferred_element_type=F32)
         + jnp.dot(a2_ref[...].astype(BF16), w2_ref[...], preferred_element_type=F32))
    _post(y, g_ref, x_ref, o_ref)


def _outproj_ab(a1, a2, w1, w2, g, x, tm=512):
    t = x.shape[0]
    rows = lambda w: pl.BlockSpec((tm, w), lambda i: (i, 0))
    full = lambda shape: pl.BlockSpec(shape, lambda i: (0, 0))
    return pl.pallas_call(
        _outproj_ab_kernel,
        grid=(t // tm,),
        in_specs=[rows(a1.shape[1]), rows(a2.shape[1]), full(w1.shape), full(w2.shape),
                  full((1, D)), rows(D)],
        out_specs=rows(D),
        out_shape=jax.ShapeDtypeStruct((t, D), F32),
        compiler_params=_params("arbitrary"),
        name="outproj_ab",
    )(a1, a2, w1, w2, g.reshape(1, D), x)


def _outproj_c_kernel(oc_ref, os_ref, ow_ref, gl_ref, w_ref, g_ref, x_ref, o_ref):
    gates = jax.nn.sigmoid(gl_ref[...])
    parts = []
    for h in range(HC):
        sl = slice(h * DHC, (h + 1) * DHC)
        parts.append(gates[:, 3 * h:3 * h + 1] * oc_ref[:, sl]
                     + gates[:, 3 * h + 1:3 * h + 2] * os_ref[:, sl]
                     + gates[:, 3 * h + 2:3 * h + 3] * ow_ref[:, sl])
    o = jnp.concatenate(parts, axis=1).astype(BF16)
    _post(jnp.dot(o, w_ref[...], preferred_element_type=F32), g_ref, x_ref, o_ref)


def _outproj_c(oc, os_, ow, gl, w, g, x, tm=512):
    t = x.shape[0]
    rows = lambda w_: pl.BlockSpec((tm, w_), lambda i: (i, 0))
    full = lambda shape: pl.BlockSpec(shape, lambda i: (0, 0))
    return pl.pallas_call(
        _outproj_c_kernel,
        grid=(t // tm,),
        in_specs=[rows(D), rows(D), rows(D), rows(128), full(w.shape), full((1, D)), rows(D)],
        out_specs=rows(D),
        out_shape=jax.ShapeDtypeStruct((t, D), F32),
        compiler_params=_params("arbitrary"),
        name="outproj_c",
    )(oc, os_, ow, gl, w, g.reshape(1, D), x)


def _mlp_kernel(x_ref, gpre_ref, wup_ref, wdn_ref, gpost_ref, o_ref, h_s, acc_s):
    c = pl.program_id(1)

    @pl.when(c == 0)
    def _():
        x = x_ref[...]
        ms = jnp.mean(x * x, axis=-1, keepdims=True)
        h_s[...] = (x * lax.rsqrt(ms + EPS) * gpre_ref[...]).astype(BF16)
        acc_s[...] = jnp.zeros_like(acc_s)

    u = jnp.maximum(jnp.dot(h_s[...], wup_ref[...], preferred_element_type=F32), 0.0)
    acc_s[...] += jnp.dot((u * u).astype(BF16), wdn_ref[...], preferred_element_type=F32)

    @pl.when(c == pl.num_programs(1) - 1)
    def _():
        _post(acc_s[...], gpost_ref, x_ref, o_ref)


def _mlp(x, gpre, wup, wdn, gpost, tm=1024, tf=512):
    t = x.shape[0]
    return pl.pallas_call(
        _mlp_kernel,
        grid=(t // tm, D_FF // tf),
        in_specs=[pl.BlockSpec((tm, D), lambda i, c: (i, 0)),
                  pl.BlockSpec((1, D), lambda i, c: (0, 0)),
                  pl.BlockSpec((D, tf), lambda i, c: (0, c)),
                  pl.BlockSpec((tf, D), lambda i, c: (c, 0)),
                  pl.BlockSpec((1, D), lambda i, c: (0, 0))],
        out_specs=pl.BlockSpec((tm, D), lambda i, c: (i, 0)),
        out_shape=jax.ShapeDtypeStruct((t, D), F32),
        scratch_shapes=[pltpu.VMEM((tm, D), BF16), pltpu.VMEM((tm, D), F32)],
        compiler_params=_params("arbitrary", "arbitrary"),
        name="mlp",
    )(x, gpre.reshape(1, D), wup, wdn, gpost.reshape(1, D))


def _compress_kernel(h_ref, pos_ref, w1_ref, w2_ref, o_ref):
    half = CMP_STRIDE * DHC
    hm = h_ref[0]
    top = jnp.dot((hm + pos_ref[:, 0:half]).astype(BF16), w1_ref[0:half, :], preferred_element_type=F32)
    bot = jnp.dot((hm + pos_ref[:, half:2 * half]).astype(BF16), w1_ref[half:2 * half, :],
                  preferred_element_type=F32)
    hid = top + pltpu.roll(bot, bot.shape[0] - 1, 0)
    o_ref[0] = jnp.dot(_gelu_tanh(hid).astype(BF16), w2_ref[...], preferred_element_type=F32).astype(o_ref.dtype)


def _compress(hmat, pos, w1, w2):
    g, n, wd = hmat.shape
    return pl.pallas_call(
        _compress_kernel,
        grid=(g,),
        in_specs=[pl.BlockSpec((1, n, wd), lambda i: (i, 0, 0)),
                  pl.BlockSpec((1, 2 * wd), lambda i: (0, 0)),
                  pl.BlockSpec(w1.shape, lambda i: (0, 0)),
                  pl.BlockSpec(w2.shape, lambda i: (0, 0))],
        out_specs=pl.BlockSpec((1, n, DHC), lambda i: (i, 0, 0)),
        out_shape=jax.ShapeDtypeStruct((g, n, DHC), BF16),
        compiler_params=_params("arbitrary"),
        name="nsa_compress",
    )(hmat, pos.reshape(1, 2 * wd), w1, w2)


def _stack_heads(q):
    return jnp.concatenate([q[:, p * DHC:(p + 1) * DHC] for p in range(PC)], axis=0)


def _nsa_cmp_kernel(tab_ref, q_ref, kc_ref, vc_ref, ov_ref, *refs, tq, ns, nwb):
    kw_refs, vw_refs = refs[:nwb], refs[nwb:2 * nwb]
    oc_ref, sb_ref, ow_ref, e_ref, qb_ref, fw_ref = refs[2 * nwb:]
    qi = pl.program_id(0)
    ncp = kc_ref.shape[1]
    nsp = ov_ref.shape[1]
    scale = DHC ** -0.5 * LOG2E
    cpq = tq // CMP_STRIDE
    k = min(SEL_TOPK, ns)

    @pl.when(qi == 0)
    def _():
        u = lax.broadcasted_iota(jnp.int32, (2 * ncp, 128), 0)
        lane = lax.broadcasted_iota(jnp.int32, (2 * ncp, 128), 1)
        hit = (lane < 2 * CMP_BIAS_LANES) & (u + lane % CMP_BIAS_LANES == ncp - 1)
        after = (lane == 2 * CMP_BIAS_LANES) & (u >= ncp)
        e_ref[...] = (jnp.where(hit, 1.0, 0.0) + jnp.where(after, 1.0, 0.0)).astype(BF16)
        i = lax.broadcasted_iota(jnp.int32, (tq, 128), 0)
        lane = lax.broadcasted_iota(jnp.int32, (tq, 128), 1)
        d = i + CMP_STRIDE * (lane % CMP_BIAS_LANES) - (tq - CMP_STRIDE + CMP_LEN - 1)
        dw = (WINDOW + lax.broadcasted_iota(jnp.int32, (tq, nwb * tq), 0)
              - lax.broadcasted_iota(jnp.int32, (tq, nwb * tq), 1))
        for h in range(HC):
            b = jnp.where(d >= 0, LOG2E * _bias_delta(jnp.maximum(d, 0), tab_ref, h), NEG)
            hi = b.astype(BF16).astype(F32)
            qb = jnp.where(lane < CMP_BIAS_LANES, hi, jnp.where(lane < 2 * CMP_BIAS_LANES, b - hi, 0.0))
            qb_ref[h] = jnp.where(lane == 2 * CMP_BIAS_LANES, NEG, qb).astype(BF16)
            fw_ref[h] = jnp.where((dw >= 0) & (dw < WINDOW),
                                  LOG2E * _bias_delta(jnp.maximum(dw, 0), tab_ref, h), NEG)

    off = pl.multiple_of(ncp - cpq * (qi + 1), 16)
    onehots = e_ref[pl.ds(off, ncp), :]
    jj = lax.broadcasted_iota(jnp.int32, (tq, nsp), 1)
    cur = (qi * tq + lax.broadcasted_iota(jnp.int32, (tq, nsp), 0)) // SEL_BLOCK
    forced = (jj == 0) | (jj == cur) | (jj == cur - 1)
    dead = (jj > cur) | (jj >= ns)
    q = q_ref[...]
    qs = [(q[:, h * DHC:(h + 1) * DHC] * scale).astype(BF16) for h in range(HC)]
    fallbacks = []
    for g in range(GC):
        heads = range(g * PC, (g + 1) * PC)
        qa = jnp.concatenate([jnp.concatenate([qs[h] for h in heads], axis=0),
                              jnp.concatenate([qb_ref[h] for h in heads], axis=0)], axis=1)
        s_all = _dot_t(qa, jnp.concatenate([kc_ref[g], onehots], axis=1))
        psum = jnp.zeros((tq, ncp), F32)
        for p, h in enumerate(heads):
            s = s_all[p * tq:(p + 1) * tq]
            m = jnp.maximum(jnp.max(s, axis=1, keepdims=True), CMP_MAX_FLOOR)
            e = jnp.exp2(s - m)
            pc = e * (1.0 / jnp.maximum(jnp.sum(e, axis=1, keepdims=True), 1e-30))
            oc_ref[:, h * DHC:(h + 1) * DHC] = jnp.dot(pc.astype(BF16), vc_ref[g], preferred_element_type=F32)
            psum = psum + pc

        hi = psum.astype(BF16)
        lo = (psum - hi.astype(F32)).astype(BF16)
        imp = (jnp.dot(hi, ov_ref[...], preferred_element_type=F32)
               + jnp.dot(lo, ov_ref[...], preferred_element_type=F32))
        rest = jnp.where(forced | dead, NEG, imp)
        sel_rest = _removed_topk(rest, k - 3)
        sel = sel_rest | (forced & jnp.logical_not(dead))
        sb_ref[g] = jnp.where(sel, 0.0, NEG).astype(BF16)
        count = jnp.sum(jnp.where(sel_rest, 1.0, 0.0), axis=1, keepdims=True)
        cand = jnp.sum(jnp.where(forced | dead, 0.0, 1.0), axis=1, keepdims=True)
        tied = jnp.max(jnp.where(count != jnp.minimum(cand, float(k - 3)), 1.0, 0.0))
        fallbacks.append((g, imp, tied))

    wk = jnp.concatenate([r[...] for r in kw_refs], axis=0)
    wv = jnp.concatenate([r[...] for r in vw_refs], axis=0)
    blk = lax.broadcasted_iota(jnp.int32, (nwb * tq, DHC), 0) // tq
    flags = jnp.where(blk + qi < nwb - 1, 1.0, 0.0).astype(BF16)
    off_lane = jnp.where(lax.broadcasted_iota(jnp.int32, (tq, DHC), 1) == 0, NEG, 0.0).astype(BF16)
    for h in range(HC):
        g = h // PC
        cols = slice(g * DHC, (g + 1) * DHC)
        s = _dot_t(jnp.concatenate([qs[h], off_lane], axis=1),
                   jnp.concatenate([wk[:, cols], flags], axis=1)) + fw_ref[h]
        e = jnp.exp2(s - jnp.max(s, axis=1, keepdims=True))
        o = jnp.dot(e.astype(BF16), wv[:, cols], preferred_element_type=F32)
        ow_ref[:, h * DHC:(h + 1) * DHC] = o * (1.0 / jnp.sum(e, axis=1, keepdims=True))

    for g, imp, tied in fallbacks:
        @pl.when(tied > 0.0)
        def _(g=g, imp=imp):
            full = jnp.where(dead, NEG, jnp.where(forced, imp + SEL_FORCE, imp))
            sb_ref[g] = jnp.where(_topk_mask(full, jj, k), 0.0, NEG).astype(BF16)


def _nsa_cmp_win(tab, q, kc, vc, overlap, kw, vw, ns, tq=TILE):
    t = q.shape[0]
    ncp = kc.shape[1]
    nsp = overlap.shape[1]
    assert (tq - CMP_STRIDE + CMP_LEN - 1 + MAX_DISTANCE) // CMP_STRIDE < CMP_BIAS_LANES
    assert WINDOW % tq == 0 and min(SEL_TOPK, ns) > 3
    nwb = WINDOW // tq + 1
    full = lambda shape: pl.BlockSpec(shape, lambda qi: (0,) * len(shape))
    rows = pl.BlockSpec((tq, HC * DHC), lambda qi: (qi, 0))
    back = [pl.BlockSpec((tq, GC * DHC), functools.partial(lambda qi, b: (jnp.maximum(qi - b, 0), 0), b=b))
            for b in range(nwb - 1, -1, -1)]
    return pl.pallas_call(
        functools.partial(_nsa_cmp_kernel, tq=tq, ns=ns, nwb=nwb),
        grid=(t // tq,),
        in_specs=[_smem(), rows, full((GC, ncp, DHC)), full((GC, ncp, DHC)), full((ncp, nsp))] + back + back,
        out_specs=[rows, pl.BlockSpec((GC, tq, nsp), lambda qi: (0, qi, 0)), rows],
        out_shape=[jax.ShapeDtypeStruct((t, HC * DHC), F32),
                   jax.ShapeDtypeStruct((GC, t, nsp), BF16),
                   jax.ShapeDtypeStruct((t, HC * DHC), F32)],
        scratch_shapes=[pltpu.VMEM((2 * ncp, 128), BF16), pltpu.VMEM((HC, tq, 128), BF16),
                        pltpu.VMEM((HC, tq, nwb * tq), F32)],
        compiler_params=_params("arbitrary"),
        name="nsa_compressed_window",
    )(tab, q, kc, vc, overlap, *([kw] * nwb), *([vw] * nwb))


def _nsa_sel_kernel(tab_ref, q_ref, sb_ref, ka_ref, va_ref, o_ref, f_ref, m_ref, acc_ref, s_ref, *, nhalf, nb,
                    unroll):
    qi = pl.program_id(0)
    scale = DHC ** -0.5 * LOG2E
    tiles_per_half = 128 * SEL_BLOCK // TILE

    @pl.when(qi == 0)
    def _():
        _near_bias_tiles(f_ref, tab_ref, list(range(HC)))

    hpc = PC // SEL_CHUNKS
    chunks = [(g, list(range(g * PC + c * hpc, g * PC + (c + 1) * hpc)))
              for g in range(GC) for c in range(SEL_CHUNKS)]
    q = q_ref[...]
    qas = []
    for g, heads in chunks:
        qc = jnp.concatenate([q[:, h * DHC:(h + 1) * DHC] for h in heads], axis=0)
        qc = (qc * scale).astype(BF16)
        sb = sb_ref[g]
        qas.append([jnp.concatenate([qc, jnp.concatenate([sb[:, hf * 128:(hf + 1) * 128]] * hpc, axis=0)], axis=1)
                    for hf in range(nhalf)])

    def qa_for(c, j):
        if nhalf == 1:
            return qas[c][0]
        return jnp.where(j >= tiles_per_half, qas[c][1], qas[c][0])

    def keys(g, off, n):
        return ka_ref[pl.ds(off, n), g * 2 * DHC:(g + 1) * 2 * DHC]

    def values(g, off, n):
        return va_ref[pl.ds(off, n), g * 2 * DHC:(g + 1) * 2 * DHC]

    far_tiles = unroll * (jnp.maximum(qi - 1, 0) // unroll)
    groups = far_tiles // unroll
    last = jnp.maximum(groups - 1, 0)

    def produce(buf, grp):
        off = pl.multiple_of(grp * (unroll * TILE), unroll * TILE)
        for c, (g, _) in enumerate(chunks):
            s_ref[buf, c] = _dot_t(qa_for(c, grp * unroll), keys(g, off, unroll * TILE))

    def consume(buf, grp):
        off = pl.multiple_of(grp * (unroll * TILE), unroll * TILE)
        for c, (g, _) in enumerate(chunks):
            _flash_update(s_ref[buf, c], values(g, off, unroll * TILE), m_ref.at[c], acc_ref.at[c], False)

    produce(0, 0)
    start = jnp.minimum(far_tiles, nb - (unroll + 1))
    for c, (g, heads) in enumerate(chunks):
        cols = []
        for u in range(unroll + 1):
            j = start + u
            s = _dot_t(qa_for(c, j), keys(g, pl.multiple_of(j * TILE, TILE), TILE))
            cols.append(jnp.concatenate(
                [s[i * TILE:(i + 1) * TILE] + _slot_bias(j, qi, far_tiles, f_ref[h, 0], f_ref[h, 1])
                 for i, h in enumerate(heads)], axis=0))
        vt = values(g, pl.multiple_of(start * TILE, TILE), (unroll + 1) * TILE)
        _flash_update(jnp.concatenate(cols, axis=1), vt, m_ref.at[c], acc_ref.at[c], True)

    def pair(k, carry):
        produce(1, 2 * k + 1)
        consume(0, 2 * k)
        produce(0, jnp.minimum(2 * k + 2, last))
        consume(1, 2 * k + 1)
        return carry

    lax.fori_loop(0, groups // 2, pair, 0)

    @pl.when(groups % 2 == 1)
    def _():
        consume(0, groups - 1)


    for c, (g, heads) in enumerate(chunks):
        acc = acc_ref[c]
        o = acc[:, :DHC] / acc[:, DHC:DHC + 1]
        for i, h in enumerate(heads):
            o_ref[:, h * DHC:(h + 1) * DHC] = o[i * TILE:(i + 1) * TILE]


def _nsa_sel(tab, q, selbias, kaug, vaug):
    t = q.shape[0]
    nsp = selbias.shape[2]
    nchunks = GC * SEL_CHUNKS
    rows = HC // nchunks * TILE
    once = pl.Buffered(1)
    unroll = min(SEL_UNROLL, t // TILE - 1)
    return pl.pallas_call(
        functools.partial(_nsa_sel_kernel, nhalf=nsp // 128, nb=t // TILE, unroll=unroll),
        grid=(t // TILE,),
        in_specs=[_smem(),
                  pl.BlockSpec((TILE, HC * DHC), lambda qi: (qi, 0)),
                  pl.BlockSpec((GC, TILE, nsp), lambda qi: (0, qi, 0)),
                  pl.BlockSpec((t, GC * 2 * DHC), lambda qi: (0, 0), pipeline_mode=once),
                  pl.BlockSpec((t, GC * 2 * DHC), lambda qi: (0, 0), pipeline_mode=once)],
        out_specs=pl.BlockSpec((TILE, HC * DHC), lambda qi: (qi, 0)),
        out_shape=jax.ShapeDtypeStruct((t, HC * DHC), F32),
        scratch_shapes=[pltpu.VMEM((HC, 2, TILE, TILE), F32),
                        pltpu.VMEM((nchunks, rows, 128), F32),
                        pltpu.VMEM((nchunks, rows, 2 * DHC), F32),
                        pltpu.VMEM((2, nchunks, rows, unroll * TILE), F32)],
        compiler_params=_params("arbitrary", vmem_limit=VMEM_LIMIT_RESIDENT),
        name="nsa_selected",
    )(tab, q, selbias, kaug, vaug)


def _layer_ab(x, tab, w_in, conv_w, conv_b, w_rg, b_rg, w_ig, b_ig, lam, w_out, g_pre, g_post):
    t = x.shape[0]
    da = HA * DHA
    wq, wk, wv = (w_in[:, i * da:(i + 1) * da] for i in range(3))
    wx = w_in[:, 3 * da:]
    wk3 = wk.reshape(D, HA, DHA)
    wv3 = wv.reshape(D, HA, DHA)
    zeros = jnp.zeros_like(wk3)
    even = (jnp.arange(HA) % 2 == 0)[None, :, None]
    wk_aug = jnp.concatenate([jnp.where(even, wk3, zeros), jnp.where(even, zeros, wk3)], axis=-1)
    wv_aug = jnp.concatenate([wv3, zeros], axis=-1)
    w = jnp.concatenate([wq, wk, wx, wk_aug.reshape(D, HA * 128), wv_aug.reshape(D, HA * 128)],
                        axis=1).astype(BF16)
    segs = _segments([(da, None), (da, None), (LRU_W, None), (LRU_W, None),
                      (HA * 128, ("onehot", 128, MOBA_BLOCK, True)), (HA * 128, ("ones", 128, DHA))])
    q, k, xr, gr, kaug, vaug = _rms_proj(x, g_pre, w, segs, [F32, F32, F32, F32, BF16, BF16])

    nb = t // MOBA_BLOCK
    kmean = jnp.pad(_block_mean(k), ((0, DHA - nb), (0, 0)))
    o_a = _moba(tab, q, jnp.concatenate([kmean, kmean], axis=0), kaug, vaug)

    eye = jnp.eye(w_rg.shape[0], dtype=F32)
    blockdiag = lambda wb: jnp.einsum('bij,bc->bicj', wb, eye).reshape(LRU_W, LRU_W).astype(BF16)
    o_b = _rglru(xr, gr, conv_w, conv_b, blockdiag(w_rg), b_rg, blockdiag(w_ig), b_ig, lam)
    wo = w_out.astype(BF16)
    return _outproj_ab(o_a, o_b, wo[:da], wo[da:], g_post, x)


def _layer_c(x, tab, w_in, pos_k, pos_v, k_w1, k_w2, v_w1, v_w2, w_out, g_pre, g_post):
    t = x.shape[0]
    dq = HC * DHC
    ckv = GC * DHC
    cols = lambda j: w_in[:, dq + j * ckv: dq + (j + 1) * ckv]
    widen = lambda wc: jnp.concatenate([wc.reshape(D, GC, DHC), jnp.zeros((D, GC, DHC), wc.dtype)],
                                       axis=-1).reshape(D, GC * 2 * DHC)
    wg = jnp.pad(w_in[:, dq + 6 * ckv:], ((0, 0), (0, 128 - 3 * HC)))
    w = jnp.concatenate([w_in[:, :dq], cols(0), cols(1), widen(cols(2)), widen(cols(3)), cols(4), cols(5), wg],
                        axis=1).astype(BF16)
    segs = _segments([(dq, None), (DHC, None), (DHC, None), (DHC, None), (DHC, None),
                      (GC * 2 * DHC, ("onehot", 2 * DHC, SEL_BLOCK, False)),
                      (GC * 2 * DHC, ("ones", 2 * DHC, DHC)),
                      (ckv, None), (ckv, None), (128, None)])
    q, kc0, kc1, vc0, vc1, ks_aug, vs_aug, kw, vw, gl = _rms_proj(
        x, g_pre, w, segs, [F32, F32, F32, F32, F32, BF16, BF16, BF16, BF16, F32])

    n = t // CMP_STRIDE
    hm = lambda a0, a1: jnp.stack([a0.reshape(n, CMP_STRIDE * DHC), a1.reshape(n, CMP_STRIDE * DHC)])
    kc = _compress(hm(kc0, kc1), pos_k, k_w1.astype(BF16), k_w2.astype(BF16))
    vc = _compress(hm(vc0, vc1), pos_v, v_w1.astype(BF16), v_w2.astype(BF16))

    ns = t // SEL_BLOCK
    nsp = -(-ns // 128) * 128
    cs = jnp.arange(n)[:, None] * CMP_STRIDE
    ss = jnp.arange(nsp)[None, :] * SEL_BLOCK
    overlap = ((cs < ss + SEL_BLOCK) & (cs + CMP_LEN > ss) & (jnp.arange(n)[:, None] < n - 1)
               & (jnp.arange(nsp)[None, :] < ns)).astype(BF16)
    o_c, selbias, o_w = _nsa_cmp_win(tab, q, kc, vc, overlap, kw, vw, ns)
    o_s = _nsa_sel(tab, q, selbias, ks_aug, vs_aug)
    return _outproj_c(o_c, o_s, o_w, gl, w_out.astype(BF16), g_post, x)


def kernel(x, rel_bias, ab_w_in, ab_conv_w, ab_conv_b, ab_w_rg, ab_b_rg, ab_w_ig, ab_b_ig, ab_lru_lambda, ab_w_out, c_w_in, c_cmp_pos_k, c_cmp_pos_v, c_cmp_k_w1, c_cmp_k_w2, c_cmp_v_w1, c_cmp_v_w2, c_w_out, norm_mix_pre, norm_mix_post, norm_mlp_pre, norm_mlp_post, w_up, w_down):
    batch = x.shape[0]
    depth = norm_mix_pre.shape[0]
    outs = []
    for b in range(batch):
        xb = x[b]
        for layer in range(depth):
            i = layer // 2
            if layer % 2 == 0:
                xb = _layer_ab(xb, rel_bias, ab_w_in[i], ab_conv_w[i], ab_conv_b[i], ab_w_rg[i], ab_b_rg[i],
                               ab_w_ig[i], ab_b_ig[i], ab_lru_lambda[i], ab_w_out[i],
                               norm_mix_pre[layer], norm_mix_post[layer])
            else:
                xb = _layer_c(xb, rel_bias, c_w_in[i], c_cmp_pos_k[i], c_cmp_pos_v[i], c_cmp_k_w1[i],
                              c_cmp_k_w2[i], c_cmp_v_w1[i], c_cmp_v_w2[i], c_w_out[i],
                              norm_mix_pre[layer], norm_mix_post[layer])
            xb = _mlp(xb, norm_mlp_pre[layer], w_up[layer].astype(BF16), w_down[layer].astype(BF16),
                      norm_mlp_post[layer])
        outs.append(xb)
    return jnp.stack(outs)
```

```python
import functools
import math

import numpy as np
import jax
import jax.numpy as jnp
from jax import lax
from jax.experimental import pallas as pl
from jax.experimental.pallas import tpu as pltpu

F32 = jnp.float32
BF16 = jnp.bfloat16
HIGHEST = lax.Precision.HIGHEST
LOG2E = 1.4426950408889634

D = 1024
EPS = 1e-6
NEG = -1e30
N_BUCKETS = 32
MAX_DISTANCE = 128

HA = 8
DHA = 64
MOBA_BLOCK = 256
MOBA_TOPK = 3
LRU_W = 512
LRU_C = 8.0
CONV_W = 4
HC = 8
DHC = 128
GC = 2
PC = 4
CMP_LEN = 32
CMP_STRIDE = 16
SEL_BLOCK = 64
SEL_TOPK = 16
WINDOW = 512
SEL_FORCE = 1e4
D_FF = 4096

TILE = 256
MOBA_UNROLL = 2
SEL_UNROLL = 2
SEL_CHUNKS = 2
CMP_BIAS_LANES = 32
CMP_MAX_FLOOR = -1e20
VMEM_LIMIT = 56 * 1024 * 1024
VMEM_LIMIT_RESIDENT = 60 * 1024 * 1024


def _bucket_np(d):
    n = np.maximum(d, 0)
    exact = N_BUCKETS // 2
    nf = np.maximum(n, 1).astype(np.float32)
    large = exact + (np.log(nf / np.float32(exact)) / np.float32(math.log(MAX_DISTANCE / exact))
                     * np.float32(N_BUCKETS - exact)).astype(np.int32)
    return np.where(n < exact, n, np.minimum(large, N_BUCKETS - 1))


_THR = [int(np.argmax(_bucket_np(np.arange(0, 4 * MAX_DISTANCE)) >= k)) for k in range(1, N_BUCKETS)]
assert _THR[-1] <= MAX_DISTANCE


def _bias_delta(d, tab_ref, h):
    acc = jnp.full(d.shape, tab_ref[0, h] - tab_ref[N_BUCKETS - 1, h], F32)
    for k in range(1, N_BUCKETS):
        acc = acc + jnp.where(d >= _THR[k - 1], tab_ref[k, h] - tab_ref[k - 1, h], 0.0)
    return acc


def _dot_t(a, b, **kw):
    return lax.dot_general(a, b, (((1,), (1,)), ((), ())), preferred_element_type=F32, **kw)


def _gelu_tanh(x):
    return 0.5 * x * (1.0 + jnp.tanh(0.7978845608028654 * (x + 0.044715 * x * x * x)))


def _sigmoid(x):
    return 0.5 * jnp.tanh(0.5 * x) + 0.5


def _params(*sem, vmem_limit=VMEM_LIMIT):
    return pltpu.CompilerParams(dimension_semantics=sem, vmem_limit_bytes=vmem_limit)


def _smem():
    return pl.BlockSpec(memory_space=pltpu.SMEM)


def _proj_kernel(x_ref, g_ref, w_ref, *out_refs, segs, tm):
    x = x_ref[...]
    ms = jnp.mean(x * x, axis=-1, keepdims=True)
    hn = (x * lax.rsqrt(ms + EPS) * g_ref[...]).astype(BF16)
    row0 = pl.program_id(0) * tm
    for o_ref, (start, width, aug) in zip(out_refs, segs):
        y = jnp.dot(hn, w_ref[:, start:start + width], preferred_element_type=F32)
        if aug is not None:
            lane = lax.broadcasted_iota(jnp.int32, y.shape, 1)
            if aug[0] == "onehot":
                _, group, rows_per_block, lo_first = aug
                half = group // 2
                row = row0 + lax.broadcasted_iota(jnp.int32, y.shape, 0)
                which_half = (lane // half) % 2
                data_half = (lane // group) % 2 if lo_first else jnp.zeros_like(lane)
                blk = (row // rows_per_block) % half
                y = jnp.where((which_half != data_half) & (lane % half == blk), 1.0, y)
            else:
                _, group, col = aug
                y = jnp.where(lane % group == col, 1.0, y)
        o_ref[...] = y.astype(o_ref.dtype)


def _rms_proj(x, g, w, segs, dtypes, tm=512):
    t = x.shape[0]
    return pl.pallas_call(
        functools.partial(_proj_kernel, segs=segs, tm=tm),
        grid=(t // tm,),
        in_specs=[pl.BlockSpec((tm, D), lambda i: (i, 0)),
                  pl.BlockSpec((1, D), lambda i: (0, 0)),
                  pl.BlockSpec(w.shape, lambda i: (0, 0))],
        out_specs=[pl.BlockSpec((tm, wd), lambda i: (i, 0)) for (_, wd, _) in segs],
        out_shape=[jax.ShapeDtypeStruct((t, wd), dt) for (_, wd, _), dt in zip(segs, dtypes)],
        compiler_params=_params("arbitrary"),
        name="rms_proj",
    )(x, g.reshape(1, D), w)


def _segments(widths_augs):
    segs, off = [], 0
    for wd, aug in widths_augs:
        segs.append((off, wd, aug))
        off += wd
    return segs


def _topk_mask(r, lane, k):
    sel = jnp.zeros(r.shape, jnp.bool_)
    lanef = lane.astype(F32)
    for _ in range(k):
        mx = jnp.max(r, axis=1, keepdims=True)
        first = jnp.min(jnp.where(r == mx, lanef, 1e9), axis=1, keepdims=True)
        pick = lanef == first
        sel = sel | (pick & (mx > NEG / 2))
        r = jnp.where(pick, NEG, r)
    return sel


def _removed_topk(r0, k):
    r = r0
    for _ in range(k):
        r = jnp.where(r == jnp.max(r, axis=1, keepdims=True), NEG, r)
    return (r0 > NEG / 2) & (r < NEG / 2)


def _near_bias_tiles(f_ref, tab_ref, heads):
    d0 = (lax.broadcasted_iota(jnp.int32, (TILE, TILE), 0)
          - lax.broadcasted_iota(jnp.int32, (TILE, TILE), 1))
    for i, h in enumerate(heads):
        f_ref[i, 0] = jnp.where(d0 >= 0, LOG2E * _bias_delta(jnp.maximum(d0, 0), tab_ref, h), NEG)
        f_ref[i, 1] = LOG2E * _bias_delta(d0 + TILE, tab_ref, h)


def _slot_bias(j, qi, lo, f_diag, f_prev):
    other = jnp.where((j >= lo) & (j <= qi), 0.0, NEG)
    return jnp.where(j == qi, f_diag, jnp.where(j == qi - 1, f_prev, other))


def _flash_update(s, vt, m_ref, acc_ref, first):
    rm = jnp.max(s, axis=1, keepdims=True)
    if first:
        m_ref[...] = jnp.broadcast_to(rm, m_ref.shape)
        p = jnp.exp2(s - rm)
        acc_ref[...] = jnp.dot(p.astype(BF16), vt, preferred_element_type=F32)
        return
    m_old = m_ref[...]
    m_new = jnp.maximum(m_old, rm)
    alpha = jnp.exp2(m_old - m_new)
    p = jnp.exp2(s - jnp.concatenate([m_new] * (s.shape[1] // m_new.shape[1]), axis=1))
    pv = jnp.dot(p.astype(BF16), vt, preferred_element_type=F32)
    acc_ref[...] = jnp.concatenate([alpha] * (pv.shape[1] // alpha.shape[1]), axis=1) * acc_ref[...] + pv
    m_ref[...] = m_new


def _moba_route_kernel(q_ref, k_ref, qa_ref, km_ref):
    qi = pl.program_id(0)
    scale = DHA ** -0.5 * LOG2E
    lane = lax.broadcasted_iota(jnp.int32, (TILE, 128), 1)

    @pl.when(qi == 0)
    def _():
        km_ref[...] = jnp.zeros_like(km_ref)

    def augmented_query(q, head_lanes, in_aug, blk, sel):
        keep = sel | (in_aug & (blk == qi))
        return jnp.where(head_lanes, q * scale, jnp.where(keep, 0.0, NEG)).astype(BF16)

    fallbacks = []
    for hh in range(HA):
        pair = slice(hh // 2 * 128, (hh // 2 + 1) * 128)
        q = q_ref[:, pair]
        head_lanes = (lane < DHA) if hh % 2 == 0 else (lane >= DHA)
        qh = jnp.where(head_lanes, q, 0.0)
        route = _dot_t(qh, km_ref[:, pair], precision=HIGHEST)
        blk = lane - DHA if hh % 2 == 0 else lane
        in_aug = jnp.logical_not(head_lanes)
        cand = in_aug & (blk < qi)
        r = jnp.where(cand, route, NEG)
        sel = _removed_topk(r, MOBA_TOPK)
        qa_ref[:, hh * 128:(hh + 1) * 128] = augmented_query(q, head_lanes, in_aug, blk, sel)
        count = jnp.sum(jnp.where(sel, 1.0, 0.0), axis=1, keepdims=True)
        ncand = jnp.sum(jnp.where(cand, 1.0, 0.0), axis=1, keepdims=True)
        tied = jnp.max(jnp.where(count != jnp.minimum(ncand, float(MOBA_TOPK)), 1.0, 0.0))
        fallbacks.append((hh, tied, (q, head_lanes, in_aug, blk), r))

    mean = jnp.mean(k_ref[...], axis=0, keepdims=True)
    km_ref[pl.ds(qi, 1), :] = mean
    km_ref[pl.ds(qi + DHA, 1), :] = mean

    for hh, tied, args, r in fallbacks:
        @pl.when(tied > 0.0)
        def _(hh=hh, args=args, r=r):
            qa_ref[:, hh * 128:(hh + 1) * 128] = augmented_query(*args, _topk_mask(r, lane, MOBA_TOPK))


def _moba_route(q, k):
    t = q.shape[0]
    assert t // MOBA_BLOCK <= DHA and MOBA_BLOCK == TILE
    rows = lambda w: pl.BlockSpec((TILE, w), lambda qi: (qi, 0))
    return pl.pallas_call(
        _moba_route_kernel,
        grid=(t // TILE,),
        in_specs=[rows(HA * DHA), rows(HA * DHA)],
        out_specs=rows(HA * 128),
        out_shape=jax.ShapeDtypeStruct((t, HA * 128), BF16),
        scratch_shapes=[pltpu.VMEM((2 * DHA, HA * DHA), F32)],
        compiler_params=_params("arbitrary"),
        name="moba_route",
    )(q, k)


def _moba_kernel(tab_ref, qa_ref, k_ref, v_ref, o_ref, f_ref, m_ref, acc_ref, s_ref, *, nb, unroll):
    qi = pl.program_id(0)
    lane = lax.broadcasted_iota(jnp.int32, (TILE, 128), 1)

    @pl.when(qi == 0)
    def _():
        _near_bias_tiles(f_ref, tab_ref, list(range(HA)))

    def own_lanes(hh, rows):
        ln = lax.broadcasted_iota(jnp.int32, (rows, 128), 1)
        return (ln < DHA) if hh % 2 == 0 else (ln >= DHA)

    def keys(j, hh, n):
        pair = slice(hh // 2 * 128, (hh // 2 + 1) * 128)
        parts = []
        for u in range(n):
            off = pl.multiple_of((j + u) * TILE, TILE)
            hot = jnp.where(lane == (j + u) + (DHA if hh % 2 == 0 else 0), 1.0, 0.0).astype(BF16)
            parts.append(jnp.where(own_lanes(hh, TILE), k_ref[pl.ds(off, TILE), pair], hot))
        return jnp.concatenate(parts, axis=0)

    def values(j, hh, n):
        pair = slice(hh // 2 * 128, (hh // 2 + 1) * 128)
        off = pl.multiple_of(j * TILE, TILE)
        ln = lax.broadcasted_iota(jnp.int32, (n * TILE, 128), 1)
        ones_col = jnp.where(ln == (DHA if hh % 2 == 0 else 0), 1.0, 0.0).astype(BF16)
        return jnp.where(own_lanes(hh, n * TILE), v_ref[pl.ds(off, n * TILE), pair], ones_col)

    def scores(j, hh, n):
        return _dot_t(qa_ref[:, hh * 128:(hh + 1) * 128], keys(j, hh, n))

    far_tiles = unroll * (jnp.maximum(qi - 1, 0) // unroll)
    groups = far_tiles // unroll
    last = jnp.maximum(groups - 1, 0)

    def produce(buf, grp):
        for hh in range(HA):
            s_ref[buf, hh] = scores(grp * unroll, hh, unroll)

    def consume(buf, grp):
        for hh in range(HA):
            _flash_update(s_ref[buf, hh], values(grp * unroll, hh, unroll), m_ref.at[hh], acc_ref.at[hh], False)

    produce(0, 0)
    start = jnp.minimum(far_tiles, nb - (unroll + 1))
    for hh in range(HA):
        bias = jnp.concatenate([_slot_bias(start + u, qi, far_tiles, f_ref[hh, 0], f_ref[hh, 1])
                                for u in range(unroll + 1)], axis=1)
        _flash_update(scores(start, hh, unroll + 1) + bias, values(start, hh, unroll + 1),
                      m_ref.at[hh], acc_ref.at[hh], True)

    def pair(k, c):
        produce(1, 2 * k + 1)
        consume(0, 2 * k)
        produce(0, jnp.minimum(2 * k + 2, last))
        consume(1, 2 * k + 1)
        return c

    lax.fori_loop(0, groups // 2, pair, 0)

    @pl.when(groups % 2 == 1)
    def _():
        consume(0, groups - 1)

    for pp in range(HA // 2):
        even, odd = acc_ref[2 * pp], acc_ref[2 * pp + 1]
        o_ref[:, pp * 128:(pp + 1) * 128] = jnp.where(lane < DHA, even * (1.0 / even[:, DHA:DHA + 1]),
                                                      odd * (1.0 / odd[:, 0:1]))


def _moba(tab, qa, k, v):
    t = qa.shape[0]
    once = pl.Buffered(1)
    unroll = min(MOBA_UNROLL, t // TILE - 1)
    return pl.pallas_call(
        functools.partial(_moba_kernel, nb=t // TILE, unroll=unroll),
        grid=(t // TILE,),
        in_specs=[_smem(),
                  pl.BlockSpec((TILE, HA * 128), lambda qi: (qi, 0)),
                  pl.BlockSpec((t, HA * DHA), lambda qi: (0, 0), pipeline_mode=once),
                  pl.BlockSpec((t, HA * DHA), lambda qi: (0, 0), pipeline_mode=once)],
        out_specs=pl.BlockSpec((TILE, HA * DHA), lambda qi: (qi, 0)),
        out_shape=jax.ShapeDtypeStruct((t, HA * DHA), F32),
        scratch_shapes=[pltpu.VMEM((HA, 2, TILE, TILE), F32),
                        pltpu.VMEM((HA, TILE, 128), F32),
                        pltpu.VMEM((HA, TILE, 128), F32),
                        pltpu.VMEM((2, HA, TILE, unroll * TILE), F32)],
        compiler_params=_params("arbitrary", vmem_limit=VMEM_LIMIT_RESIDENT),
        name="moba_attention",
    )(tab, qa, k, v)


def _rglru_kernel(xr_ref, gr_ref, cw_ref, cb_ref, wrg_ref, brg_ref, wig_ref, big_ref, lam_ref,
                  o_ref, xbuf, a_s, b_s, hcar, *, tt):
    @pl.when(pl.program_id(0) == 0)
    def _():
        xbuf[0:8, :] = jnp.zeros((8, LRU_W), F32)
        hcar[...] = jnp.zeros_like(hcar)

    xbuf[8:8 + tt, :] = xr_ref[...]
    xc = cb_ref[...]
    for j in range(CONV_W):
        xc = xc + xbuf[8 - (CONV_W - 1) + j:8 - (CONV_W - 1) + j + tt, :] * cw_ref[j:j + 1, :]
    xbuf[0:8, :] = xbuf[tt:tt + 8, :]

    xcb = xc.astype(BF16)
    r = _sigmoid(jnp.dot(xcb, wrg_ref[...], preferred_element_type=F32) + brg_ref[...])
    ig = _sigmoid(jnp.dot(xcb, wig_ref[...], preferred_element_type=F32) + big_ref[...])
    z = -lam_ref[...]
    softplus = jnp.maximum(z, 0.0) + jnp.log(1.0 + jnp.exp(-jnp.abs(z)))
    a = jnp.exp(-LRU_C * r * softplus)
    a_s[...] = a
    b_s[...] = jnp.sqrt(1.0 - a * a) * ig * xc

    row = lax.broadcasted_iota(jnp.int32, (8, LRU_W), 0)

    def body(gi, h):
        off = pl.multiple_of(gi * 8, 8)
        a8 = a_s[pl.ds(off, 8), :]
        b8 = b_s[pl.ds(off, 8), :]
        for s in (1, 2, 4):
            a_sh = jnp.where(row >= s, pltpu.roll(a8, s, 0), 1.0)
            b_sh = jnp.where(row >= s, pltpu.roll(b8, s, 0), 0.0)
            b8 = a8 * b_sh + b8
            a8 = a8 * a_sh
        h8 = b8 + a8 * h
        b_s[pl.ds(off, 8), :] = h8
        return h8[7:8, :]

    hcar[0:1, :] = lax.fori_loop(0, tt // 8, body, hcar[0:1, :])
    o_ref[...] = b_s[...] * _gelu_tanh(gr_ref[...])


def _rglru(xr, gr, conv_w, conv_b, wrg, b_rg, wig, b_ig, lam, tt=512):
    t = xr.shape[0]
    row = lambda a: a.reshape(1, LRU_W)
    full = lambda shape: pl.BlockSpec(shape, lambda i: (0, 0))
    return pl.pallas_call(
        functools.partial(_rglru_kernel, tt=tt),
        grid=(t // tt,),
        in_specs=[pl.BlockSpec((tt, LRU_W), lambda i: (i, 0)),
                  pl.BlockSpec((tt, LRU_W), lambda i: (i, 0)),
                  full((CONV_W, LRU_W)), full((1, LRU_W)),
                  full((LRU_W, LRU_W)), full((1, LRU_W)),
                  full((LRU_W, LRU_W)), full((1, LRU_W)), full((1, LRU_W))],
        out_specs=pl.BlockSpec((tt, LRU_W), lambda i: (i, 0)),
        out_shape=jax.ShapeDtypeStruct((t, LRU_W), F32),
        scratch_shapes=[pltpu.VMEM((tt + 8, LRU_W), F32),
                        pltpu.VMEM((tt, LRU_W), F32),
                        pltpu.VMEM((tt, LRU_W), F32),
                        pltpu.VMEM((8, LRU_W), F32)],
        compiler_params=_params("arbitrary"),
        name="rglru",
    )(xr, gr, conv_w, row(conv_b), wrg, row(b_rg), wig, row(b_ig), row(lam))


def _post(y, g_ref, x_ref, o_ref):
    ms = jnp.mean(y * y, axis=-1, keepdims=True)
    o_ref[...] = x_ref[...] + y * lax.rsqrt(ms + EPS) * g_ref[...]


def _outproj_ab_kernel(a1_ref, a2_ref, w1_ref, w2_ref, g_ref, x_ref, o_ref):
    y = (jnp.dot(a1_ref[...].astype(BF16), w1_ref[...], preferred_element_type=F32)
         + jnp.dot(a2_ref[...].astype(BF16), w2_ref[...], preferred_element_type=F32))
    _post(y, g_ref, x_ref, o_ref)


def _outproj_ab(a1, a2, w1, w2, g, x, tm=512):
    t = x.shape[0]
    rows = lambda w: pl.BlockSpec((tm, w), lambda i: (i, 0))
    full = lambda shape: pl.BlockSpec(shape, lambda i: (0, 0))
    return pl.pallas_call(
        _outproj_ab_kernel,
        grid=(t // tm,),
        in_specs=[rows(a1.shape[1]), rows(a2.shape[1]), full(w1.shape), full(w2.shape),
                  full((1, D)), rows(D)],
        out_specs=rows(D),
        out_shape=jax.ShapeDtypeStruct((t, D), F32),
        compiler_params=_params("arbitrary"),
        name="outproj_ab",
    )(a1, a2, w1, w2, g.reshape(1, D), x)


def _outproj_c_kernel(oc_ref, os_ref, ow_ref, gl_ref, w_ref, g_ref, x_ref, o_ref):
    gates = jax.nn.sigmoid(gl_ref[...])
    parts = []
    for h in range(HC):
        sl = slice(h * DHC, (h + 1) * DHC)
        parts.append(gates[:, 3 * h:3 * h + 1] * oc_ref[:, sl]
                     + gates[:, 3 * h + 1:3 * h + 2] * os_ref[:, sl]
                     + gates[:, 3 * h + 2:3 * h + 3] * ow_ref[:, sl])
    o = jnp.concatenate(parts, axis=1).astype(BF16)
    _post(jnp.dot(o, w_ref[...], preferred_element_type=F32), g_ref, x_ref, o_ref)


def _outproj_c(oc, os_, ow, gl, w, g, x, tm=512):
    t = x.shape[0]
    rows = lambda w_: pl.BlockSpec((tm, w_), lambda i: (i, 0))
    full = lambda shape: pl.BlockSpec(shape, lambda i: (0, 0))
    return pl.pallas_call(
        _outproj_c_kernel,
        grid=(t // tm,),
        in_specs=[rows(D), rows(D), rows(D), rows(128), full(w.shape), full((1, D)), rows(D)],
        out_specs=rows(D),
        out_shape=jax.ShapeDtypeStruct((t, D), F32),
        compiler_params=_params("arbitrary"),
        name="outproj_c",
    )(oc, os_, ow, gl, w, g.reshape(1, D), x)


def _mlp_kernel(x_ref, gpre_ref, wup_ref, wdn_ref, gpost_ref, o_ref, h_s, acc_s):
    c = pl.program_id(1)

    @pl.when(c == 0)
    def _():
        x = x_ref[...]
        ms = jnp.mean(x * x, axis=-1, keepdims=True)
        h_s[...] = (x * lax.rsqrt(ms + EPS) * gpre_ref[...]).astype(BF16)
        acc_s[...] = jnp.zeros_like(acc_s)

    u = jnp.maximum(jnp.dot(h_s[...], wup_ref[...], preferred_element_type=F32), 0.0)
    acc_s[...] += jnp.dot((u * u).astype(BF16), wdn_ref[...], preferred_element_type=F32)

    @pl.when(c == pl.num_programs(1) - 1)
    def _():
        _post(acc_s[...], gpost_ref, x_ref, o_ref)


def _mlp(x, gpre, wup, wdn, gpost, tm=1024, tf=512):
    t = x.shape[0]
    return pl.pallas_call(
        _mlp_kernel,
        grid=(t // tm, D_FF // tf),
        in_specs=[pl.BlockSpec((tm, D), lambda i, c: (i, 0)),
                  pl.BlockSpec((1, D), lambda i, c: (0, 0)),
                  pl.BlockSpec((D, tf), lambda i, c: (0, c)),
                  pl.BlockSpec((tf, D), lambda i, c: (c, 0)),
                  pl.BlockSpec((1, D), lambda i, c: (0, 0))],
        out_specs=pl.BlockSpec((tm, D), lambda i, c: (i, 0)),
        out_shape=jax.ShapeDtypeStruct((t, D), F32),
        scratch_shapes=[pltpu.VMEM((tm, D), BF16), pltpu.VMEM((tm, D), F32)],
        compiler_params=_params("arbitrary", "arbitrary"),
        name="mlp",
    )(x, gpre.reshape(1, D), wup, wdn, gpost.reshape(1, D))


def _compress_kernel(h_ref, pos_ref, w1_ref, w2_ref, o_ref):
    half = CMP_STRIDE * DHC
    hm = h_ref[0]
    top = jnp.dot((hm + pos_ref[:, 0:half]).astype(BF16), w1_ref[0:half, :], preferred_element_type=F32)
    bot = jnp.dot((hm + pos_ref[:, half:2 * half]).astype(BF16), w1_ref[half:2 * half, :],
                  preferred_element_type=F32)
    hid = top + pltpu.roll(bot, bot.shape[0] - 1, 0)
    o_ref[0] = jnp.dot(_gelu_tanh(hid).astype(BF16), w2_ref[...], preferred_element_type=F32).astype(o_ref.dtype)


def _compress(hmat, pos, w1, w2):
    g, n, wd = hmat.shape
    return pl.pallas_call(
        _compress_kernel,
        grid=(g,),
        in_specs=[pl.BlockSpec((1, n, wd), lambda i: (i, 0, 0)),
                  pl.BlockSpec((1, 2 * wd), lambda i: (0, 0)),
                  pl.BlockSpec(w1.shape, lambda i: (0, 0)),
                  pl.BlockSpec(w2.shape, lambda i: (0, 0))],
        out_specs=pl.BlockSpec((1, n, DHC), lambda i: (i, 0, 0)),
        out_shape=jax.ShapeDtypeStruct((g, n, DHC), BF16),
        compiler_params=_params("arbitrary"),
        name="nsa_compress",
    )(hmat, pos.reshape(1, 2 * wd), w1, w2)


def _stack_heads(q):
    return jnp.concatenate([q[:, p * DHC:(p + 1) * DHC] for p in range(PC)], axis=0)


def _nsa_cmp_kernel(tab_ref, q_ref, kc_ref, vc_ref, ov_ref, *refs, tq, ns, nwb):
    kw_refs, vw_refs = refs[:nwb], refs[nwb:2 * nwb]
    oc_ref, sb_ref, ow_ref, e_ref, qb_ref, fw_ref = refs[2 * nwb:]
    qi = pl.program_id(0)
    ncp = kc_ref.shape[1]
    nsp = ov_ref.shape[1]
    scale = DHC ** -0.5 * LOG2E
    cpq = tq // CMP_STRIDE
    k = min(SEL_TOPK, ns)

    @pl.when(qi == 0)
    def _():
        u = lax.broadcasted_iota(jnp.int32, (2 * ncp, 128), 0)
        lane = lax.broadcasted_iota(jnp.int32, (2 * ncp, 128), 1)
        hit = (lane < 2 * CMP_BIAS_LANES) & (u + lane % CMP_BIAS_LANES == ncp - 1)
        after = (lane == 2 * CMP_BIAS_LANES) & (u >= ncp)
        e_ref[...] = (jnp.where(hit, 1.0, 0.0) + jnp.where(after, 1.0, 0.0)).astype(BF16)
        i = lax.broadcasted_iota(jnp.int32, (tq, 128), 0)
        lane = lax.broadcasted_iota(jnp.int32, (tq, 128), 1)
        d = i + CMP_STRIDE * (lane % CMP_BIAS_LANES) - (tq - CMP_STRIDE + CMP_LEN - 1)
        dw = (WINDOW + lax.broadcasted_iota(jnp.int32, (tq, nwb * tq), 0)
              - lax.broadcasted_iota(jnp.int32, (tq, nwb * tq), 1))
        for h in range(HC):
            b = jnp.where(d >= 0, LOG2E * _bias_delta(jnp.maximum(d, 0), tab_ref, h), NEG)
            hi = b.astype(BF16).astype(F32)
            qb = jnp.where(lane < CMP_BIAS_LANES, hi, jnp.where(lane < 2 * CMP_BIAS_LANES, b - hi, 0.0))
            qb_ref[h] = jnp.where(lane == 2 * CMP_BIAS_LANES, NEG, qb).astype(BF16)
            fw_ref[h] = jnp.where((dw >= 0) & (dw < WINDOW),
                                  LOG2E * _bias_delta(jnp.maximum(dw, 0), tab_ref, h), NEG)

    off = pl.multiple_of(ncp - cpq * (qi + 1), 16)
    onehots = e_ref[pl.ds(off, ncp), :]
    jj = lax.broadcasted_iota(jnp.int32, (tq, nsp), 1)
    cur = (qi * tq + lax.broadcasted_iota(jnp.int32, (tq, nsp), 0)) // SEL_BLOCK
    forced = (jj == 0) | (jj == cur) | (jj == cur - 1)
    dead = (jj > cur) | (jj >= ns)
    q = q_ref[...]
    qs = [(q[:, h * DHC:(h + 1) * DHC] * scale).astype(BF16) for h in range(HC)]
    fallbacks = []
    for g in range(GC):
        heads = range(g * PC, (g + 1) * PC)
        qa = jnp.concatenate([jnp.concatenate([qs[h] for h in heads], axis=0),
                              jnp.concatenate([qb_ref[h] for h in heads], axis=0)], axis=1)
        s_all = _dot_t(qa, jnp.concatenate([kc_ref[g], onehots], axis=1))
        psum = jnp.zeros((tq, ncp), F32)
        for p, h in enumerate(heads):
            s = s_all[p * tq:(p + 1) * tq]
            m = jnp.maximum(jnp.max(s, axis=1, keepdims=True), CMP_MAX_FLOOR)
            e = jnp.exp2(s - m)
            pc = e * (1.0 / jnp.maximum(jnp.sum(e, axis=1, keepdims=True), 1e-30))
            oc_ref[:, h * DHC:(h + 1) * DHC] = jnp.dot(pc.astype(BF16), vc_ref[g], preferred_element_type=F32)
            psum = psum + pc

        hi = psum.astype(BF16)
        lo = (psum - hi.astype(F32)).astype(BF16)
        imp = (jnp.dot(hi, ov_ref[...], preferred_element_type=F32)
               + jnp.dot(lo, ov_ref[...], preferred_element_type=F32))
        rest = jnp.where(forced | dead, NEG, imp)
        sel_rest = _removed_topk(rest, k - 3)
        sel = sel_rest | (forced & jnp.logical_not(dead))
        sb_ref[g] = jnp.where(sel, 0.0, NEG).astype(BF16)
        count = jnp.sum(jnp.where(sel_rest, 1.0, 0.0), axis=1, keepdims=True)
        cand = jnp.sum(jnp.where(forced | dead, 0.0, 1.0), axis=1, keepdims=True)
        tied = jnp.max(jnp.where(count != jnp.minimum(cand, float(k - 3)), 1.0, 0.0))
        fallbacks.append((g, imp, tied))

    wk = jnp.concatenate([r[...] for r in kw_refs], axis=0)
    wv = jnp.concatenate([r[...] for r in vw_refs], axis=0)
    blk = lax.broadcasted_iota(jnp.int32, (nwb * tq, DHC), 0) // tq
    flags = jnp.where(blk + qi < nwb - 1, 1.0, 0.0).astype(BF16)
    off_lane = jnp.where(lax.broadcasted_iota(jnp.int32, (tq, DHC), 1) == 0, NEG, 0.0).astype(BF16)
    for h in range(HC):
        g = h // PC
        cols = slice(g * DHC, (g + 1) * DHC)
        s = _dot_t(jnp.concatenate([qs[h], off_lane], axis=1),
                   jnp.concatenate([wk[:, cols], flags], axis=1)) + fw_ref[h]
        e = jnp.exp2(s - jnp.max(s, axis=1, keepdims=True))
        o = jnp.dot(e.astype(BF16), wv[:, cols], preferred_element_type=F32)
        ow_ref[:, h * DHC:(h + 1) * DHC] = o * (1.0 / jnp.sum(e, axis=1, keepdims=True))

    for g, imp, tied in fallbacks:
        @pl.when(tied > 0.0)
        def _(g=g, imp=imp):
            full = jnp.where(dead, NEG, jnp.where(forced, imp + SEL_FORCE, imp))
            sb_ref[g] = jnp.where(_topk_mask(full, jj, k), 0.0, NEG).astype(BF16)


def _nsa_cmp_win(tab, q, kc, vc, overlap, kw, vw, ns, tq=TILE):
    t = q.shape[0]
    ncp = kc.shape[1]
    nsp = overlap.shape[1]
    assert (tq - CMP_STRIDE + CMP_LEN - 1 + MAX_DISTANCE) // CMP_STRIDE < CMP_BIAS_LANES
    assert WINDOW % tq == 0 and min(SEL_TOPK, ns) > 3
    nwb = WINDOW // tq + 1
    full = lambda shape: pl.BlockSpec(shape, lambda qi: (0,) * len(shape))
    rows = pl.BlockSpec((tq, HC * DHC), lambda qi: (qi, 0))
    back = [pl.BlockSpec((tq, GC * DHC), functools.partial(lambda qi, b: (jnp.maximum(qi - b, 0), 0), b=b))
            for b in range(nwb - 1, -1, -1)]
    return pl.pallas_call(
        functools.partial(_nsa_cmp_kernel, tq=tq, ns=ns, nwb=nwb),
        grid=(t // tq,),
        in_specs=[_smem(), rows, full((GC, ncp, DHC)), full((GC, ncp, DHC)), full((ncp, nsp))] + back + back,
        out_specs=[rows, pl.BlockSpec((GC, tq, nsp), lambda qi: (0, qi, 0)), rows],
        out_shape=[jax.ShapeDtypeStruct((t, HC * DHC), F32),
                   jax.ShapeDtypeStruct((GC, t, nsp), BF16),
                   jax.ShapeDtypeStruct((t, HC * DHC), F32)],
        scratch_shapes=[pltpu.VMEM((2 * ncp, 128), BF16), pltpu.VMEM((HC, tq, 128), BF16),
                        pltpu.VMEM((HC, tq, nwb * tq), F32)],
        compiler_params=_params("arbitrary"),
        name="nsa_compressed_window",
    )(tab, q, kc, vc, overlap, *([kw] * nwb), *([vw] * nwb))


def _nsa_sel_kernel(tab_ref, q_ref, sb_ref, ka_ref, va_ref, o_ref, f_ref, m_ref, acc_ref, s_ref, *, nhalf, nb,
                    unroll):
    qi = pl.program_id(0)
    scale = DHC ** -0.5 * LOG2E
    tiles_per_half = 128 * SEL_BLOCK // TILE

    @pl.when(qi == 0)
    def _():
        _near_bias_tiles(f_ref, tab_ref, list(range(HC)))

    hpc = PC // SEL_CHUNKS
    chunks = [(g, list(range(g * PC + c * hpc, g * PC + (c + 1) * hpc)))
              for g in range(GC) for c in range(SEL_CHUNKS)]
    q = q_ref[...]
    qas = []
    for g, heads in chunks:
        qc = jnp.concatenate([q[:, h * DHC:(h + 1) * DHC] for h in heads], axis=0)
        qc = (qc * scale).astype(BF16)
        sb = sb_ref[g]
        qas.append([jnp.concatenate([qc, jnp.concatenate([sb[:, hf * 128:(hf + 1) * 128]] * hpc, axis=0)], axis=1)
                    for hf in range(nhalf)])

    def qa_for(c, j):
        if nhalf == 1:
            return qas[c][0]
        return jnp.where(j >= tiles_per_half, qas[c][1], qas[c][0])

    def keys(g, off, n):
        return ka_ref[pl.ds(off, n), g * 2 * DHC:(g + 1) * 2 * DHC]

    def values(g, off, n):
        return va_ref[pl.ds(off, n), g * 2 * DHC:(g + 1) * 2 * DHC]

    far_tiles = unroll * (jnp.maximum(qi - 1, 0) // unroll)
    groups = far_tiles // unroll
    last = jnp.maximum(groups - 1, 0)

    def produce(buf, grp, only=None):
        off = pl.multiple_of(grp * (unroll * TILE), unroll * TILE)
        for c, (g, _) in enumerate(chunks):
            if only is None or c == only:
                s_ref[buf, c] = _dot_t(qa_for(c, grp * unroll), keys(g, off, unroll * TILE))

    def consume(buf, grp, only=None):
        off = pl.multiple_of(grp * (unroll * TILE), unroll * TILE)
        for c, (g, _) in enumerate(chunks):
            if only is None or c == only:
                _flash_update(s_ref[buf, c], values(g, off, unroll * TILE), m_ref.at[c], acc_ref.at[c], False)

    produce(0, 0)
    start = jnp.minimum(far_tiles, nb - (unroll + 1))
    for c, (g, heads) in enumerate(chunks):
        cols = []
        for u in range(unroll + 1):
            j = start + u
            s = _dot_t(qa_for(c, j), keys(g, pl.multiple_of(j * TILE, TILE), TILE))
            cols.append(jnp.concatenate(
                [s[i * TILE:(i + 1) * TILE] + _slot_bias(j, qi, far_tiles, f_ref[h, 0], f_ref[h, 1])
                 for i, h in enumerate(heads)], axis=0))
        vt = values(g, pl.multiple_of(start * TILE, TILE), (unroll + 1) * TILE)
        _flash_update(jnp.concatenate(cols, axis=1), vt, m_ref.at[c], acc_ref.at[c], True)

    def pair(k, carry):
        for c in range(len(chunks)):
            produce(1, 2 * k + 1, c)
            consume(0, 2 * k, c)
        for c in range(len(chunks)):
            produce(0, jnp.minimum(2 * k + 2, last), c)
            consume(1, 2 * k + 1, c)
        return carry

    lax.fori_loop(0, groups // 2, pair, 0)

    @pl.when(groups % 2 == 1)
    def _():
        consume(0, groups - 1)


    for c, (g, heads) in enumerate(chunks):
        acc = acc_ref[c]
        o = acc[:, :DHC] / acc[:, DHC:DHC + 1]
        for i, h in enumerate(heads):
            o_ref[:, h * DHC:(h + 1) * DHC] = o[i * TILE:(i + 1) * TILE]


def _nsa_sel(tab, q, selbias, kaug, vaug):
    t = q.shape[0]
    nsp = selbias.shape[2]
    nchunks = GC * SEL_CHUNKS
    rows = HC // nchunks * TILE
    once = pl.Buffered(1)
    unroll = min(SEL_UNROLL, t // TILE - 1)
    return pl.pallas_call(
        functools.partial(_nsa_sel_kernel, nhalf=nsp // 128, nb=t // TILE, unroll=unroll),
        grid=(t // TILE,),
        in_specs=[_smem(),
                  pl.BlockSpec((TILE, HC * DHC), lambda qi: (qi, 0)),
                  pl.BlockSpec((GC, TILE, nsp), lambda qi: (0, qi, 0)),
                  pl.BlockSpec((t, GC * 2 * DHC), lambda qi: (0, 0), pipeline_mode=once),
                  pl.BlockSpec((t, GC * 2 * DHC), lambda qi: (0, 0), pipeline_mode=once)],
        out_specs=pl.BlockSpec((TILE, HC * DHC), lambda qi: (qi, 0)),
        out_shape=jax.ShapeDtypeStruct((t, HC * DHC), F32),
        scratch_shapes=[pltpu.VMEM((HC, 2, TILE, TILE), F32),
                        pltpu.VMEM((nchunks, rows, 128), F32),
                        pltpu.VMEM((nchunks, rows, 2 * DHC), F32),
                        pltpu.VMEM((2, nchunks, rows, unroll * TILE), F32)],
        compiler_params=_params("arbitrary", vmem_limit=VMEM_LIMIT_RESIDENT),
        name="nsa_selected",
    )(tab, q, selbias, kaug, vaug)


def _layer_ab(x, tab, w_in, conv_w, conv_b, w_rg, b_rg, w_ig, b_ig, lam, w_out, g_pre, g_post):
    da = HA * DHA
    segs = [(0, da, None), (da, da, None), (da, da, None), (2 * da, da, None),
            (3 * da, LRU_W, None), (3 * da + LRU_W, LRU_W, None)]
    q, k32, k, v, xr, gr = _rms_proj(x, g_pre, w_in.astype(BF16), segs, [F32, F32, BF16, BF16, F32, F32])
    o_a = _moba(tab, _moba_route(q, k32), k, v)

    eye = jnp.eye(w_rg.shape[0], dtype=F32)
    blockdiag = lambda wb: jnp.einsum('bij,bc->bicj', wb, eye).reshape(LRU_W, LRU_W).astype(BF16)
    o_b = _rglru(xr, gr, conv_w, conv_b, blockdiag(w_rg), b_rg, blockdiag(w_ig), b_ig, lam)
    wo = w_out.astype(BF16)
    return _outproj_ab(o_a, o_b, wo[:da], wo[da:], g_post, x)


def _layer_c(x, tab, w_in, pos_k, pos_v, k_w1, k_w2, v_w1, v_w2, w_out, g_pre, g_post):
    t = x.shape[0]
    dq = HC * DHC
    ckv = GC * DHC
    cols = lambda j: w_in[:, dq + j * ckv: dq + (j + 1) * ckv]
    widen = lambda wc: jnp.concatenate([wc.reshape(D, GC, DHC), jnp.zeros((D, GC, DHC), wc.dtype)],
                                       axis=-1).reshape(D, GC * 2 * DHC)
    wg = jnp.pad(w_in[:, dq + 6 * ckv:], ((0, 0), (0, 128 - 3 * HC)))
    w = jnp.concatenate([w_in[:, :dq], cols(0), cols(1), widen(cols(2)), widen(cols(3)), cols(4), cols(5), wg],
                        axis=1).astype(BF16)
    segs = _segments([(dq, None), (DHC, None), (DHC, None), (DHC, None), (DHC, None),
                      (GC * 2 * DHC, ("onehot", 2 * DHC, SEL_BLOCK, False)),
                      (GC * 2 * DHC, ("ones", 2 * DHC, DHC)),
                      (ckv, None), (ckv, None), (128, None)])
    q, kc0, kc1, vc0, vc1, ks_aug, vs_aug, kw, vw, gl = _rms_proj(
        x, g_pre, w, segs, [F32, F32, F32, F32, F32, BF16, BF16, BF16, BF16, F32])

    n = t // CMP_STRIDE
    hm = lambda a0, a1: jnp.stack([a0.reshape(n, CMP_STRIDE * DHC), a1.reshape(n, CMP_STRIDE * DHC)])
    kc = _compress(hm(kc0, kc1), pos_k, k_w1.astype(BF16), k_w2.astype(BF16))
    vc = _compress(hm(vc0, vc1), pos_v, v_w1.astype(BF16), v_w2.astype(BF16))

    ns = t // SEL_BLOCK
    nsp = -(-ns // 128) * 128
    cs = jnp.arange(n)[:, None] * CMP_STRIDE
    ss = jnp.arange(nsp)[None, :] * SEL_BLOCK
    overlap = ((cs < ss + SEL_BLOCK) & (cs + CMP_LEN > ss) & (jnp.arange(n)[:, None] < n - 1)
               & (jnp.arange(nsp)[None, :] < ns)).astype(BF16)
    o_c, selbias, o_w = _nsa_cmp_win(tab, q, kc, vc, overlap, kw, vw, ns)
    o_s = _nsa_sel(tab, q, selbias, ks_aug, vs_aug)
    return _outproj_c(o_c, o_s, o_w, gl, w_out.astype(BF16), g_post, x)


def kernel(x, rel_bias, ab_w_in, ab_conv_w, ab_conv_b, ab_w_rg, ab_b_rg, ab_w_ig, ab_b_ig, ab_lru_lambda, ab_w_out, c_w_in, c_cmp_pos_k, c_cmp_pos_v, c_cmp_k_w1, c_cmp_k_w2, c_cmp_v_w1, c_cmp_v_w2, c_w_out, norm_mix_pre, norm_mix_post, norm_mlp_pre, norm_mlp_post, w_up, w_down):
    batch = x.shape[0]
    depth = norm_mix_pre.shape[0]
    outs = []
    for b in range(batch):
        xb = x[b]
        for layer in range(depth):
            i = layer // 2
            if layer % 2 == 0:
                xb = _layer_ab(xb, rel_bias, ab_w_in[i], ab_conv_w[i], ab_conv_b[i], ab_w_rg[i], ab_b_rg[i],
                               ab_w_ig[i], ab_b_ig[i], ab_lru_lambda[i], ab_w_out[i],
                               norm_mix_pre[layer], norm_mix_post[layer])
            else:
                xb = _layer_c(xb, rel_bias, c_w_in[i], c_cmp_pos_k[i], c_cmp_pos_v[i], c_cmp_k_w1[i],
                              c_cmp_k_w2[i], c_cmp_v_w1[i], c_cmp_v_w2[i], c_w_out[i],
                              norm_mix_pre[layer], norm_mix_post[layer])
            xb = _mlp(xb, norm_mlp_pre[layer], w_up[layer].astype(BF16), w_down[layer].astype(BF16),
                      norm_mlp_post[layer])
        outs.append(xb)
    return jnp.stack(outs)
```

```python
import functools
import math

import numpy as np
import jax
import jax.numpy as jnp
from jax import lax
from jax.experimental import pallas as pl
from jax.experimental.pallas import tpu as pltpu

F32 = jnp.float32
BF16 = jnp.bfloat16
HIGHEST = lax.Precision.HIGHEST
LOG2E = 1.4426950408889634

D = 1024
EPS = 1e-6
NEG = -1e30
N_BUCKETS = 32
MAX_DISTANCE = 128

HA = 8
DHA = 64
MOBA_BLOCK = 256
MOBA_TOPK = 3
LRU_W = 512
LRU_C = 8.0
CONV_W = 4
HC = 8
DHC = 128
GC = 2
PC = 4
CMP_LEN = 32
CMP_STRIDE = 16
SEL_BLOCK = 64
SEL_TOPK = 16
WINDOW = 512
SEL_FORCE = 1e4
D_FF = 4096

TILE = 256
MOBA_UNROLL = 2
SEL_UNROLL = 2
SEL_CHUNKS = 2
CMP_BIAS_LANES = 32
CMP_MAX_FLOOR = -1e20
CMP_WIDTH_CLASS = 256
BRANCH_DTYPE = BF16
VMEM_LIMIT = 56 * 1024 * 1024
VMEM_LIMIT_RESIDENT = 60 * 1024 * 1024


def _bucket_np(d):
    n = np.maximum(d, 0)
    exact = N_BUCKETS // 2
    nf = np.maximum(n, 1).astype(np.float32)
    large = exact + (np.log(nf / np.float32(exact)) / np.float32(math.log(MAX_DISTANCE / exact))
                     * np.float32(N_BUCKETS - exact)).astype(np.int32)
    return np.where(n < exact, n, np.minimum(large, N_BUCKETS - 1))


_THR = [int(np.argmax(_bucket_np(np.arange(0, 4 * MAX_DISTANCE)) >= k)) for k in range(1, N_BUCKETS)]
assert _THR[-1] <= MAX_DISTANCE


def _bias_delta(d, tab_ref, h):
    acc = jnp.full(d.shape, tab_ref[0, h] - tab_ref[N_BUCKETS - 1, h], F32)
    for k in range(1, N_BUCKETS):
        acc = acc + jnp.where(d >= _THR[k - 1], tab_ref[k, h] - tab_ref[k - 1, h], 0.0)
    return acc


def _dot_t(a, b, **kw):
    return lax.dot_general(a, b, (((1,), (1,)), ((), ())), preferred_element_type=F32, **kw)


def _gelu_tanh(x):
    return 0.5 * x * (1.0 + jnp.tanh(0.7978845608028654 * (x + 0.044715 * x * x * x)))


def _sigmoid(x):
    return 0.5 * jnp.tanh(0.5 * x) + 0.5


def _params(*sem, vmem_limit=VMEM_LIMIT):
    return pltpu.CompilerParams(dimension_semantics=sem, vmem_limit_bytes=vmem_limit)


def _smem():
    return pl.BlockSpec(memory_space=pltpu.SMEM)


def _proj_kernel(x_ref, g_ref, w_ref, *out_refs, segs, tm):
    x = x_ref[...]
    ms = jnp.mean(x * x, axis=-1, keepdims=True)
    hn = (x * lax.rsqrt(ms + EPS) * g_ref[...]).astype(BF16)
    row0 = pl.program_id(0) * tm
    for o_ref, (start, width, aug) in zip(out_refs, segs):
        y = jnp.dot(hn, w_ref[:, start:start + width], preferred_element_type=F32)
        if aug is not None:
            lane = lax.broadcasted_iota(jnp.int32, y.shape, 1)
            if aug[0] == "onehot":
                _, group, rows_per_block, lo_first = aug
                half = group // 2
                row = row0 + lax.broadcasted_iota(jnp.int32, y.shape, 0)
                which_half = (lane // half) % 2
                data_half = (lane // group) % 2 if lo_first else jnp.zeros_like(lane)
                blk = (row // rows_per_block) % half
                y = jnp.where((which_half != data_half) & (lane % half == blk), 1.0, y)
            else:
                _, group, col = aug
                y = jnp.where(lane % group == col, 1.0, y)
        o_ref[...] = y.astype(o_ref.dtype)


def _rms_proj(x, g, w, segs, dtypes, tm=512):
    t = x.shape[0]
    return pl.pallas_call(
        functools.partial(_proj_kernel, segs=segs, tm=tm),
        grid=(t // tm,),
        in_specs=[pl.BlockSpec((tm, D), lambda i: (i, 0)),
                  pl.BlockSpec((1, D), lambda i: (0, 0)),
                  pl.BlockSpec(w.shape, lambda i: (0, 0))],
        out_specs=[pl.BlockSpec((tm, wd), lambda i: (i, 0)) for (_, wd, _) in segs],
        out_shape=[jax.ShapeDtypeStruct((t, wd), dt) for (_, wd, _), dt in zip(segs, dtypes)],
        compiler_params=_params("arbitrary"),
        name="rms_proj",
    )(x, g.reshape(1, D), w)


def _segments(widths_augs):
    segs, off = [], 0
    for wd, aug in widths_augs:
        segs.append((off, wd, aug))
        off += wd
    return segs


def _topk_mask(r, lane, k):
    sel = jnp.zeros(r.shape, jnp.bool_)
    lanef = lane.astype(F32)
    for _ in range(k):
        mx = jnp.max(r, axis=1, keepdims=True)
        first = jnp.min(jnp.where(r == mx, lanef, 1e9), axis=1, keepdims=True)
        pick = lanef == first
        sel = sel | (pick & (mx > NEG / 2))
        r = jnp.where(pick, NEG, r)
    return sel


def _removed_topk(r0, k):
    r = r0
    for _ in range(k):
        r = jnp.where(r == jnp.max(r, axis=1, keepdims=True), NEG, r)
    return (r0 > NEG / 2) & (r < NEG / 2)


def _near_bias_tiles(f_ref, tab_ref, heads):
    d0 = (lax.broadcasted_iota(jnp.int32, (TILE, TILE), 0)
          - lax.broadcasted_iota(jnp.int32, (TILE, TILE), 1))
    for i, h in enumerate(heads):
        f_ref[i, 0] = jnp.where(d0 >= 0, LOG2E * _bias_delta(jnp.maximum(d0, 0), tab_ref, h), NEG)
        f_ref[i, 1] = LOG2E * _bias_delta(d0 + TILE, tab_ref, h)


def _slot_bias(j, qi, lo, f_diag, f_prev):
    other = jnp.where((j >= lo) & (j <= qi), 0.0, NEG)
    return jnp.where(j == qi, f_diag, jnp.where(j == qi - 1, f_prev, other))


def _flash_update(s, vt, m_ref, acc_ref, first):
    rm = jnp.max(s, axis=1, keepdims=True)
    if first:
        m_ref[...] = jnp.broadcast_to(rm, m_ref.shape)
        p = jnp.exp2(s - rm)
        acc_ref[...] = jnp.dot(p.astype(BF16), vt, preferred_element_type=F32)
        return
    m_old = m_ref[...]
    m_new = jnp.maximum(m_old, rm)
    alpha = jnp.exp2(m_old - m_new)
    p = jnp.exp2(s - jnp.concatenate([m_new] * (s.shape[1] // m_new.shape[1]), axis=1))
    pv = jnp.dot(p.astype(BF16), vt, preferred_element_type=F32)
    acc_ref[...] = jnp.concatenate([alpha] * (pv.shape[1] // alpha.shape[1]), axis=1) * acc_ref[...] + pv
    m_ref[...] = m_new


def _moba_route_kernel(q_ref, k_ref, qa_ref, km_ref):
    qi = pl.program_id(0)
    scale = DHA ** -0.5 * LOG2E
    lane = lax.broadcasted_iota(jnp.int32, (TILE, 128), 1)

    @pl.when(qi == 0)
    def _():
        km_ref[...] = jnp.zeros_like(km_ref)

    def augmented_query(q, head_lanes, in_aug, blk, sel):
        keep = sel | (in_aug & (blk == qi))
        return jnp.where(head_lanes, q * scale, jnp.where(keep, 0.0, NEG)).astype(BF16)

    fallbacks = []
    for hh in range(HA):
        pair = slice(hh // 2 * 128, (hh // 2 + 1) * 128)
        q = q_ref[:, pair]
        head_lanes = (lane < DHA) if hh % 2 == 0 else (lane >= DHA)
        qh = jnp.where(head_lanes, q, 0.0)
        route = _dot_t(qh, km_ref[:, pair], precision=HIGHEST)
        blk = lane - DHA if hh % 2 == 0 else lane
        in_aug = jnp.logical_not(head_lanes)
        cand = in_aug & (blk < qi)
        r = jnp.where(cand, route, NEG)
        sel = _removed_topk(r, MOBA_TOPK)
        qa_ref[:, hh * 128:(hh + 1) * 128] = augmented_query(q, head_lanes, in_aug, blk, sel)
        count = jnp.sum(jnp.where(sel, 1.0, 0.0), axis=1, keepdims=True)
        ncand = jnp.sum(jnp.where(cand, 1.0, 0.0), axis=1, keepdims=True)
        tied = jnp.max(jnp.where(count != jnp.minimum(ncand, float(MOBA_TOPK)), 1.0, 0.0))
        fallbacks.append((hh, tied, (q, head_lanes, in_aug, blk), r))

    mean = jnp.mean(k_ref[...], axis=0, keepdims=True)
    km_ref[pl.ds(qi, 1), :] = mean
    km_ref[pl.ds(qi + DHA, 1), :] = mean

    for hh, tied, args, r in fallbacks:
        @pl.when(tied > 0.0)
        def _(hh=hh, args=args, r=r):
            qa_ref[:, hh * 128:(hh + 1) * 128] = augmented_query(*args, _topk_mask(r, lane, MOBA_TOPK))


def _moba_route(q, k):
    t = q.shape[0]
    assert t // MOBA_BLOCK <= DHA and MOBA_BLOCK == TILE
    rows = lambda w: pl.BlockSpec((TILE, w), lambda qi: (qi, 0))
    return pl.pallas_call(
        _moba_route_kernel,
        grid=(t // TILE,),
        in_specs=[rows(HA * DHA), rows(HA * DHA)],
        out_specs=rows(HA * 128),
        out_shape=jax.ShapeDtypeStruct((t, HA * 128), BF16),
        scratch_shapes=[pltpu.VMEM((2 * DHA, HA * DHA), F32)],
        compiler_params=_params("arbitrary"),
        name="moba_route",
    )(q, k)


def _moba_kernel(tab_ref, qa_ref, k_ref, v_ref, o_ref, f_ref, m_ref, acc_ref, s_ref, *, nb, unroll):
    qi = pl.program_id(0)
    lane = lax.broadcasted_iota(jnp.int32, (TILE, 128), 1)

    @pl.when(qi == 0)
    def _():
        _near_bias_tiles(f_ref, tab_ref, list(range(HA)))

    def own_lanes(hh, rows):
        ln = lax.broadcasted_iota(jnp.int32, (rows, 128), 1)
        return (ln < DHA) if hh % 2 == 0 else (ln >= DHA)

    def keys(j, hh, n):
        pair = slice(hh // 2 * 128, (hh // 2 + 1) * 128)
        parts = []
        for u in range(n):
            off = pl.multiple_of((j + u) * TILE, TILE)
            hot = jnp.where(lane == (j + u) + (DHA if hh % 2 == 0 else 0), 1.0, 0.0).astype(BF16)
            parts.append(jnp.where(own_lanes(hh, TILE), k_ref[pl.ds(off, TILE), pair], hot))
        return jnp.concatenate(parts, axis=0)

    def values(j, hh, n):
        pair = slice(hh // 2 * 128, (hh // 2 + 1) * 128)
        off = pl.multiple_of(j * TILE, TILE)
        ln = lax.broadcasted_iota(jnp.int32, (n * TILE, 128), 1)
        ones_col = jnp.where(ln == (DHA if hh % 2 == 0 else 0), 1.0, 0.0).astype(BF16)
        return jnp.where(own_lanes(hh, n * TILE), v_ref[pl.ds(off, n * TILE), pair], ones_col)

    def scores(j, hh, n):
        return _dot_t(qa_ref[:, hh * 128:(hh + 1) * 128], keys(j, hh, n))

    far_tiles = unroll * (jnp.maximum(qi - 1, 0) // unroll)
    groups = far_tiles // unroll
    last = jnp.maximum(groups - 1, 0)

    def produce(buf, grp):
        for hh in range(HA):
            s_ref[buf, hh] = scores(grp * unroll, hh, unroll)

    def consume(buf, grp):
        for hh in range(HA):
            _flash_update(s_ref[buf, hh], values(grp * unroll, hh, unroll), m_ref.at[hh], acc_ref.at[hh], False)

    produce(0, 0)
    start = jnp.minimum(far_tiles, nb - (unroll + 1))
    for hh in range(HA):
        bias = jnp.concatenate([_slot_bias(start + u, qi, far_tiles, f_ref[hh, 0], f_ref[hh, 1])
                                for u in range(unroll + 1)], axis=1)
        _flash_update(scores(start, hh, unroll + 1) + bias, values(start, hh, unroll + 1),
                      m_ref.at[hh], acc_ref.at[hh], True)

    def pair(k, c):
        produce(1, 2 * k + 1)
        consume(0, 2 * k)
        produce(0, jnp.minimum(2 * k + 2, last))
        consume(1, 2 * k + 1)
        return c

    lax.fori_loop(0, groups // 2, pair, 0)

    @pl.when(groups % 2 == 1)
    def _():
        consume(0, groups - 1)

    for pp in range(HA // 2):
        even, odd = acc_ref[2 * pp], acc_ref[2 * pp + 1]
        o_ref[:, pp * 128:(pp + 1) * 128] = jnp.where(lane < DHA, even * (1.0 / even[:, DHA:DHA + 1]),
                                                      odd * (1.0 / odd[:, 0:1])).astype(o_ref.dtype)


def _moba(tab, qa, k, v):
    t = qa.shape[0]
    once = pl.Buffered(1)
    unroll = min(MOBA_UNROLL, t // TILE - 1)
    return pl.pallas_call(
        functools.partial(_moba_kernel, nb=t // TILE, unroll=unroll),
        grid=(t // TILE,),
        in_specs=[_smem(),
                  pl.BlockSpec((TILE, HA * 128), lambda qi: (qi, 0)),
                  pl.BlockSpec((t, HA * DHA), lambda qi: (0, 0), pipeline_mode=once),
                  pl.BlockSpec((t, HA * DHA), lambda qi: (0, 0), pipeline_mode=once)],
        out_specs=pl.BlockSpec((TILE, HA * DHA), lambda qi: (qi, 0)),
        out_shape=jax.ShapeDtypeStruct((t, HA * DHA), BF16),
        scratch_shapes=[pltpu.VMEM((HA, 2, TILE, TILE), F32),
                        pltpu.VMEM((HA, TILE, 128), F32),
                        pltpu.VMEM((HA, TILE, 128), F32),
                        pltpu.VMEM((2, HA, TILE, unroll * TILE), F32)],
        compiler_params=_params("arbitrary", vmem_limit=VMEM_LIMIT_RESIDENT),
        name="moba_attention",
    )(tab, qa, k, v)


def _rglru_kernel(xr_ref, gr_ref, cw_ref, cb_ref, wrg_ref, brg_ref, wig_ref, big_ref, lam_ref,
                  o_ref, xbuf, a_s, b_s, hcar, *, tt):
    @pl.when(pl.program_id(0) == 0)
    def _():
        xbuf[0:8, :] = jnp.zeros((8, LRU_W), F32)
        hcar[...] = jnp.zeros_like(hcar)

    xbuf[8:8 + tt, :] = xr_ref[...]
    xc = cb_ref[...]
    for j in range(CONV_W):
        xc = xc + xbuf[8 - (CONV_W - 1) + j:8 - (CONV_W - 1) + j + tt, :] * cw_ref[j:j + 1, :]
    xbuf[0:8, :] = xbuf[tt:tt + 8, :]

    xcb = xc.astype(BF16)
    r = _sigmoid(jnp.dot(xcb, wrg_ref[...], preferred_element_type=F32) + brg_ref[...])
    ig = _sigmoid(jnp.dot(xcb, wig_ref[...], preferred_element_type=F32) + big_ref[...])
    z = -lam_ref[...]
    softplus = jnp.maximum(z, 0.0) + jnp.log(1.0 + jnp.exp(-jnp.abs(z)))
    a = jnp.exp(-LRU_C * r * softplus)
    a_s[...] = a
    b_s[...] = jnp.sqrt(1.0 - a * a) * ig * xc

    row = lax.broadcasted_iota(jnp.int32, (8, LRU_W), 0)

    def body(gi, h):
        off = pl.multiple_of(gi * 8, 8)
        a8 = a_s[pl.ds(off, 8), :]
        b8 = b_s[pl.ds(off, 8), :]
        for s in (1, 2, 4):
            a_sh = jnp.where(row >= s, pltpu.roll(a8, s, 0), 1.0)
            b_sh = jnp.where(row >= s, pltpu.roll(b8, s, 0), 0.0)
            b8 = a8 * b_sh + b8
            a8 = a8 * a_sh
        h8 = b8 + a8 * h
        b_s[pl.ds(off, 8), :] = h8
        return h8[7:8, :]

    hcar[0:1, :] = lax.fori_loop(0, tt // 8, body, hcar[0:1, :])
    o_ref[...] = (b_s[...] * _gelu_tanh(gr_ref[...])).astype(o_ref.dtype)


def _rglru(xr, gr, conv_w, conv_b, wrg, b_rg, wig, b_ig, lam, tt=512):
    t = xr.shape[0]
    row = lambda a: a.reshape(1, LRU_W)
    full = lambda shape: pl.BlockSpec(shape, lambda i: (0, 0))
    return pl.pallas_call(
        functools.partial(_rglru_kernel, tt=tt),
        grid=(t // tt,),
        in_specs=[pl.BlockSpec((tt, LRU_W), lambda i: (i, 0)),
                  pl.BlockSpec((tt, LRU_W), lambda i: (i, 0)),
                  full((CONV_W, LRU_W)), full((1, LRU_W)),
                  full((LRU_W, LRU_W)), full((1, LRU_W)),
                  full((LRU_W, LRU_W)), full((1, LRU_W)), full((1, LRU_W))],
        out_specs=pl.BlockSpec((tt, LRU_W), lambda i: (i, 0)),
        out_shape=jax.ShapeDtypeStruct((t, LRU_W), BF16),
        scratch_shapes=[pltpu.VMEM((tt + 8, LRU_W), F32),
                        pltpu.VMEM((tt, LRU_W), F32),
                        pltpu.VMEM((tt, LRU_W), F32),
                        pltpu.VMEM((8, LRU_W), F32)],
        compiler_params=_params("arbitrary"),
        name="rglru",
    )(xr, gr, conv_w, row(conv_b), wrg, row(b_rg), wig, row(b_ig), row(lam))


def _post(y, g_ref, x_ref, o_ref):
    ms = jnp.mean(y * y, axis=-1, keepdims=True)
    o_ref[...] = x_ref[...] + y * lax.rsqrt(ms + EPS) * g_ref[...]


def _outproj_ab_kernel(a1_ref, a2_ref, w1_ref, w2_ref, g_ref, x_ref, o_ref):
    y = (jnp.dot(a1_ref[...], w1_ref[...], preferred_element_type=F32)
         + jnp.dot(a2_ref[...], w2_ref[...], preferred_element_type=F32))
    _post(y, g_ref, x_ref, o_ref)


def _outproj_ab(a1, a2, w1, w2, g, x, tm=512):
    t = x.shape[0]
    rows = lambda w: pl.BlockSpec((tm, w), lambda i: (i, 0))
    full = lambda shape: pl.BlockSpec(shape, lambda i: (0, 0))
    return pl.pallas_call(
        _outproj_ab_kernel,
        grid=(t // tm,),
        in_specs=[rows(a1.shape[1]), rows(a2.shape[1]), full(w1.shape), full(w2.shape),
                  full((1, D)), rows(D)],
        out_specs=rows(D),
        out_shape=jax.ShapeDtypeStruct((t, D), F32),
        compiler_params=_params("arbitrary"),
        name="outproj_ab",
    )(a1, a2, w1, w2, g.reshape(1, D), x)


def _outproj_c_kernel(oc_ref, os_ref, ow_ref, gl_ref, w_ref, g_ref, x_ref, o_ref):
    gates = jax.nn.sigmoid(gl_ref[...])
    parts = []
    for h in range(HC):
        sl = slice(h * DHC, (h + 1) * DHC)
        parts.append(gates[:, 3 * h:3 * h + 1] * oc_ref[:, sl]
                     + gates[:, 3 * h + 1:3 * h + 2] * os_ref[:, sl]
                     + gates[:, 3 * h + 2:3 * h + 3] * ow_ref[:, sl])
    o = jnp.concatenate(parts, axis=1).astype(BF16)
    _post(jnp.dot(o, w_ref[...], preferred_element_type=F32), g_ref, x_ref, o_ref)


def _outproj_c(oc, os_, ow, gl, w, g, x, tm=512):
    t = x.shape[0]
    rows = lambda w_: pl.BlockSpec((tm, w_), lambda i: (i, 0))
    full = lambda shape: pl.BlockSpec(shape, lambda i: (0, 0))
    return pl.pallas_call(
        _outproj_c_kernel,
        grid=(t // tm,),
        in_specs=[rows(D), rows(D), rows(D), rows(128), full(w.shape), full((1, D)), rows(D)],
        out_specs=rows(D),
        out_shape=jax.ShapeDtypeStruct((t, D), F32),
        compiler_params=_params("arbitrary"),
        name="outproj_c",
    )(oc, os_, ow, gl, w, g.reshape(1, D), x)


def _mlp_kernel(x_ref, gpre_ref, wup_ref, wdn_ref, gpost_ref, o_ref, h_s, acc_s):
    c = pl.program_id(1)

    @pl.when(c == 0)
    def _():
        x = x_ref[...]
        ms = jnp.mean(x * x, axis=-1, keepdims=True)
        h_s[...] = (x * lax.rsqrt(ms + EPS) * gpre_ref[...]).astype(BF16)
        acc_s[...] = jnp.zeros_like(acc_s)

    u = jnp.maximum(jnp.dot(h_s[...], wup_ref[...], preferred_element_type=F32), 0.0)
    acc_s[...] += jnp.dot((u * u).astype(BF16), wdn_ref[...], preferred_element_type=F32)

    @pl.when(c == pl.num_programs(1) - 1)
    def _():
        _post(acc_s[...], gpost_ref, x_ref, o_ref)


def _mlp(x, gpre, wup, wdn, gpost, tm=1024, tf=512):
    t = x.shape[0]
    return pl.pallas_call(
        _mlp_kernel,
        grid=(t // tm, D_FF // tf),
        in_specs=[pl.BlockSpec((tm, D), lambda i, c: (i, 0)),
                  pl.BlockSpec((1, D), lambda i, c: (0, 0)),
                  pl.BlockSpec((D, tf), lambda i, c: (0, c)),
                  pl.BlockSpec((tf, D), lambda i, c: (c, 0)),
                  pl.BlockSpec((1, D), lambda i, c: (0, 0))],
        out_specs=pl.BlockSpec((tm, D), lambda i, c: (i, 0)),
        out_shape=jax.ShapeDtypeStruct((t, D), F32),
        scratch_shapes=[pltpu.VMEM((tm, D), BF16), pltpu.VMEM((tm, D), F32)],
        compiler_params=_params("arbitrary", "arbitrary"),
        name="mlp",
    )(x, gpre.reshape(1, D), wup, wdn, gpost.reshape(1, D))


def _compress_kernel(*refs):
    h_refs, (pos_ref, w1_ref, w2_ref, o_ref) = refs[:GC], refs[GC:]
    half = CMP_STRIDE * DHC
    for g, h_ref in enumerate(h_refs):
        hm = h_ref[...]
        top = jnp.dot((hm + pos_ref[:, 0:half]).astype(BF16), w1_ref[0:half, :], preferred_element_type=F32)
        bot = jnp.dot((hm + pos_ref[:, half:2 * half]).astype(BF16), w1_ref[half:2 * half, :],
                      preferred_element_type=F32)
        hid = top + pltpu.roll(bot, bot.shape[0] - 1, 0)
        o_ref[g] = jnp.dot(_gelu_tanh(hid).astype(BF16), w2_ref[...],
                           preferred_element_type=F32).astype(o_ref.dtype)


def _compress(hmats, pos, w1, w2):
    n, wd = hmats[0].shape
    full = lambda shape: pl.BlockSpec(shape, lambda i: (0,) * len(shape))
    return pl.pallas_call(
        _compress_kernel,
        grid=(1,),
        in_specs=[full((n, wd))] * GC + [full((1, 2 * wd)), full(w1.shape), full(w2.shape)],
        out_specs=full((GC, n, DHC)),
        out_shape=jax.ShapeDtypeStruct((GC, n, DHC), BF16),
        compiler_params=_params("arbitrary"),
        name="nsa_compress",
    )(*hmats, pos.reshape(1, 2 * wd), w1, w2)


def _stack_heads(q):
    return jnp.concatenate([q[:, p * DHC:(p + 1) * DHC] for p in range(PC)], axis=0)


def _nsa_cmp_kernel(tab_ref, q_ref, kc_ref, vc_ref, ov_ref, *refs, tq, ns, nwb):
    kw_refs, vw_refs = refs[:nwb], refs[nwb:2 * nwb]
    oc_ref, sb_ref, ow_ref, e_ref, qb_ref, fw_ref, imp_ref = refs[2 * nwb:]
    qi = pl.program_id(0)
    ncp = kc_ref.shape[1]
    nsp = ov_ref.shape[1]
    scale = DHC ** -0.5 * LOG2E
    cpq = tq // CMP_STRIDE
    k = min(SEL_TOPK, ns)

    @pl.when(qi == 0)
    def _():
        u = lax.broadcasted_iota(jnp.int32, (2 * ncp, 128), 0)
        lane = lax.broadcasted_iota(jnp.int32, (2 * ncp, 128), 1)
        hit = (lane < 2 * CMP_BIAS_LANES) & (u + lane % CMP_BIAS_LANES == ncp - 1)
        after = (lane == 2 * CMP_BIAS_LANES) & (u >= ncp)
        e_ref[...] = (jnp.where(hit, 1.0, 0.0) + jnp.where(after, 1.0, 0.0)).astype(BF16)
        i = lax.broadcasted_iota(jnp.int32, (tq, 128), 0)
        lane = lax.broadcasted_iota(jnp.int32, (tq, 128), 1)
        d = i + CMP_STRIDE * (lane % CMP_BIAS_LANES) - (tq - CMP_STRIDE + CMP_LEN - 1)
        dw = (WINDOW + lax.broadcasted_iota(jnp.int32, (tq, nwb * tq), 0)
              - lax.broadcasted_iota(jnp.int32, (tq, nwb * tq), 1))
        for h in range(HC):
            b = jnp.where(d >= 0, LOG2E * _bias_delta(jnp.maximum(d, 0), tab_ref, h), NEG)
            hi = b.astype(BF16).astype(F32)
            qb = jnp.where(lane < CMP_BIAS_LANES, hi, jnp.where(lane < 2 * CMP_BIAS_LANES, b - hi, 0.0))
            qb_ref[h] = jnp.where(lane == 2 * CMP_BIAS_LANES, NEG, qb).astype(BF16)
            fw_ref[h] = jnp.where((dw >= 0) & (dw < WINDOW),
                                  LOG2E * _bias_delta(jnp.maximum(dw, 0), tab_ref, h), NEG)

    off = pl.multiple_of(ncp - cpq * (qi + 1), 16)
    jj = lax.broadcasted_iota(jnp.int32, (tq, nsp), 1)
    cur = (qi * tq + lax.broadcasted_iota(jnp.int32, (tq, nsp), 0)) // SEL_BLOCK
    forced = (jj == 0) | (jj == cur) | (jj == cur - 1)
    dead = (jj > cur) | (jj >= ns)
    q = q_ref[...]
    qs = [(q[:, h * DHC:(h + 1) * DHC] * scale).astype(BF16) for h in range(HC)]

    def compressed(width):
        onehots = e_ref[pl.ds(off, width), :]
        for g in range(GC):
            heads = range(g * PC, (g + 1) * PC)
            qa = jnp.concatenate([jnp.concatenate([qs[h] for h in heads], axis=0),
                                  jnp.concatenate([qb_ref[h] for h in heads], axis=0)], axis=1)
            s_all = _dot_t(qa, jnp.concatenate([kc_ref[g, 0:width, :], onehots], axis=1))
            psum = jnp.zeros((tq, width), F32)
            for p, h in enumerate(heads):
                s = s_all[p * tq:(p + 1) * tq]
                m = jnp.maximum(jnp.max(s, axis=1, keepdims=True), CMP_MAX_FLOOR)
                e = jnp.exp2(s - m)
                pc = e * (1.0 / jnp.maximum(jnp.sum(e, axis=1, keepdims=True), 1e-30))
                oc_ref[:, h * DHC:(h + 1) * DHC] = jnp.dot(pc.astype(BF16), vc_ref[g, 0:width, :],
                                                           preferred_element_type=F32).astype(oc_ref.dtype)
                psum = psum + pc
            hi = psum.astype(BF16)
            lo = (psum - hi.astype(F32)).astype(BF16)
            imp_ref[g] = (jnp.dot(hi, ov_ref[0:width, :], preferred_element_type=F32)
                          + jnp.dot(lo, ov_ref[0:width, :], preferred_element_type=F32))

    cw = min(CMP_WIDTH_CLASS, ncp)
    cls = (cpq * (qi + 1) - 1) // cw
    for c in range(ncp // cw):
        @pl.when(cls == c)
        def _(c=c):
            compressed((c + 1) * cw)

    fallbacks = []
    for g in range(GC):
        imp = imp_ref[g]
        rest = jnp.where(forced | dead, NEG, imp)
        sel_rest = _removed_topk(rest, k - 3)
        sel = sel_rest | (forced & jnp.logical_not(dead))
        sb_ref[g] = jnp.where(sel, 0.0, NEG).astype(BF16)
        count = jnp.sum(jnp.where(sel_rest, 1.0, 0.0), axis=1, keepdims=True)
        cand = jnp.sum(jnp.where(forced | dead, 0.0, 1.0), axis=1, keepdims=True)
        tied = jnp.max(jnp.where(count != jnp.minimum(cand, float(k - 3)), 1.0, 0.0))
        fallbacks.append((g, imp, tied))

    wk = jnp.concatenate([r[...] for r in kw_refs], axis=0)
    wv = jnp.concatenate([r[...] for r in vw_refs], axis=0)
    blk = lax.broadcasted_iota(jnp.int32, (nwb * tq, DHC), 0) // tq
    flags = jnp.where(blk + qi < nwb - 1, 1.0, 0.0).astype(BF16)
    off_lane = jnp.where(lax.broadcasted_iota(jnp.int32, (tq, DHC), 1) == 0, NEG, 0.0).astype(BF16)
    for h in range(HC):
        g = h // PC
        cols = slice(g * DHC, (g + 1) * DHC)
        s = _dot_t(jnp.concatenate([qs[h], off_lane], axis=1),
                   jnp.concatenate([wk[:, cols], flags], axis=1)) + fw_ref[h]
        e = jnp.exp2(s - jnp.max(s, axis=1, keepdims=True))
        o = jnp.dot(e.astype(BF16), wv[:, cols], preferred_element_type=F32)
        ow_ref[:, h * DHC:(h + 1) * DHC] = (o * (1.0 / jnp.sum(e, axis=1, keepdims=True))).astype(ow_ref.dtype)

    for g, imp, tied in fallbacks:
        @pl.when(tied > 0.0)
        def _(g=g, imp=imp):
            full = jnp.where(dead, NEG, jnp.where(forced, imp + SEL_FORCE, imp))
            sb_ref[g] = jnp.where(_topk_mask(full, jj, k), 0.0, NEG).astype(BF16)


def _nsa_cmp_win(tab, q, kc, vc, overlap, kw, vw, ns, tq=TILE):
    t = q.shape[0]
    ncp = kc.shape[1]
    nsp = overlap.shape[1]
    assert (tq - CMP_STRIDE + CMP_LEN - 1 + MAX_DISTANCE) // CMP_STRIDE < CMP_BIAS_LANES
    assert WINDOW % tq == 0 and min(SEL_TOPK, ns) > 3
    nwb = WINDOW // tq + 1
    full = lambda shape: pl.BlockSpec(shape, lambda qi: (0,) * len(shape))
    rows = pl.BlockSpec((tq, HC * DHC), lambda qi: (qi, 0))
    back = [pl.BlockSpec((tq, GC * DHC), functools.partial(lambda qi, b: (jnp.maximum(qi - b, 0), 0), b=b))
            for b in range(nwb - 1, -1, -1)]
    return pl.pallas_call(
        functools.partial(_nsa_cmp_kernel, tq=tq, ns=ns, nwb=nwb),
        grid=(t // tq,),
        in_specs=[_smem(), rows, full((GC, ncp, DHC)), full((GC, ncp, DHC)), full((ncp, nsp))] + back + back,
        out_specs=[rows, pl.BlockSpec((GC, tq, nsp), lambda qi: (0, qi, 0)), rows],
        out_shape=[jax.ShapeDtypeStruct((t, HC * DHC), BRANCH_DTYPE),
                   jax.ShapeDtypeStruct((GC, t, nsp), BF16),
                   jax.ShapeDtypeStruct((t, HC * DHC), BRANCH_DTYPE)],
        scratch_shapes=[pltpu.VMEM((2 * ncp, 128), BF16), pltpu.VMEM((HC, tq, 128), BF16),
                        pltpu.VMEM((HC, tq, nwb * tq), F32), pltpu.VMEM((GC, tq, nsp), F32)],
        compiler_params=_params("arbitrary"),
        name="nsa_compressed_window",
    )(tab, q, kc, vc, overlap, *([kw] * nwb), *([vw] * nwb))


def _nsa_sel_kernel(tab_ref, q_ref, sb_ref, ka_ref, va_ref, o_ref, f_ref, m_ref, acc_ref, s_ref, *, nhalf, nb,
                    unroll):
    qi = pl.program_id(0)
    scale = DHC ** -0.5 * LOG2E
    tiles_per_half = 128 * SEL_BLOCK // TILE

    @pl.when(qi == 0)
    def _():
        _near_bias_tiles(f_ref, tab_ref, list(range(HC)))

    hpc = PC // SEL_CHUNKS
    chunks = [(g, list(range(g * PC + c * hpc, g * PC + (c + 1) * hpc)))
              for g in range(GC) for c in range(SEL_CHUNKS)]
    q = q_ref[...]
    qas = []
    for g, heads in chunks:
        qc = jnp.concatenate([q[:, h * DHC:(h + 1) * DHC] for h in heads], axis=0)
        qc = (qc * scale).astype(BF16)
        sb = sb_ref[g]
        qas.append([jnp.concatenate([qc, jnp.concatenate([sb[:, hf * 128:(hf + 1) * 128]] * hpc, axis=0)], axis=1)
                    for hf in range(nhalf)])

    def qa_for(c, j):
        if nhalf == 1:
            return qas[c][0]
        return jnp.where(j >= tiles_per_half, qas[c][1], qas[c][0])

    def keys(g, off, n):
        return ka_ref[pl.ds(off, n), g * 2 * DHC:(g + 1) * 2 * DHC]

    def values(g, off, n):
        return va_ref[pl.ds(off, n), g * 2 * DHC:(g + 1) * 2 * DHC]

    far_tiles = unroll * (jnp.maximum(qi - 1, 0) // unroll)
    groups = far_tiles // unroll
    last = jnp.maximum(groups - 1, 0)

    def produce(buf, grp, only=None):
        off = pl.multiple_of(grp * (unroll * TILE), unroll * TILE)
        for c, (g, _) in enumerate(chunks):
            if only is None or c == only:
                s_ref[buf, c] = _dot_t(qa_for(c, grp * unroll), keys(g, off, unroll * TILE))

    def consume(buf, grp, only=None):
        off = pl.multiple_of(grp * (unroll * TILE), unroll * TILE)
        for c, (g, _) in enumerate(chunks):
            if only is None or c == only:
                _flash_update(s_ref[buf, c], values(g, off, unroll * TILE), m_ref.at[c], acc_ref.at[c], False)

    produce(0, 0)
    start = jnp.minimum(far_tiles, nb - (unroll + 1))
    for c, (g, heads) in enumerate(chunks):
        cols = []
        for u in range(unroll + 1):
            j = start + u
            s = _dot_t(qa_for(c, j), keys(g, pl.multiple_of(j * TILE, TILE), TILE))
            cols.append(jnp.concatenate(
                [s[i * TILE:(i + 1) * TILE] + _slot_bias(j, qi, far_tiles, f_ref[h, 0], f_ref[h, 1])
                 for i, h in enumerate(heads)], axis=0))
        vt = values(g, pl.multiple_of(start * TILE, TILE), (unroll + 1) * TILE)
        _flash_update(jnp.concatenate(cols, axis=1), vt, m_ref.at[c], acc_ref.at[c], True)

    def pair(k, carry):
        for c in range(len(chunks)):
            produce(1, 2 * k + 1, c)
            consume(0, 2 * k, c)
        for c in range(len(chunks)):
            produce(0, jnp.minimum(2 * k + 2, last), c)
            consume(1, 2 * k + 1, c)
        return carry

    lax.fori_loop(0, groups // 2, pair, 0)

    @pl.when(groups % 2 == 1)
    def _():
        consume(0, groups - 1)


    for c, (g, heads) in enumerate(chunks):
        acc = acc_ref[c]
        o = acc[:, :DHC] / acc[:, DHC:DHC + 1]
        for i, h in enumerate(heads):
            o_ref[:, h * DHC:(h + 1) * DHC] = o[i * TILE:(i + 1) * TILE].astype(o_ref.dtype)


def _nsa_sel(tab, q, selbias, kaug, vaug):
    t = q.shape[0]
    nsp = selbias.shape[2]
    nchunks = GC * SEL_CHUNKS
    rows = HC // nchunks * TILE
    once = pl.Buffered(1)
    unroll = min(SEL_UNROLL, t // TILE - 1)
    return pl.pallas_call(
        functools.partial(_nsa_sel_kernel, nhalf=nsp // 128, nb=t // TILE, unroll=unroll),
        grid=(t // TILE,),
        in_specs=[_smem(),
                  pl.BlockSpec((TILE, HC * DHC), lambda qi: (qi, 0)),
                  pl.BlockSpec((GC, TILE, nsp), lambda qi: (0, qi, 0)),
                  pl.BlockSpec((t, GC * 2 * DHC), lambda qi: (0, 0), pipeline_mode=once),
                  pl.BlockSpec((t, GC * 2 * DHC), lambda qi: (0, 0), pipeline_mode=once)],
        out_specs=pl.BlockSpec((TILE, HC * DHC), lambda qi: (qi, 0)),
        out_shape=jax.ShapeDtypeStruct((t, HC * DHC), BRANCH_DTYPE),
        scratch_shapes=[pltpu.VMEM((HC, 2, TILE, TILE), F32),
                        pltpu.VMEM((nchunks, rows, 128), F32),
                        pltpu.VMEM((nchunks, rows, 2 * DHC), F32),
                        pltpu.VMEM((2, nchunks, rows, unroll * TILE), F32)],
        compiler_params=_params("arbitrary", vmem_limit=VMEM_LIMIT_RESIDENT),
        name="nsa_selected",
    )(tab, q, selbias, kaug, vaug)


def _layer_ab(x, tab, w_in, conv_w, conv_b, w_rg, b_rg, w_ig, b_ig, lam, w_out, g_pre, g_post):
    da = HA * DHA
    segs = [(0, da, None), (da, da, None), (da, da, None), (2 * da, da, None),
            (3 * da, LRU_W, None), (3 * da + LRU_W, LRU_W, None)]
    q, k32, k, v, xr, gr = _rms_proj(x, g_pre, w_in.astype(BF16), segs, [F32, F32, BF16, BF16, F32, F32])
    o_a = _moba(tab, _moba_route(q, k32), k, v)

    eye = jnp.eye(w_rg.shape[0], dtype=F32)
    blockdiag = lambda wb: jnp.einsum('bij,bc->bicj', wb, eye).reshape(LRU_W, LRU_W).astype(BF16)
    o_b = _rglru(xr, gr, conv_w, conv_b, blockdiag(w_rg), b_rg, blockdiag(w_ig), b_ig, lam)
    wo = w_out.astype(BF16)
    return _outproj_ab(o_a, o_b, wo[:da], wo[da:], g_post, x)


def _layer_c(x, tab, w_in, pos_k, pos_v, k_w1, k_w2, v_w1, v_w2, w_out, g_pre, g_post):
    t = x.shape[0]
    dq = HC * DHC
    ckv = GC * DHC
    cols = lambda j: w_in[:, dq + j * ckv: dq + (j + 1) * ckv]
    widen = lambda wc: jnp.concatenate([wc.reshape(D, GC, DHC), jnp.zeros((D, GC, DHC), wc.dtype)],
                                       axis=-1).reshape(D, GC * 2 * DHC)
    wg = jnp.pad(w_in[:, dq + 6 * ckv:], ((0, 0), (0, 128 - 3 * HC)))
    w = jnp.concatenate([w_in[:, :dq], cols(0), cols(1), widen(cols(2)), widen(cols(3)), cols(4), cols(5), wg],
                        axis=1).astype(BF16)
    segs = _segments([(dq, None), (DHC, None), (DHC, None), (DHC, None), (DHC, None),
                      (GC * 2 * DHC, ("onehot", 2 * DHC, SEL_BLOCK, False)),
                      (GC * 2 * DHC, ("ones", 2 * DHC, DHC)),
                      (ckv, None), (ckv, None), (128, None)])
    q, kc0, kc1, vc0, vc1, ks_aug, vs_aug, kw, vw, gl = _rms_proj(
        x, g_pre, w, segs, [F32, F32, F32, F32, F32, BF16, BF16, BF16, BF16, F32])

    n = t // CMP_STRIDE
    hm = lambda a: a.reshape(n, CMP_STRIDE * DHC)
    kc = _compress([hm(kc0), hm(kc1)], pos_k, k_w1.astype(BF16), k_w2.astype(BF16))
    vc = _compress([hm(vc0), hm(vc1)], pos_v, v_w1.astype(BF16), v_w2.astype(BF16))

    ns = t // SEL_BLOCK
    nsp = -(-ns // 128) * 128
    cs = jnp.arange(n)[:, None] * CMP_STRIDE
    ss = jnp.arange(nsp)[None, :] * SEL_BLOCK
    overlap = ((cs < ss + SEL_BLOCK) & (cs + CMP_LEN > ss) & (jnp.arange(n)[:, None] < n - 1)
               & (jnp.arange(nsp)[None, :] < ns)).astype(BF16)
    o_c, selbias, o_w = _nsa_cmp_win(tab, q, kc, vc, overlap, kw, vw, ns)
    o_s = _nsa_sel(tab, q, selbias, ks_aug, vs_aug)
    return _outproj_c(o_c, o_s, o_w, gl, w_out.astype(BF16), g_post, x)


def kernel(x, rel_bias, ab_w_in, ab_conv_w, ab_conv_b, ab_w_rg, ab_b_rg, ab_w_ig, ab_b_ig, ab_lru_lambda, ab_w_out, c_w_in, c_cmp_pos_k, c_cmp_pos_v, c_cmp_k_w1, c_cmp_k_w2, c_cmp_v_w1, c_cmp_v_w2, c_w_out, norm_mix_pre, norm_mix_post, norm_mlp_pre, norm_mlp_post, w_up, w_down):
    batch = x.shape[0]
    depth = norm_mix_pre.shape[0]
    outs = []
    for b in range(batch):
        xb = x[b]
        for layer in range(depth):
            i = layer // 2
            if layer % 2 == 0:
                xb = _layer_ab(xb, rel_bias, ab_w_in[i], ab_conv_w[i], ab_conv_b[i], ab_w_rg[i], ab_b_rg[i],
                               ab_w_ig[i], ab_b_ig[i], ab_lru_lambda[i], ab_w_out[i],
                               norm_mix_pre[layer], norm_mix_post[layer])
            else:
                xb = _layer_c(xb, rel_bias, c_w_in[i], c_cmp_pos_k[i], c_cmp_pos_v[i], c_cmp_k_w1[i],
                              c_cmp_k_w2[i], c_cmp_v_w1[i], c_cmp_v_w2[i], c_w_out[i],
                              norm_mix_pre[layer], norm_mix_post[layer])
            xb = _mlp(xb, norm_mlp_pre[layer], w_up[layer].astype(BF16), w_down[layer].astype(BF16),
                      norm_mlp_post[layer])
        outs.append(xb)
    return jnp.stack(outs)
```

```python
import functools
import math

import numpy as np
import jax
import jax.numpy as jnp
from jax import lax
from jax.experimental import pallas as pl
from jax.experimental.pallas import tpu as pltpu

F32 = jnp.float32
BF16 = jnp.bfloat16
HIGHEST = lax.Precision.HIGHEST
LOG2E = 1.4426950408889634

D = 1024
EPS = 1e-6
NEG = -1e30
N_BUCKETS = 32
MAX_DISTANCE = 128

HA = 8
DHA = 64
MOBA_BLOCK = 256
MOBA_TOPK = 3
LRU_W = 512
LRU_C = 8.0
CONV_W = 4
HC = 8
DHC = 128
GC = 2
PC = 4
CMP_LEN = 32
CMP_STRIDE = 16
SEL_BLOCK = 64
SEL_TOPK = 16
WINDOW = 512
SEL_FORCE = 1e4
D_FF = 4096

TILE = 256
MOBA_UNROLL = 2
SEL_UNROLL = 2
SEL_CHUNKS = 2
CMP_BIAS_LANES = 32
CMP_MAX_FLOOR = -1e20
CMP_WIDTH_CLASS = 256
BRANCH_DTYPE = BF16
VMEM_LIMIT = 56 * 1024 * 1024
VMEM_LIMIT_RESIDENT = 60 * 1024 * 1024


def _bucket_np(d):
    n = np.maximum(d, 0)
    exact = N_BUCKETS // 2
    nf = np.maximum(n, 1).astype(np.float32)
    large = exact + (np.log(nf / np.float32(exact)) / np.float32(math.log(MAX_DISTANCE / exact))
                     * np.float32(N_BUCKETS - exact)).astype(np.int32)
    return np.where(n < exact, n, np.minimum(large, N_BUCKETS - 1))


_THR = [int(np.argmax(_bucket_np(np.arange(0, 4 * MAX_DISTANCE)) >= k)) for k in range(1, N_BUCKETS)]
assert _THR[-1] <= MAX_DISTANCE


def _bias_delta(d, tab_ref, h):
    acc = jnp.full(d.shape, tab_ref[0, h] - tab_ref[N_BUCKETS - 1, h], F32)
    for k in range(1, N_BUCKETS):
        acc = acc + jnp.where(d >= _THR[k - 1], tab_ref[k, h] - tab_ref[k - 1, h], 0.0)
    return acc


def _dot_t(a, b, **kw):
    return lax.dot_general(a, b, (((1,), (1,)), ((), ())), preferred_element_type=F32, **kw)


def _gelu_tanh(x):
    return 0.5 * x * (1.0 + jnp.tanh(0.7978845608028654 * (x + 0.044715 * x * x * x)))


def _sigmoid(x):
    return 0.5 * jnp.tanh(0.5 * x) + 0.5


def _params(*sem, vmem_limit=VMEM_LIMIT):
    return pltpu.CompilerParams(dimension_semantics=sem, vmem_limit_bytes=vmem_limit)


def _smem():
    return pl.BlockSpec(memory_space=pltpu.SMEM)


def _proj_kernel(x_ref, g_ref, w_ref, *out_refs, segs):
    x = x_ref[...]
    ms = jnp.mean(x * x, axis=-1, keepdims=True)
    hn = (x * lax.rsqrt(ms + EPS) * g_ref[...]).astype(BF16)
    for o_ref, (start, width) in zip(out_refs, segs):
        o_ref[...] = jnp.dot(hn, w_ref[:, start:start + width], preferred_element_type=F32).astype(o_ref.dtype)


def _rms_proj(x, g, w, segs, dtypes, tm=512):
    t = x.shape[0]
    return pl.pallas_call(
        functools.partial(_proj_kernel, segs=segs),
        grid=(t // tm,),
        in_specs=[pl.BlockSpec((tm, D), lambda i: (i, 0)),
                  pl.BlockSpec((1, D), lambda i: (0, 0)),
                  pl.BlockSpec(w.shape, lambda i: (0, 0))],
        out_specs=[pl.BlockSpec((tm, wd), lambda i: (i, 0)) for (_, wd) in segs],
        out_shape=[jax.ShapeDtypeStruct((t, wd), dt) for (_, wd), dt in zip(segs, dtypes)],
        compiler_params=_params("arbitrary"),
        name="rms_proj",
    )(x, g.reshape(1, D), w)


def _segments(widths):
    segs, off = [], 0
    for wd in widths:
        segs.append((off, wd))
        off += wd
    return segs


def _topk_mask(r, lane, k):
    sel = jnp.zeros(r.shape, jnp.bool_)
    lanef = lane.astype(F32)
    for _ in range(k):
        mx = jnp.max(r, axis=1, keepdims=True)
        first = jnp.min(jnp.where(r == mx, lanef, 1e9), axis=1, keepdims=True)
        pick = lanef == first
        sel = sel | (pick & (mx > NEG / 2))
        r = jnp.where(pick, NEG, r)
    return sel


def _removed_topk(r0, k):
    r = r0
    for _ in range(k):
        r = jnp.where(r == jnp.max(r, axis=1, keepdims=True), NEG, r)
    return (r0 > NEG / 2) & (r < NEG / 2)


def _near_bias_tiles(f_ref, tab_ref, heads):
    d0 = (lax.broadcasted_iota(jnp.int32, (TILE, TILE), 0)
          - lax.broadcasted_iota(jnp.int32, (TILE, TILE), 1))
    for i, h in enumerate(heads):
        f_ref[i, 0] = jnp.where(d0 >= 0, LOG2E * _bias_delta(jnp.maximum(d0, 0), tab_ref, h), NEG)
        f_ref[i, 1] = LOG2E * _bias_delta(d0 + TILE, tab_ref, h)


def _slot_bias(j, qi, lo, f_diag, f_prev):
    other = jnp.where((j >= lo) & (j <= qi), 0.0, NEG)
    return jnp.where(j == qi, f_diag, jnp.where(j == qi - 1, f_prev, other))


def _flash_update(s, vt, m_ref, acc_ref, first):
    rm = jnp.max(s, axis=1, keepdims=True)
    if first:
        m_ref[...] = jnp.broadcast_to(rm, m_ref.shape)
        p = jnp.exp2(s - rm)
        acc_ref[...] = jnp.dot(p.astype(BF16), vt, preferred_element_type=F32)
        return
    m_old = m_ref[...]
    m_new = jnp.maximum(m_old, rm)
    alpha = jnp.exp2(m_old - m_new)
    p = jnp.exp2(s - jnp.concatenate([m_new] * (s.shape[1] // m_new.shape[1]), axis=1))
    pv = jnp.dot(p.astype(BF16), vt, preferred_element_type=F32)
    acc_ref[...] = jnp.concatenate([alpha] * (pv.shape[1] // alpha.shape[1]), axis=1) * acc_ref[...] + pv
    m_ref[...] = m_new


def _moba_route_kernel(q_ref, k_ref, qa_ref, km_ref):
    qi = pl.program_id(0)
    scale = DHA ** -0.5 * LOG2E
    lane = lax.broadcasted_iota(jnp.int32, (TILE, 128), 1)

    @pl.when(qi == 0)
    def _():
        km_ref[...] = jnp.zeros_like(km_ref)

    def augmented_query(q, head_lanes, in_aug, blk, sel):
        keep = sel | (in_aug & (blk == qi))
        return jnp.where(head_lanes, q * scale, jnp.where(keep, 0.0, NEG)).astype(BF16)

    fallbacks = []
    for hh in range(HA):
        pair = slice(hh // 2 * 128, (hh // 2 + 1) * 128)
        q = q_ref[:, pair]
        head_lanes = (lane < DHA) if hh % 2 == 0 else (lane >= DHA)
        qh = jnp.where(head_lanes, q, 0.0)
        route = _dot_t(qh, km_ref[:, pair], precision=HIGHEST)
        blk = lane - DHA if hh % 2 == 0 else lane
        in_aug = jnp.logical_not(head_lanes)
        cand = in_aug & (blk < qi)
        r = jnp.where(cand, route, NEG)
        sel = _removed_topk(r, MOBA_TOPK)
        qa_ref[:, hh * 128:(hh + 1) * 128] = augmented_query(q, head_lanes, in_aug, blk, sel)
        count = jnp.sum(jnp.where(sel, 1.0, 0.0), axis=1, keepdims=True)
        ncand = jnp.sum(jnp.where(cand, 1.0, 0.0), axis=1, keepdims=True)
        tied = jnp.max(jnp.where(count != jnp.minimum(ncand, float(MOBA_TOPK)), 1.0, 0.0))
        fallbacks.append((hh, tied, (q, head_lanes, in_aug, blk), r))

    mean = jnp.mean(k_ref[...], axis=0, keepdims=True)
    km_ref[pl.ds(qi, 1), :] = mean
    km_ref[pl.ds(qi + DHA, 1), :] = mean

    for hh, tied, args, r in fallbacks:
        @pl.when(tied > 0.0)
        def _(hh=hh, args=args, r=r):
            qa_ref[:, hh * 128:(hh + 1) * 128] = augmented_query(*args, _topk_mask(r, lane, MOBA_TOPK))


def _moba_route(q, k):
    t = q.shape[0]
    assert t // MOBA_BLOCK <= DHA and MOBA_BLOCK == TILE
    rows = lambda w: pl.BlockSpec((TILE, w), lambda qi: (qi, 0))
    return pl.pallas_call(
        _moba_route_kernel,
        grid=(t // TILE,),
        in_specs=[rows(HA * DHA), rows(HA * DHA)],
        out_specs=rows(HA * 128),
        out_shape=jax.ShapeDtypeStruct((t, HA * 128), BF16),
        scratch_shapes=[pltpu.VMEM((2 * DHA, HA * DHA), F32)],
        compiler_params=_params("arbitrary"),
        name="moba_route",
    )(q, k)


def _moba_kernel(tab_ref, qa_ref, k_ref, v_ref, o_ref, f_ref, m_ref, acc_ref, s_ref, *, nb, unroll):
    qi = pl.program_id(0)
    lane = lax.broadcasted_iota(jnp.int32, (TILE, 128), 1)

    @pl.when(qi == 0)
    def _():
        _near_bias_tiles(f_ref, tab_ref, list(range(HA)))

    def own_lanes(hh, rows):
        ln = lax.broadcasted_iota(jnp.int32, (rows, 128), 1)
        return (ln < DHA) if hh % 2 == 0 else (ln >= DHA)

    def keys(j, hh, n):
        pair = slice(hh // 2 * 128, (hh // 2 + 1) * 128)
        parts = []
        for u in range(n):
            off = pl.multiple_of((j + u) * TILE, TILE)
            hot = jnp.where(lane == (j + u) + (DHA if hh % 2 == 0 else 0), 1.0, 0.0).astype(BF16)
            parts.append(jnp.where(own_lanes(hh, TILE), k_ref[pl.ds(off, TILE), pair], hot))
        return jnp.concatenate(parts, axis=0)

    def values(j, hh, n):
        pair = slice(hh // 2 * 128, (hh // 2 + 1) * 128)
        off = pl.multiple_of(j * TILE, TILE)
        ln = lax.broadcasted_iota(jnp.int32, (n * TILE, 128), 1)
        ones_col = jnp.where(ln == (DHA if hh % 2 == 0 else 0), 1.0, 0.0).astype(BF16)
        return jnp.where(own_lanes(hh, n * TILE), v_ref[pl.ds(off, n * TILE), pair], ones_col)

    def scores(j, hh, n):
        return _dot_t(qa_ref[:, hh * 128:(hh + 1) * 128], keys(j, hh, n))

    far_tiles = unroll * (jnp.maximum(qi - 1, 0) // unroll)
    groups = far_tiles // unroll
    last = jnp.maximum(groups - 1, 0)

    def produce(buf, grp):
        for hh in range(HA):
            s_ref[buf, hh] = scores(grp * unroll, hh, unroll)

    def consume(buf, grp):
        for hh in range(HA):
            _flash_update(s_ref[buf, hh], values(grp * unroll, hh, unroll), m_ref.at[hh], acc_ref.at[hh], False)

    produce(0, 0)
    start = jnp.minimum(far_tiles, nb - (unroll + 1))
    for hh in range(HA):
        bias = jnp.concatenate([_slot_bias(start + u, qi, far_tiles, f_ref[hh, 0], f_ref[hh, 1])
                                for u in range(unroll + 1)], axis=1)
        _flash_update(scores(start, hh, unroll + 1) + bias, values(start, hh, unroll + 1),
                      m_ref.at[hh], acc_ref.at[hh], True)

    def pair(k, c):
        produce(1, 2 * k + 1)
        consume(0, 2 * k)
        produce(0, jnp.minimum(2 * k + 2, last))
        consume(1, 2 * k + 1)
        return c

    lax.fori_loop(0, groups // 2, pair, 0)

    @pl.when(groups % 2 == 1)
    def _():
        consume(0, groups - 1)

    for pp in range(HA // 2):
        even, odd = acc_ref[2 * pp], acc_ref[2 * pp + 1]
        o_ref[:, pp * 128:(pp + 1) * 128] = jnp.where(lane < DHA, even * (1.0 / even[:, DHA:DHA + 1]),
                                                      odd * (1.0 / odd[:, 0:1])).astype(o_ref.dtype)


def _moba(tab, qa, k, v):
    t = qa.shape[0]
    once = pl.Buffered(1)
    unroll = min(MOBA_UNROLL, t // TILE - 1)
    return pl.pallas_call(
        functools.partial(_moba_kernel, nb=t // TILE, unroll=unroll),
        grid=(t // TILE,),
        in_specs=[_smem(),
                  pl.BlockSpec((TILE, HA * 128), lambda qi: (qi, 0)),
                  pl.BlockSpec((t, HA * DHA), lambda qi: (0, 0), pipeline_mode=once),
                  pl.BlockSpec((t, HA * DHA), lambda qi: (0, 0), pipeline_mode=once)],
        out_specs=pl.BlockSpec((TILE, HA * DHA), lambda qi: (qi, 0)),
        out_shape=jax.ShapeDtypeStruct((t, HA * DHA), BF16),
        scratch_shapes=[pltpu.VMEM((HA, 2, TILE, TILE), F32),
                        pltpu.VMEM((HA, TILE, 128), F32),
                        pltpu.VMEM((HA, TILE, 128), F32),
                        pltpu.VMEM((2, HA, TILE, unroll * TILE), F32)],
        compiler_params=_params("arbitrary", vmem_limit=VMEM_LIMIT_RESIDENT),
        name="moba_attention",
    )(tab, qa, k, v)


def _rglru_kernel(xr_ref, gr_ref, cw_ref, cb_ref, wrg_ref, brg_ref, wig_ref, big_ref, lam_ref,
                  o_ref, xbuf, a_s, b_s, hcar, *, tt):
    @pl.when(pl.program_id(0) == 0)
    def _():
        xbuf[0:8, :] = jnp.zeros((8, LRU_W), F32)
        hcar[...] = jnp.zeros_like(hcar)

    xbuf[8:8 + tt, :] = xr_ref[...]
    xc = cb_ref[...]
    for j in range(CONV_W):
        xc = xc + xbuf[8 - (CONV_W - 1) + j:8 - (CONV_W - 1) + j + tt, :] * cw_ref[j:j + 1, :]
    xbuf[0:8, :] = xbuf[tt:tt + 8, :]

    xcb = xc.astype(BF16)
    r = _sigmoid(jnp.dot(xcb, wrg_ref[...], preferred_element_type=F32) + brg_ref[...])
    ig = _sigmoid(jnp.dot(xcb, wig_ref[...], preferred_element_type=F32) + big_ref[...])
    z = -lam_ref[...]
    softplus = jnp.maximum(z, 0.0) + jnp.log(1.0 + jnp.exp(-jnp.abs(z)))
    a = jnp.exp(-LRU_C * r * softplus)
    a_s[...] = a
    b_s[...] = jnp.sqrt(1.0 - a * a) * ig * xc

    row = lax.broadcasted_iota(jnp.int32, (8, LRU_W), 0)

    def body(gi, h):
        off = pl.multiple_of(gi * 8, 8)
        a8 = a_s[pl.ds(off, 8), :]
        b8 = b_s[pl.ds(off, 8), :]
        for s in (1, 2, 4):
            a_sh = jnp.where(row >= s, pltpu.roll(a8, s, 0), 1.0)
            b_sh = jnp.where(row >= s, pltpu.roll(b8, s, 0), 0.0)
            b8 = a8 * b_sh + b8
            a8 = a8 * a_sh
        h8 = b8 + a8 * h
        b_s[pl.ds(off, 8), :] = h8
        return h8[7:8, :]

    hcar[0:1, :] = lax.fori_loop(0, tt // 8, body, hcar[0:1, :])
    o_ref[...] = (b_s[...] * _gelu_tanh(gr_ref[...])).astype(o_ref.dtype)


def _rglru(xr, gr, conv_w, conv_b, wrg, b_rg, wig, b_ig, lam, tt=512):
    t = xr.shape[0]
    row = lambda a: a.reshape(1, LRU_W)
    full = lambda shape: pl.BlockSpec(shape, lambda i: (0, 0))
    return pl.pallas_call(
        functools.partial(_rglru_kernel, tt=tt),
        grid=(t // tt,),
        in_specs=[pl.BlockSpec((tt, LRU_W), lambda i: (i, 0)),
                  pl.BlockSpec((tt, LRU_W), lambda i: (i, 0)),
                  full((CONV_W, LRU_W)), full((1, LRU_W)),
                  full((LRU_W, LRU_W)), full((1, LRU_W)),
                  full((LRU_W, LRU_W)), full((1, LRU_W)), full((1, LRU_W))],
        out_specs=pl.BlockSpec((tt, LRU_W), lambda i: (i, 0)),
        out_shape=jax.ShapeDtypeStruct((t, LRU_W), BF16),
        scratch_shapes=[pltpu.VMEM((tt + 8, LRU_W), F32),
                        pltpu.VMEM((tt, LRU_W), F32),
                        pltpu.VMEM((tt, LRU_W), F32),
                        pltpu.VMEM((8, LRU_W), F32)],
        compiler_params=_params("arbitrary"),
        name="rglru",
    )(xr, gr, conv_w, row(conv_b), wrg, row(b_rg), wig, row(b_ig), row(lam))


def _post(y, g_ref, x_ref, o_ref):
    ms = jnp.mean(y * y, axis=-1, keepdims=True)
    o_ref[...] = x_ref[...] + y * lax.rsqrt(ms + EPS) * g_ref[...]


def _outproj_ab_kernel(a1_ref, a2_ref, w1_ref, w2_ref, g_ref, x_ref, o_ref):
    y = (jnp.dot(a1_ref[...], w1_ref[...], preferred_element_type=F32)
         + jnp.dot(a2_ref[...], w2_ref[...], preferred_element_type=F32))
    _post(y, g_ref, x_ref, o_ref)


def _outproj_ab(a1, a2, w1, w2, g, x, tm=512):
    t = x.shape[0]
    rows = lambda w: pl.BlockSpec((tm, w), lambda i: (i, 0))
    full = lambda shape: pl.BlockSpec(shape, lambda i: (0, 0))
    return pl.pallas_call(
        _outproj_ab_kernel,
        grid=(t // tm,),
        in_specs=[rows(a1.shape[1]), rows(a2.shape[1]), full(w1.shape), full(w2.shape),
                  full((1, D)), rows(D)],
        out_specs=rows(D),
        out_shape=jax.ShapeDtypeStruct((t, D), F32),
        compiler_params=_params("arbitrary"),
        name="outproj_ab",
    )(a1, a2, w1, w2, g.reshape(1, D), x)


def _outproj_c_kernel(oc_ref, os_ref, ow_ref, gl_ref, w_ref, g_ref, x_ref, o_ref):
    gates = jax.nn.sigmoid(gl_ref[...])
    parts = []
    for h in range(HC):
        sl = slice(h * DHC, (h + 1) * DHC)
        parts.append(gates[:, 3 * h:3 * h + 1] * oc_ref[:, sl]
                     + gates[:, 3 * h + 1:3 * h + 2] * os_ref[:, sl]
                     + gates[:, 3 * h + 2:3 * h + 3] * ow_ref[:, sl])
    o = jnp.concatenate(parts, axis=1).astype(BF16)
    _post(jnp.dot(o, w_ref[...], preferred_element_type=F32), g_ref, x_ref, o_ref)


def _outproj_c(oc, os_, ow, gl, w, g, x, tm=512):
    t = x.shape[0]
    rows = lambda w_: pl.BlockSpec((tm, w_), lambda i: (i, 0))
    full = lambda shape: pl.BlockSpec(shape, lambda i: (0, 0))
    return pl.pallas_call(
        _outproj_c_kernel,
        grid=(t // tm,),
        in_specs=[rows(D), rows(D), rows(D), rows(128), full(w.shape), full((1, D)), rows(D)],
        out_specs=rows(D),
        out_shape=jax.ShapeDtypeStruct((t, D), F32),
        compiler_params=_params("arbitrary"),
        name="outproj_c",
    )(oc, os_, ow, gl, w, g.reshape(1, D), x)


def _mlp_kernel(x_ref, gpre_ref, wup_ref, wdn_ref, gpost_ref, o_ref, h_s, acc_s):
    c = pl.program_id(1)

    @pl.when(c == 0)
    def _():
        x = x_ref[...]
        ms = jnp.mean(x * x, axis=-1, keepdims=True)
        h_s[...] = (x * lax.rsqrt(ms + EPS) * gpre_ref[...]).astype(BF16)
        acc_s[...] = jnp.zeros_like(acc_s)

    u = jnp.maximum(jnp.dot(h_s[...], wup_ref[...].astype(BF16), preferred_element_type=F32), 0.0)
    acc_s[...] += jnp.dot((u * u).astype(BF16), wdn_ref[...].astype(BF16), preferred_element_type=F32)

    @pl.when(c == pl.num_programs(1) - 1)
    def _():
        _post(acc_s[...], gpost_ref, x_ref, o_ref)


def _mlp(x, gpre, wup, wdn, gpost, tm=1024, tf=512):
    t = x.shape[0]
    return pl.pallas_call(
        _mlp_kernel,
        grid=(t // tm, D_FF // tf),
        in_specs=[pl.BlockSpec((tm, D), lambda i, c: (i, 0)),
                  pl.BlockSpec((1, D), lambda i, c: (0, 0)),
                  pl.BlockSpec((D, tf), lambda i, c: (0, c)),
                  pl.BlockSpec((tf, D), lambda i, c: (c, 0)),
                  pl.BlockSpec((1, D), lambda i, c: (0, 0))],
        out_specs=pl.BlockSpec((tm, D), lambda i, c: (i, 0)),
        out_shape=jax.ShapeDtypeStruct((t, D), F32),
        scratch_shapes=[pltpu.VMEM((tm, D), BF16), pltpu.VMEM((tm, D), F32)],
        compiler_params=_params("arbitrary", "arbitrary"),
        name="mlp",
    )(x, gpre.reshape(1, D), wup, wdn, gpost.reshape(1, D))


def _compress_kernel(*refs):
    x_refs, (pos_ref, w1_ref, w2_ref, o_ref) = refs[:GC], refs[GC:]
    n = o_ref.shape[1]
    for g, x_ref in enumerate(x_refs):
        top = jnp.zeros((n, w1_ref.shape[1]), F32)
        bot = jnp.zeros((n, w1_ref.shape[1]), F32)
        for l in range(CMP_STRIDE):
            xl = x_ref[pl.ds(l, n, stride=CMP_STRIDE), :]
            lo, hi = l * DHC, (CMP_STRIDE + l) * DHC
            top = top + jnp.dot((xl + pos_ref[l:l + 1, :]).astype(BF16), w1_ref[lo:lo + DHC, :],
                                preferred_element_type=F32)
            bot = bot + jnp.dot((xl + pos_ref[CMP_STRIDE + l:CMP_STRIDE + l + 1, :]).astype(BF16),
                                w1_ref[hi:hi + DHC, :], preferred_element_type=F32)
        hid = top + pltpu.roll(bot, n - 1, 0)
        o_ref[g] = jnp.dot(_gelu_tanh(hid).astype(BF16), w2_ref[...],
                           preferred_element_type=F32).astype(o_ref.dtype)


def _compress(xs, pos, w1, w2):
    t = xs[0].shape[0]
    n = t // CMP_STRIDE
    full = lambda shape: pl.BlockSpec(shape, lambda i: (0,) * len(shape))
    return pl.pallas_call(
        _compress_kernel,
        grid=(1,),
        in_specs=[full((t, DHC))] * GC + [full(pos.shape), full(w1.shape), full(w2.shape)],
        out_specs=full((GC, n, DHC)),
        out_shape=jax.ShapeDtypeStruct((GC, n, DHC), BF16),
        compiler_params=_params("arbitrary"),
        name="nsa_compress",
    )(*xs, pos, w1, w2)


def _stack_heads(q):
    return jnp.concatenate([q[:, p * DHC:(p + 1) * DHC] for p in range(PC)], axis=0)


def _nsa_cmp_kernel(tab_ref, q_ref, kc_ref, vc_ref, ov_ref, *refs, tq, ns, nwb):
    kw_refs, vw_refs = refs[:nwb], refs[nwb:2 * nwb]
    oc_ref, sb_ref, ow_ref, e_ref, qb_ref, fw_ref, imp_ref = refs[2 * nwb:]
    qi = pl.program_id(0)
    ncp = kc_ref.shape[1]
    nsp = ov_ref.shape[1]
    scale = DHC ** -0.5 * LOG2E
    cpq = tq // CMP_STRIDE
    k = min(SEL_TOPK, ns)

    @pl.when(qi == 0)
    def _():
        u = lax.broadcasted_iota(jnp.int32, (2 * ncp, 128), 0)
        lane = lax.broadcasted_iota(jnp.int32, (2 * ncp, 128), 1)
        hit = (lane < 2 * CMP_BIAS_LANES) & (u + lane % CMP_BIAS_LANES == ncp - 1)
        after = (lane == 2 * CMP_BIAS_LANES) & (u >= ncp)
        e_ref[...] = (jnp.where(hit, 1.0, 0.0) + jnp.where(after, 1.0, 0.0)).astype(BF16)
        i = lax.broadcasted_iota(jnp.int32, (tq, 128), 0)
        lane = lax.broadcasted_iota(jnp.int32, (tq, 128), 1)
        d = i + CMP_STRIDE * (lane % CMP_BIAS_LANES) - (tq - CMP_STRIDE + CMP_LEN - 1)
        dw = (WINDOW + lax.broadcasted_iota(jnp.int32, (tq, nwb * tq), 0)
              - lax.broadcasted_iota(jnp.int32, (tq, nwb * tq), 1))
        for h in range(HC):
            b = jnp.where(d >= 0, LOG2E * _bias_delta(jnp.maximum(d, 0), tab_ref, h), NEG)
            hi = b.astype(BF16).astype(F32)
            qb = jnp.where(lane < CMP_BIAS_LANES, hi, jnp.where(lane < 2 * CMP_BIAS_LANES, b - hi, 0.0))
            qb_ref[h] = jnp.where(lane == 2 * CMP_BIAS_LANES, NEG, qb).astype(BF16)
            fw_ref[h] = jnp.where((dw >= 0) & (dw < WINDOW),
                                  LOG2E * _bias_delta(jnp.maximum(dw, 0), tab_ref, h), NEG)

    off = pl.multiple_of(ncp - cpq * (qi + 1), 16)
    jj = lax.broadcasted_iota(jnp.int32, (tq, nsp), 1)
    cur = (qi * tq + lax.broadcasted_iota(jnp.int32, (tq, nsp), 0)) // SEL_BLOCK
    forced = (jj == 0) | (jj == cur) | (jj == cur - 1)
    dead = (jj > cur) | (jj >= ns)
    q = q_ref[...]
    qs = [(q[:, h * DHC:(h + 1) * DHC] * scale).astype(BF16) for h in range(HC)]

    def compressed(width):
        onehots = e_ref[pl.ds(off, width), :]
        for g in range(GC):
            heads = range(g * PC, (g + 1) * PC)
            qa = jnp.concatenate([jnp.concatenate([qs[h] for h in heads], axis=0),
                                  jnp.concatenate([qb_ref[h] for h in heads], axis=0)], axis=1)
            s_all = _dot_t(qa, jnp.concatenate([kc_ref[g, 0:width, :], onehots], axis=1))
            psum = jnp.zeros((tq, width), F32)
            for p, h in enumerate(heads):
                s = s_all[p * tq:(p + 1) * tq]
                m = jnp.maximum(jnp.max(s, axis=1, keepdims=True), CMP_MAX_FLOOR)
                e = jnp.exp2(s - m)
                pc = e * (1.0 / jnp.maximum(jnp.sum(e, axis=1, keepdims=True), 1e-30))
                oc_ref[:, h * DHC:(h + 1) * DHC] = jnp.dot(pc.astype(BF16), vc_ref[g, 0:width, :],
                                                           preferred_element_type=F32).astype(oc_ref.dtype)
                psum = psum + pc
            hi = psum.astype(BF16)
            lo = (psum - hi.astype(F32)).astype(BF16)
            imp_ref[g] = (jnp.dot(hi, ov_ref[0:width, :], preferred_element_type=F32)
                          + jnp.dot(lo, ov_ref[0:width, :], preferred_element_type=F32))

    cw = min(CMP_WIDTH_CLASS, ncp)
    cls = (cpq * (qi + 1) - 1) // cw
    for c in range(ncp // cw):
        @pl.when(cls == c)
        def _(c=c):
            compressed((c + 1) * cw)

    fallbacks = []
    for g in range(GC):
        imp = imp_ref[g]
        rest = jnp.where(forced | dead, NEG, imp)
        sel_rest = _removed_topk(rest, k - 3)
        sel = sel_rest | (forced & jnp.logical_not(dead))
        sb_ref[g] = jnp.where(sel, 0.0, NEG).astype(BF16)
        count = jnp.sum(jnp.where(sel_rest, 1.0, 0.0), axis=1, keepdims=True)
        cand = jnp.sum(jnp.where(forced | dead, 0.0, 1.0), axis=1, keepdims=True)
        tied = jnp.max(jnp.where(count != jnp.minimum(cand, float(k - 3)), 1.0, 0.0))
        fallbacks.append((g, imp, tied))

    wk = jnp.concatenate([r[...] for r in kw_refs], axis=0)
    wv = jnp.concatenate([r[...] for r in vw_refs], axis=0)
    blk = lax.broadcasted_iota(jnp.int32, (nwb * tq, DHC), 0) // tq
    flags = jnp.where(blk + qi < nwb - 1, 1.0, 0.0).astype(BF16)
    off_lane = jnp.where(lax.broadcasted_iota(jnp.int32, (tq, DHC), 1) == 0, NEG, 0.0).astype(BF16)
    for h in range(HC):
        g = h // PC
        cols = slice(g * DHC, (g + 1) * DHC)
        s = _dot_t(jnp.concatenate([qs[h], off_lane], axis=1),
                   jnp.concatenate([wk[:, cols], flags], axis=1)) + fw_ref[h]
        e = jnp.exp2(s - jnp.max(s, axis=1, keepdims=True))
        o = jnp.dot(e.astype(BF16), wv[:, cols], preferred_element_type=F32)
        ow_ref[:, h * DHC:(h + 1) * DHC] = (o * (1.0 / jnp.sum(e, axis=1, keepdims=True))).astype(ow_ref.dtype)

    for g, imp, tied in fallbacks:
        @pl.when(tied > 0.0)
        def _(g=g, imp=imp):
            full = jnp.where(dead, NEG, jnp.where(forced, imp + SEL_FORCE, imp))
            sb_ref[g] = jnp.where(_topk_mask(full, jj, k), 0.0, NEG).astype(BF16)


def _nsa_cmp_win(tab, q, kc, vc, overlap, kw, vw, ns, tq=TILE):
    t = q.shape[0]
    ncp = kc.shape[1]
    nsp = overlap.shape[1]
    assert (tq - CMP_STRIDE + CMP_LEN - 1 + MAX_DISTANCE) // CMP_STRIDE < CMP_BIAS_LANES
    assert WINDOW % tq == 0 and min(SEL_TOPK, ns) > 3
    nwb = WINDOW // tq + 1
    full = lambda shape: pl.BlockSpec(shape, lambda qi: (0,) * len(shape))
    rows = pl.BlockSpec((tq, HC * DHC), lambda qi: (qi, 0))
    back = [pl.BlockSpec((tq, GC * DHC), functools.partial(lambda qi, b: (jnp.maximum(qi - b, 0), 0), b=b))
            for b in range(nwb - 1, -1, -1)]
    return pl.pallas_call(
        functools.partial(_nsa_cmp_kernel, tq=tq, ns=ns, nwb=nwb),
        grid=(t // tq,),
        in_specs=[_smem(), rows, full((GC, ncp, DHC)), full((GC, ncp, DHC)), full((ncp, nsp))] + back + back,
        out_specs=[rows, pl.BlockSpec((GC, tq, nsp), lambda qi: (0, qi, 0)), rows],
        out_shape=[jax.ShapeDtypeStruct((t, HC * DHC), BRANCH_DTYPE),
                   jax.ShapeDtypeStruct((GC, t, nsp), BF16),
                   jax.ShapeDtypeStruct((t, HC * DHC), BRANCH_DTYPE)],
        scratch_shapes=[pltpu.VMEM((2 * ncp, 128), BF16), pltpu.VMEM((HC, tq, 128), BF16),
                        pltpu.VMEM((HC, tq, nwb * tq), F32), pltpu.VMEM((GC, tq, nsp), F32)],
        compiler_params=_params("arbitrary"),
        name="nsa_compressed_window",
    )(tab, q, kc, vc, overlap, *([kw] * nwb), *([vw] * nwb))


def _nsa_sel_kernel(tab_ref, q_ref, sb_ref, k_ref, v_ref, o_ref, f_ref, m_ref, acc_ref, s_ref, hot_ref, *, nhalf,
                    nb, unroll):
    qi = pl.program_id(0)
    scale = DHC ** -0.5 * LOG2E
    tiles_per_half = 128 * SEL_BLOCK // TILE

    @pl.when(qi == 0)
    def _():
        _near_bias_tiles(f_ref, tab_ref, list(range(HC)))
        row = lax.broadcasted_iota(jnp.int32, hot_ref.shape, 0)
        lane = lax.broadcasted_iota(jnp.int32, hot_ref.shape, 1)
        hot_ref[...] = jnp.where(lane == (row // SEL_BLOCK) % 128, 1.0, 0.0).astype(BF16)

    hpc = PC // SEL_CHUNKS
    chunks = [(g, list(range(g * PC + c * hpc, g * PC + (c + 1) * hpc)))
              for g in range(GC) for c in range(SEL_CHUNKS)]
    q = q_ref[...]
    qas = []
    for g, heads in chunks:
        qc = jnp.concatenate([q[:, h * DHC:(h + 1) * DHC] for h in heads], axis=0)
        qc = (qc * scale).astype(BF16)
        sb = sb_ref[g]
        qas.append([jnp.concatenate([qc, jnp.concatenate([sb[:, hf * 128:(hf + 1) * 128]] * hpc, axis=0)], axis=1)
                    for hf in range(nhalf)])

    def qa_for(c, j):
        if nhalf == 1:
            return qas[c][0]
        return jnp.where(j >= tiles_per_half, qas[c][1], qas[c][0])

    def block_onehot(off, n):
        return hot_ref[pl.ds(off, n), :]

    def keys(g, off, n, hot):
        return jnp.concatenate([k_ref[pl.ds(off, n), g * DHC:(g + 1) * DHC], hot], axis=1)

    def values(g, off, n):
        ones_col = jnp.where(lax.broadcasted_iota(jnp.int32, (n, DHC), 1) == 0, 1.0, 0.0).astype(BF16)
        return jnp.concatenate([v_ref[pl.ds(off, n), g * DHC:(g + 1) * DHC], ones_col], axis=1)

    far_tiles = unroll * (jnp.maximum(qi - 1, 0) // unroll)
    groups = far_tiles // unroll
    last = jnp.maximum(groups - 1, 0)

    def group_onehot(grp):
        return block_onehot(pl.multiple_of(grp * (unroll * TILE), unroll * TILE), unroll * TILE)

    def produce(buf, grp, hot, only=None):
        off = pl.multiple_of(grp * (unroll * TILE), unroll * TILE)
        for c, (g, _) in enumerate(chunks):
            if only is None or c == only:
                s_ref[buf, c] = _dot_t(qa_for(c, grp * unroll), keys(g, off, unroll * TILE, hot))

    def consume(buf, grp, only=None):
        off = pl.multiple_of(grp * (unroll * TILE), unroll * TILE)
        for c, (g, _) in enumerate(chunks):
            if only is None or c == only:
                _flash_update(s_ref[buf, c], values(g, off, unroll * TILE), m_ref.at[c], acc_ref.at[c], False)

    produce(0, 0, group_onehot(0))
    start = jnp.minimum(far_tiles, nb - (unroll + 1))
    slot_hots = [block_onehot(pl.multiple_of((start + u) * TILE, TILE), TILE) for u in range(unroll + 1)]
    for c, (g, heads) in enumerate(chunks):
        cols = []
        for u in range(unroll + 1):
            j = start + u
            s = _dot_t(qa_for(c, j), keys(g, pl.multiple_of(j * TILE, TILE), TILE, slot_hots[u]))
            cols.append(jnp.concatenate(
                [s[i * TILE:(i + 1) * TILE] + _slot_bias(j, qi, far_tiles, f_ref[h, 0], f_ref[h, 1])
                 for i, h in enumerate(heads)], axis=0))
        vt = values(g, pl.multiple_of(start * TILE, TILE), (unroll + 1) * TILE)
        _flash_update(jnp.concatenate(cols, axis=1), vt, m_ref.at[c], acc_ref.at[c], True)

    def pair(k, carry):
        nxt = 2 * k + 1
        hot = group_onehot(nxt)
        for c in range(len(chunks)):
            produce(1, nxt, hot, c)
            consume(0, 2 * k, c)
        nxt = jnp.minimum(2 * k + 2, last)
        hot = group_onehot(nxt)
        for c in range(len(chunks)):
            produce(0, nxt, hot, c)
            consume(1, 2 * k + 1, c)
        return carry

    lax.fori_loop(0, groups // 2, pair, 0)

    @pl.when(groups % 2 == 1)
    def _():
        consume(0, groups - 1)


    for c, (g, heads) in enumerate(chunks):
        acc = acc_ref[c]
        o = acc[:, :DHC] / acc[:, DHC:DHC + 1]
        for i, h in enumerate(heads):
            o_ref[:, h * DHC:(h + 1) * DHC] = o[i * TILE:(i + 1) * TILE].astype(o_ref.dtype)


def _nsa_sel(tab, q, selbias, k, v):
    t = q.shape[0]
    nsp = selbias.shape[2]
    nchunks = GC * SEL_CHUNKS
    rows = HC // nchunks * TILE
    once = pl.Buffered(1)
    unroll = min(SEL_UNROLL, t // TILE - 1)
    return pl.pallas_call(
        functools.partial(_nsa_sel_kernel, nhalf=nsp // 128, nb=t // TILE, unroll=unroll),
        grid=(t // TILE,),
        in_specs=[_smem(),
                  pl.BlockSpec((TILE, HC * DHC), lambda qi: (qi, 0)),
                  pl.BlockSpec((GC, TILE, nsp), lambda qi: (0, qi, 0)),
                  pl.BlockSpec((t, GC * DHC), lambda qi: (0, 0), pipeline_mode=once),
                  pl.BlockSpec((t, GC * DHC), lambda qi: (0, 0), pipeline_mode=once)],
        out_specs=pl.BlockSpec((TILE, HC * DHC), lambda qi: (qi, 0)),
        out_shape=jax.ShapeDtypeStruct((t, HC * DHC), BRANCH_DTYPE),
        scratch_shapes=[pltpu.VMEM((HC, 2, TILE, TILE), F32),
                        pltpu.VMEM((nchunks, rows, 128), F32),
                        pltpu.VMEM((nchunks, rows, 2 * DHC), F32),
                        pltpu.VMEM((2, nchunks, rows, unroll * TILE), F32),
                        pltpu.VMEM((t, DHC), BF16)],
        compiler_params=_params("arbitrary"),
        name="nsa_selected",
    )(tab, q, selbias, k, v)


def _layer_ab(x, tab, w_in, conv_w, conv_b, w_rg, b_rg, w_ig, b_ig, lam, w_out, g_pre, g_post):
    da = HA * DHA
    segs = [(0, da), (da, da), (da, da), (2 * da, da), (3 * da, LRU_W), (3 * da + LRU_W, LRU_W)]
    q, k32, k, v, xr, gr = _rms_proj(x, g_pre, w_in.astype(BF16), segs, [F32, F32, BF16, BF16, F32, F32])
    o_a = _moba(tab, _moba_route(q, k32), k, v)

    eye = jnp.eye(w_rg.shape[0], dtype=F32)
    blockdiag = lambda wb: jnp.einsum('bij,bc->bicj', wb, eye).reshape(LRU_W, LRU_W).astype(BF16)
    o_b = _rglru(xr, gr, conv_w, conv_b, blockdiag(w_rg), b_rg, blockdiag(w_ig), b_ig, lam)
    wo = w_out.astype(BF16)
    return _outproj_ab(o_a, o_b, wo[:da], wo[da:], g_post, x)


def _layer_c(x, tab, w_in, pos_k, pos_v, k_w1, k_w2, v_w1, v_w2, w_out, g_pre, g_post):
    t = x.shape[0]
    dq = HC * DHC
    ckv = GC * DHC
    w = jnp.pad(w_in, ((0, 0), (0, 128 - 3 * HC))).astype(BF16)
    segs = _segments([dq, DHC, DHC, DHC, DHC, ckv, ckv, ckv, ckv, 128])
    q, kc0, kc1, vc0, vc1, ks, vs, kw, vw, gl = _rms_proj(
        x, g_pre, w, segs, [F32, F32, F32, F32, F32, BF16, BF16, BF16, BF16, F32])

    n = t // CMP_STRIDE
    kc = _compress([kc0, kc1], pos_k, k_w1.astype(BF16), k_w2.astype(BF16))
    vc = _compress([vc0, vc1], pos_v, v_w1.astype(BF16), v_w2.astype(BF16))

    ns = t // SEL_BLOCK
    nsp = -(-ns // 128) * 128
    cs = jnp.arange(n)[:, None] * CMP_STRIDE
    ss = jnp.arange(nsp)[None, :] * SEL_BLOCK
    overlap = ((cs < ss + SEL_BLOCK) & (cs + CMP_LEN > ss) & (jnp.arange(n)[:, None] < n - 1)
               & (jnp.arange(nsp)[None, :] < ns)).astype(BF16)
    o_c, selbias, o_w = _nsa_cmp_win(tab, q, kc, vc, overlap, kw, vw, ns)
    o_s = _nsa_sel(tab, q, selbias, ks, vs)
    return _outproj_c(o_c, o_s, o_w, gl, w_out.astype(BF16), g_post, x)


def kernel(x, rel_bias, ab_w_in, ab_conv_w, ab_conv_b, ab_w_rg, ab_b_rg, ab_w_ig, ab_b_ig, ab_lru_lambda, ab_w_out, c_w_in, c_cmp_pos_k, c_cmp_pos_v, c_cmp_k_w1, c_cmp_k_w2, c_cmp_v_w1, c_cmp_v_w2, c_w_out, norm_mix_pre, norm_mix_post, norm_mlp_pre, norm_mlp_post, w_up, w_down):
    batch = x.shape[0]
    depth = norm_mix_pre.shape[0]
    outs = []
    for b in range(batch):
        xb = x[b]
        for layer in range(depth):
            i = layer // 2
            if layer % 2 == 0:
                xb = _layer_ab(xb, rel_bias, ab_w_in[i], ab_conv_w[i], ab_conv_b[i], ab_w_rg[i], ab_b_rg[i],
                               ab_w_ig[i], ab_b_ig[i], ab_lru_lambda[i], ab_w_out[i],
                               norm_mix_pre[layer], norm_mix_post[layer])
            else:
                xb = _layer_c(xb, rel_bias, c_w_in[i], c_cmp_pos_k[i], c_cmp_pos_v[i], c_cmp_k_w1[i],
                              c_cmp_k_w2[i], c_cmp_v_w1[i], c_cmp_v_w2[i], c_w_out[i],
                              norm_mix_pre[layer], norm_mix_post[layer])
            xb = _mlp(xb, norm_mlp_pre[layer], w_up[layer], w_down[layer], norm_mlp_post[layer])
        outs.append(xb)
    return jnp.stack(outs)
```

```python
import functools
import math

import numpy as np
import jax
import jax.numpy as jnp
from jax import lax
from jax.experimental import pallas as pl
from jax.experimental.pallas import tpu as pltpu

F32 = jnp.float32
BF16 = jnp.bfloat16
HIGHEST = lax.Precision.HIGHEST
LOG2E = 1.4426950408889634

D = 1024
EPS = 1e-6
NEG = -1e30
N_BUCKETS = 32
MAX_DISTANCE = 128

HA = 8
DHA = 64
MOBA_BLOCK = 256
MOBA_TOPK = 3
LRU_W = 512
LRU_C = 8.0
CONV_W = 4
HC = 8
DHC = 128
GC = 2
PC = 4
CMP_LEN = 32
CMP_STRIDE = 16
SEL_BLOCK = 64
SEL_TOPK = 16
WINDOW = 512
SEL_FORCE = 1e4
D_FF = 4096

TILE = 256
MOBA_UNROLL = 2
ROUTE_TILES = 2
SEL_UNROLL = 2
SEL_CHUNKS = 2
CMP_BIAS_LANES = 32
CMP_MAX_FLOOR = -1e20
CMP_WIDTH_CLASS = 256
BRANCH_DTYPE = BF16
VMEM_LIMIT = 56 * 1024 * 1024
VMEM_LIMIT_RESIDENT = 60 * 1024 * 1024


def _bucket_np(d):
    n = np.maximum(d, 0)
    exact = N_BUCKETS // 2
    nf = np.maximum(n, 1).astype(np.float32)
    large = exact + (np.log(nf / np.float32(exact)) / np.float32(math.log(MAX_DISTANCE / exact))
                     * np.float32(N_BUCKETS - exact)).astype(np.int32)
    return np.where(n < exact, n, np.minimum(large, N_BUCKETS - 1))


_THR = [int(np.argmax(_bucket_np(np.arange(0, 4 * MAX_DISTANCE)) >= k)) for k in range(1, N_BUCKETS)]
assert _THR[-1] <= MAX_DISTANCE


def _bias_delta(d, tab_ref, h):
    acc = jnp.full(d.shape, tab_ref[0, h] - tab_ref[N_BUCKETS - 1, h], F32)
    for k in range(1, N_BUCKETS):
        acc = acc + jnp.where(d >= _THR[k - 1], tab_ref[k, h] - tab_ref[k - 1, h], 0.0)
    return acc


def _dot_t(a, b, **kw):
    return lax.dot_general(a, b, (((1,), (1,)), ((), ())), preferred_element_type=F32, **kw)


def _gelu_tanh(x):
    return 0.5 * x * (1.0 + jnp.tanh(0.7978845608028654 * (x + 0.044715 * x * x * x)))


def _sigmoid(x):
    return 0.5 * jnp.tanh(0.5 * x) + 0.5


def _params(*sem, vmem_limit=VMEM_LIMIT):
    return pltpu.CompilerParams(dimension_semantics=sem, vmem_limit_bytes=vmem_limit)


def _smem():
    return pl.BlockSpec(memory_space=pltpu.SMEM)


def _proj_kernel(x_ref, g_ref, w_ref, *out_refs, segs):
    x = x_ref[...]
    ms = jnp.mean(x * x, axis=-1, keepdims=True)
    hn = (x * lax.rsqrt(ms + EPS) * g_ref[...]).astype(BF16)
    for o_ref, (start, width) in zip(out_refs, segs):
        o_ref[...] = jnp.dot(hn, w_ref[:, start:start + width], preferred_element_type=F32).astype(o_ref.dtype)


def _rms_proj(x, g, w, segs, dtypes, tm=512):
    t = x.shape[0]
    return pl.pallas_call(
        functools.partial(_proj_kernel, segs=segs),
        grid=(t // tm,),
        in_specs=[pl.BlockSpec((tm, D), lambda i: (i, 0)),
                  pl.BlockSpec((1, D), lambda i: (0, 0)),
                  pl.BlockSpec(w.shape, lambda i: (0, 0))],
        out_specs=[pl.BlockSpec((tm, wd), lambda i: (i, 0)) for (_, wd) in segs],
        out_shape=[jax.ShapeDtypeStruct((t, wd), dt) for (_, wd), dt in zip(segs, dtypes)],
        compiler_params=_params("arbitrary"),
        name="rms_proj",
    )(x, g.reshape(1, D), w)


def _segments(widths):
    segs, off = [], 0
    for wd in widths:
        segs.append((off, wd))
        off += wd
    return segs


def _topk_mask(r, lane, k):
    sel = jnp.zeros(r.shape, jnp.bool_)
    lanef = lane.astype(F32)
    for _ in range(k):
        mx = jnp.max(r, axis=1, keepdims=True)
        first = jnp.min(jnp.where(r == mx, lanef, 1e9), axis=1, keepdims=True)
        pick = lanef == first
        sel = sel | (pick & (mx > NEG / 2))
        r = jnp.where(pick, NEG, r)
    return sel


def _removed_topk(r0, k):
    r = r0
    for _ in range(k):
        r = jnp.where(r == jnp.max(r, axis=1, keepdims=True), NEG, r)
    return (r0 > NEG / 2) & (r < NEG / 2)


def _near_bias_tiles(f_ref, tab_ref, heads):
    d0 = (lax.broadcasted_iota(jnp.int32, (TILE, TILE), 0)
          - lax.broadcasted_iota(jnp.int32, (TILE, TILE), 1))
    for i, h in enumerate(heads):
        f_ref[i, 0] = jnp.where(d0 >= 0, LOG2E * _bias_delta(jnp.maximum(d0, 0), tab_ref, h), NEG)
        f_ref[i, 1] = LOG2E * _bias_delta(d0 + TILE, tab_ref, h)


def _slot_bias(j, qi, lo, f_diag, f_prev):
    other = jnp.where((j >= lo) & (j <= qi), 0.0, NEG)
    return jnp.where(j == qi, f_diag, jnp.where(j == qi - 1, f_prev, other))


def _flash_update(s, vt, m_ref, acc_ref, first):
    rm = jnp.max(s, axis=1, keepdims=True)
    if first:
        m_ref[...] = jnp.broadcast_to(rm, m_ref.shape)
        p = jnp.exp2(s - rm)
        acc_ref[...] = jnp.dot(p.astype(BF16), vt, preferred_element_type=F32)
        return
    m_old = m_ref[...]
    m_new = jnp.maximum(m_old, rm)
    alpha = jnp.exp2(m_old - m_new)
    p = jnp.exp2(s - jnp.concatenate([m_new] * (s.shape[1] // m_new.shape[1]), axis=1))
    pv = jnp.dot(p.astype(BF16), vt, preferred_element_type=F32)
    acc_ref[...] = jnp.concatenate([alpha] * (pv.shape[1] // alpha.shape[1]), axis=1) * acc_ref[...] + pv
    m_ref[...] = m_new


def _moba_route_kernel(q_ref, k_ref, qa_ref, km_ref, *, tiles):
    scale = DHA ** -0.5 * LOG2E
    lane = lax.broadcasted_iota(jnp.int32, (TILE, 128), 1)

    @pl.when(pl.program_id(0) == 0)
    def _():
        km_ref[...] = jnp.zeros_like(km_ref)

    def augmented_query(q, head_lanes, in_aug, blk, sel, qi):
        keep = sel | (in_aug & (blk == qi))
        return jnp.where(head_lanes, q * scale, jnp.where(keep, 0.0, NEG)).astype(BF16)

    fallbacks = []
    for sub in range(tiles):
        qi = pl.program_id(0) * tiles + sub
        rows = slice(sub * TILE, (sub + 1) * TILE)
        for hh in range(HA):
            pair = slice(hh // 2 * 128, (hh // 2 + 1) * 128)
            q = q_ref[rows, pair]
            head_lanes = (lane < DHA) if hh % 2 == 0 else (lane >= DHA)
            qh = jnp.where(head_lanes, q, 0.0)
            route = _dot_t(qh, km_ref[:, pair], precision=HIGHEST)
            blk = lane - DHA if hh % 2 == 0 else lane
            in_aug = jnp.logical_not(head_lanes)
            cand = in_aug & (blk < qi)
            r = jnp.where(cand, route, NEG)
            sel = _removed_topk(r, MOBA_TOPK)
            qa_ref[rows, hh * 128:(hh + 1) * 128] = augmented_query(q, head_lanes, in_aug, blk, sel, qi)
            count = jnp.sum(jnp.where(sel, 1.0, 0.0), axis=1, keepdims=True)
            want = jnp.minimum(qi, MOBA_TOPK).astype(F32)
            tied = jnp.max(jnp.where(count != want, 1.0, 0.0))
            fallbacks.append((rows, hh, tied, (q, head_lanes, in_aug, blk), r, qi))

        mean = jnp.mean(k_ref[rows, :], axis=0, keepdims=True)
        km_ref[pl.ds(qi, 1), :] = mean
        km_ref[pl.ds(qi + DHA, 1), :] = mean

    for rows, hh, tied, args, r, qi in fallbacks:
        @pl.when(tied > 0.0)
        def _(rows=rows, hh=hh, args=args, r=r, qi=qi):
            qa_ref[rows, hh * 128:(hh + 1) * 128] = augmented_query(*args, _topk_mask(r, lane, MOBA_TOPK), qi)


def _moba_route(q, k):
    t = q.shape[0]
    assert t // MOBA_BLOCK <= DHA and MOBA_BLOCK == TILE
    tiles = min(ROUTE_TILES, t // TILE)
    rows = lambda w: pl.BlockSpec((tiles * TILE, w), lambda i: (i, 0))
    return pl.pallas_call(
        functools.partial(_moba_route_kernel, tiles=tiles),
        grid=(t // (tiles * TILE),),
        in_specs=[rows(HA * DHA), rows(HA * DHA)],
        out_specs=rows(HA * 128),
        out_shape=jax.ShapeDtypeStruct((t, HA * 128), BF16),
        scratch_shapes=[pltpu.VMEM((2 * DHA, HA * DHA), F32)],
        compiler_params=_params("arbitrary"),
        name="moba_route",
    )(q, k)


def _moba_kernel(tab_ref, qa_ref, k_ref, v_ref, o_ref, f_ref, m_ref, acc_ref, s_ref, *, nb, unroll):
    qi = pl.program_id(0)
    lane = lax.broadcasted_iota(jnp.int32, (TILE, 128), 1)

    @pl.when(qi == 0)
    def _():
        _near_bias_tiles(f_ref, tab_ref, list(range(HA)))

    def own_lanes(hh, rows):
        ln = lax.broadcasted_iota(jnp.int32, (rows, 128), 1)
        return (ln < DHA) if hh % 2 == 0 else (ln >= DHA)

    def keys(j, hh, n):
        pair = slice(hh // 2 * 128, (hh // 2 + 1) * 128)
        parts = []
        for u in range(n):
            off = pl.multiple_of((j + u) * TILE, TILE)
            hot = jnp.where(lane == (j + u) + (DHA if hh % 2 == 0 else 0), 1.0, 0.0).astype(BF16)
            parts.append(jnp.where(own_lanes(hh, TILE), k_ref[pl.ds(off, TILE), pair], hot))
        return jnp.concatenate(parts, axis=0)

    def values(j, hh, n):
        pair = slice(hh // 2 * 128, (hh // 2 + 1) * 128)
        off = pl.multiple_of(j * TILE, TILE)
        ln = lax.broadcasted_iota(jnp.int32, (n * TILE, 128), 1)
        ones_col = jnp.where(ln == (DHA if hh % 2 == 0 else 0), 1.0, 0.0).astype(BF16)
        return jnp.where(own_lanes(hh, n * TILE), v_ref[pl.ds(off, n * TILE), pair], ones_col)

    def scores(j, hh, n):
        return _dot_t(qa_ref[:, hh * 128:(hh + 1) * 128], keys(j, hh, n))

    far_tiles = unroll * (jnp.maximum(qi - 1, 0) // unroll)
    groups = far_tiles // unroll
    last = jnp.maximum(groups - 1, 0)

    def produce(buf, grp):
        for hh in range(HA):
            s_ref[buf, hh] = scores(grp * unroll, hh, unroll)

    def consume(buf, grp):
        for hh in range(HA):
            _flash_update(s_ref[buf, hh], values(grp * unroll, hh, unroll), m_ref.at[hh], acc_ref.at[hh], False)

    produce(0, 0)
    start = jnp.minimum(far_tiles, nb - (unroll + 1))
    for hh in range(HA):
        bias = jnp.concatenate([_slot_bias(start + u, qi, far_tiles, f_ref[hh, 0], f_ref[hh, 1])
                                for u in range(unroll + 1)], axis=1)
        _flash_update(scores(start, hh, unroll + 1) + bias, values(start, hh, unroll + 1),
                      m_ref.at[hh], acc_ref.at[hh], True)

    def pair(k, c):
        produce(1, 2 * k + 1)
        consume(0, 2 * k)
        produce(0, jnp.minimum(2 * k + 2, last))
        consume(1, 2 * k + 1)
        return c

    lax.fori_loop(0, groups // 2, pair, 0)

    @pl.when(groups % 2 == 1)
    def _():
        consume(0, groups - 1)

    for pp in range(HA // 2):
        even, odd = acc_ref[2 * pp], acc_ref[2 * pp + 1]
        o_ref[:, pp * 128:(pp + 1) * 128] = jnp.where(lane < DHA, even * (1.0 / even[:, DHA:DHA + 1]),
                                                      odd * (1.0 / odd[:, 0:1])).astype(o_ref.dtype)


def _moba(tab, qa, k, v):
    t = qa.shape[0]
    once = pl.Buffered(1)
    unroll = min(MOBA_UNROLL, t // TILE - 1)
    return pl.pallas_call(
        functools.partial(_moba_kernel, nb=t // TILE, unroll=unroll),
        grid=(t // TILE,),
        in_specs=[_smem(),
                  pl.BlockSpec((TILE, HA * 128), lambda qi: (qi, 0)),
                  pl.BlockSpec((t, HA * DHA), lambda qi: (0, 0), pipeline_mode=once),
                  pl.BlockSpec((t, HA * DHA), lambda qi: (0, 0), pipeline_mode=once)],
        out_specs=pl.BlockSpec((TILE, HA * DHA), lambda qi: (qi, 0)),
        out_shape=jax.ShapeDtypeStruct((t, HA * DHA), BF16),
        scratch_shapes=[pltpu.VMEM((HA, 2, TILE, TILE), F32),
                        pltpu.VMEM((HA, TILE, 128), F32),
                        pltpu.VMEM((HA, TILE, 128), F32),
                        pltpu.VMEM((2, HA, TILE, unroll * TILE), F32)],
        compiler_params=_params("arbitrary", vmem_limit=VMEM_LIMIT_RESIDENT),
        name="moba_attention",
    )(tab, qa, k, v)


def _rglru_kernel(xr_ref, gr_ref, cw_ref, cb_ref, wrg_ref, brg_ref, wig_ref, big_ref, lam_ref,
                  o_ref, xbuf, a_s, b_s, hcar, *, tt):
    @pl.when(pl.program_id(0) == 0)
    def _():
        xbuf[0:8, :] = jnp.zeros((8, LRU_W), F32)
        hcar[...] = jnp.zeros_like(hcar)

    xbuf[8:8 + tt, :] = xr_ref[...]
    xc = cb_ref[...]
    for j in range(CONV_W):
        xc = xc + xbuf[8 - (CONV_W - 1) + j:8 - (CONV_W - 1) + j + tt, :] * cw_ref[j:j + 1, :]
    xbuf[0:8, :] = xbuf[tt:tt + 8, :]

    xcb = xc.astype(BF16)
    r = _sigmoid(jnp.dot(xcb, wrg_ref[...], preferred_element_type=F32) + brg_ref[...])
    ig = _sigmoid(jnp.dot(xcb, wig_ref[...], preferred_element_type=F32) + big_ref[...])
    z = -lam_ref[...]
    softplus = jnp.maximum(z, 0.0) + jnp.log(1.0 + jnp.exp(-jnp.abs(z)))
    a = jnp.exp(-LRU_C * r * softplus)
    a_s[...] = a
    b_s[...] = jnp.sqrt(1.0 - a * a) * ig * xc

    row = lax.broadcasted_iota(jnp.int32, (8, LRU_W), 0)

    def body(gi, h):
        off = pl.multiple_of(gi * 8, 8)
        a8 = a_s[pl.ds(off, 8), :]
        b8 = b_s[pl.ds(off, 8), :]
        for s in (1, 2, 4):
            a_sh = jnp.where(row >= s, pltpu.roll(a8, s, 0), 1.0)
            b_sh = jnp.where(row >= s, pltpu.roll(b8, s, 0), 0.0)
            b8 = a8 * b_sh + b8
            a8 = a8 * a_sh
        h8 = b8 + a8 * h
        b_s[pl.ds(off, 8), :] = h8
        return h8[7:8, :]

    hcar[0:1, :] = lax.fori_loop(0, tt // 8, body, hcar[0:1, :])
    o_ref[...] = (b_s[...] * _gelu_tanh(gr_ref[...])).astype(o_ref.dtype)


def _rglru(xr, gr, conv_w, conv_b, wrg, b_rg, wig, b_ig, lam, tt=512):
    t = xr.shape[0]
    row = lambda a: a.reshape(1, LRU_W)
    full = lambda shape: pl.BlockSpec(shape, lambda i: (0, 0))
    return pl.pallas_call(
        functools.partial(_rglru_kernel, tt=tt),
        grid=(t // tt,),
        in_specs=[pl.BlockSpec((tt, LRU_W), lambda i: (i, 0)),
                  pl.BlockSpec((tt, LRU_W), lambda i: (i, 0)),
                  full((CONV_W, LRU_W)), full((1, LRU_W)),
                  full((LRU_W, LRU_W)), full((1, LRU_W)),
                  full((LRU_W, LRU_W)), full((1, LRU_W)), full((1, LRU_W))],
        out_specs=pl.BlockSpec((tt, LRU_W), lambda i: (i, 0)),
        out_shape=jax.ShapeDtypeStruct((t, LRU_W), BF16),
        scratch_shapes=[pltpu.VMEM((tt + 8, LRU_W), F32),
                        pltpu.VMEM((tt, LRU_W), F32),
                        pltpu.VMEM((tt, LRU_W), F32),
                        pltpu.VMEM((8, LRU_W), F32)],
        compiler_params=_params("arbitrary"),
        name="rglru",
    )(xr, gr, conv_w, row(conv_b), wrg, row(b_rg), wig, row(b_ig), row(lam))


def _post(y, g_ref, x_ref, o_ref):
    ms = jnp.mean(y * y, axis=-1, keepdims=True)
    o_ref[...] = x_ref[...] + y * lax.rsqrt(ms + EPS) * g_ref[...]


def _outproj_ab_kernel(a1_ref, a2_ref, w1_ref, w2_ref, g_ref, x_ref, o_ref):
    y = (jnp.dot(a1_ref[...], w1_ref[...], preferred_element_type=F32)
         + jnp.dot(a2_ref[...], w2_ref[...], preferred_element_type=F32))
    _post(y, g_ref, x_ref, o_ref)


def _outproj_ab(a1, a2, w1, w2, g, x, tm=512):
    t = x.shape[0]
    rows = lambda w: pl.BlockSpec((tm, w), lambda i: (i, 0))
    full = lambda shape: pl.BlockSpec(shape, lambda i: (0, 0))
    return pl.pallas_call(
        _outproj_ab_kernel,
        grid=(t // tm,),
        in_specs=[rows(a1.shape[1]), rows(a2.shape[1]), full(w1.shape), full(w2.shape),
                  full((1, D)), rows(D)],
        out_specs=rows(D),
        out_shape=jax.ShapeDtypeStruct((t, D), F32),
        compiler_params=_params("arbitrary"),
        name="outproj_ab",
    )(a1, a2, w1, w2, g.reshape(1, D), x)


def _outproj_c_kernel(oc_ref, os_ref, ow_ref, gl_ref, w_ref, g_ref, x_ref, o_ref):
    gates = jax.nn.sigmoid(gl_ref[...])
    parts = []
    for h in range(HC):
        sl = slice(h * DHC, (h + 1) * DHC)
        parts.append(gates[:, 3 * h:3 * h + 1] * oc_ref[:, sl]
                     + gates[:, 3 * h + 1:3 * h + 2] * os_ref[:, sl]
                     + gates[:, 3 * h + 2:3 * h + 3] * ow_ref[:, sl])
    o = jnp.concatenate(parts, axis=1).astype(BF16)
    _post(jnp.dot(o, w_ref[...], preferred_element_type=F32), g_ref, x_ref, o_ref)


def _outproj_c(oc, os_, ow, gl, w, g, x, tm=512):
    t = x.shape[0]
    rows = lambda w_: pl.BlockSpec((tm, w_), lambda i: (i, 0))
    full = lambda shape: pl.BlockSpec(shape, lambda i: (0, 0))
    return pl.pallas_call(
        _outproj_c_kernel,
        grid=(t // tm,),
        in_specs=[rows(D), rows(D), rows(D), rows(128), full(w.shape), full((1, D)), rows(D)],
        out_specs=rows(D),
        out_shape=jax.ShapeDtypeStruct((t, D), F32),
        compiler_params=_params("arbitrary"),
        name="outproj_c",
    )(oc, os_, ow, gl, w, g.reshape(1, D), x)


def _mlp_kernel(x_ref, gpre_ref, wup_ref, wdn_ref, gpost_ref, o_ref, h_s, acc_s):
    c = pl.program_id(1)

    @pl.when(c == 0)
    def _():
        x = x_ref[...]
        ms = jnp.mean(x * x, axis=-1, keepdims=True)
        h_s[...] = (x * lax.rsqrt(ms + EPS) * gpre_ref[...]).astype(BF16)
        acc_s[...] = jnp.zeros_like(acc_s)

    u = jnp.maximum(jnp.dot(h_s[...], wup_ref[...].astype(BF16), preferred_element_type=F32), 0.0)
    acc_s[...] += jnp.dot((u * u).astype(BF16), wdn_ref[...].astype(BF16), preferred_element_type=F32)

    @pl.when(c == pl.num_programs(1) - 1)
    def _():
        _post(acc_s[...], gpost_ref, x_ref, o_ref)


def _mlp(x, gpre, wup, wdn, gpost, layer, tm=1024, tf=512):
    t = x.shape[0]
    return pl.pallas_call(
        _mlp_kernel,
        grid=(t // tm, D_FF // tf),
        in_specs=[pl.BlockSpec((tm, D), lambda i, c: (i, 0)),
                  pl.BlockSpec((1, D), lambda i, c: (0, 0)),
                  pl.BlockSpec((None, D, tf), lambda i, c: (layer, 0, c)),
                  pl.BlockSpec((None, tf, D), lambda i, c: (layer, c, 0)),
                  pl.BlockSpec((1, D), lambda i, c: (0, 0))],
        out_specs=pl.BlockSpec((tm, D), lambda i, c: (i, 0)),
        out_shape=jax.ShapeDtypeStruct((t, D), F32),
        scratch_shapes=[pltpu.VMEM((tm, D), BF16), pltpu.VMEM((tm, D), F32)],
        compiler_params=_params("arbitrary", "arbitrary"),
        name="mlp",
    )(x, gpre.reshape(1, D), wup, wdn, gpost.reshape(1, D))


def _compress_kernel(*refs):
    x_refs, (pos_ref, w1_ref, w2_ref, o_ref) = refs[:GC], refs[GC:]
    n = o_ref.shape[1]
    for g, x_ref in enumerate(x_refs):
        top = jnp.zeros((n, w1_ref.shape[1]), F32)
        bot = jnp.zeros((n, w1_ref.shape[1]), F32)
        for l in range(CMP_STRIDE):
            xl = x_ref[pl.ds(l, n, stride=CMP_STRIDE), :]
            lo, hi = l * DHC, (CMP_STRIDE + l) * DHC
            top = top + jnp.dot((xl + pos_ref[l:l + 1, :]).astype(BF16), w1_ref[lo:lo + DHC, :],
                                preferred_element_type=F32)
            bot = bot + jnp.dot((xl + pos_ref[CMP_STRIDE + l:CMP_STRIDE + l + 1, :]).astype(BF16),
                                w1_ref[hi:hi + DHC, :], preferred_element_type=F32)
        hid = top + pltpu.roll(bot, n - 1, 0)
        o_ref[g] = jnp.dot(_gelu_tanh(hid).astype(BF16), w2_ref[...],
                           preferred_element_type=F32).astype(o_ref.dtype)


def _compress(xs, pos, w1, w2):
    t = xs[0].shape[0]
    n = t // CMP_STRIDE
    full = lambda shape: pl.BlockSpec(shape, lambda i: (0,) * len(shape))
    return pl.pallas_call(
        _compress_kernel,
        grid=(1,),
        in_specs=[full((t, DHC))] * GC + [full(pos.shape), full(w1.shape), full(w2.shape)],
        out_specs=full((GC, n, DHC)),
        out_shape=jax.ShapeDtypeStruct((GC, n, DHC), BF16),
        compiler_params=_params("arbitrary"),
        name="nsa_compress",
    )(*xs, pos, w1, w2)


def _stack_heads(q):
    return jnp.concatenate([q[:, p * DHC:(p + 1) * DHC] for p in range(PC)], axis=0)


def _nsa_cmp_kernel(tab_ref, q_ref, kc_ref, vc_ref, ov_ref, *refs, tq, ns, nwb):
    kw_refs, vw_refs = refs[:nwb], refs[nwb:2 * nwb]
    oc_ref, sb_ref, ow_ref, e_ref, qb_ref, fw_ref, imp_ref = refs[2 * nwb:]
    qi = pl.program_id(0)
    ncp = kc_ref.shape[1]
    nsp = ov_ref.shape[1]
    scale = DHC ** -0.5 * LOG2E
    cpq = tq // CMP_STRIDE
    k = min(SEL_TOPK, ns)

    @pl.when(qi == 0)
    def _():
        u = lax.broadcasted_iota(jnp.int32, (2 * ncp, 128), 0)
        lane = lax.broadcasted_iota(jnp.int32, (2 * ncp, 128), 1)
        hit = (lane < 2 * CMP_BIAS_LANES) & (u + lane % CMP_BIAS_LANES == ncp - 1)
        after = (lane == 2 * CMP_BIAS_LANES) & (u >= ncp)
        e_ref[...] = (jnp.where(hit, 1.0, 0.0) + jnp.where(after, 1.0, 0.0)).astype(BF16)
        i = lax.broadcasted_iota(jnp.int32, (tq, 128), 0)
        lane = lax.broadcasted_iota(jnp.int32, (tq, 128), 1)
        d = i + CMP_STRIDE * (lane % CMP_BIAS_LANES) - (tq - CMP_STRIDE + CMP_LEN - 1)
        dw = (WINDOW + lax.broadcasted_iota(jnp.int32, (tq, nwb * tq), 0)
              - lax.broadcasted_iota(jnp.int32, (tq, nwb * tq), 1))
        for h in range(HC):
            b = jnp.where(d >= 0, LOG2E * _bias_delta(jnp.maximum(d, 0), tab_ref, h), NEG)
            hi = b.astype(BF16).astype(F32)
            qb = jnp.where(lane < CMP_BIAS_LANES, hi, jnp.where(lane < 2 * CMP_BIAS_LANES, b - hi, 0.0))
            qb_ref[h] = jnp.where(lane == 2 * CMP_BIAS_LANES, NEG, qb).astype(BF16)
            fw_ref[h] = jnp.where((dw >= 0) & (dw < WINDOW),
                                  LOG2E * _bias_delta(jnp.maximum(dw, 0), tab_ref, h), NEG)

    off = pl.multiple_of(ncp - cpq * (qi + 1), 16)
    jj = lax.broadcasted_iota(jnp.int32, (tq, nsp), 1)
    cur = (qi * tq + lax.broadcasted_iota(jnp.int32, (tq, nsp), 0)) // SEL_BLOCK
    forced = (jj == 0) | (jj == cur) | (jj == cur - 1)
    dead = (jj > cur) | (jj >= ns)
    q = q_ref[...]
    qs = [(q[:, h * DHC:(h + 1) * DHC] * scale).astype(BF16) for h in range(HC)]

    def compressed(width):
        onehots = e_ref[pl.ds(off, width), :]
        for g in range(GC):
            heads = range(g * PC, (g + 1) * PC)
            qa = jnp.concatenate([jnp.concatenate([qs[h] for h in heads], axis=0),
                                  jnp.concatenate([qb_ref[h] for h in heads], axis=0)], axis=1)
            s_all = _dot_t(qa, jnp.concatenate([kc_ref[g, 0:width, :], onehots], axis=1))
            psum = jnp.zeros((tq, width), F32)
            for p, h in enumerate(heads):
                s = s_all[p * tq:(p + 1) * tq]
                m = jnp.maximum(jnp.max(s, axis=1, keepdims=True), CMP_MAX_FLOOR)
                e = jnp.exp2(s - m)
                pc = e * (1.0 / jnp.maximum(jnp.sum(e, axis=1, keepdims=True), 1e-30))
                oc_ref[:, h * DHC:(h + 1) * DHC] = jnp.dot(pc.astype(BF16), vc_ref[g, 0:width, :],
                                                           preferred_element_type=F32).astype(oc_ref.dtype)
                psum = psum + pc
            hi = psum.astype(BF16)
            lo = (psum - hi.astype(F32)).astype(BF16)
            imp_ref[g] = (jnp.dot(hi, ov_ref[0:width, :], preferred_element_type=F32)
                          + jnp.dot(lo, ov_ref[0:width, :], preferred_element_type=F32))

    cw = min(CMP_WIDTH_CLASS, ncp)
    cls = (cpq * (qi + 1) - 1) // cw
    for c in range(ncp // cw):
        @pl.when(cls == c)
        def _(c=c):
            compressed((c + 1) * cw)

    fallbacks = []
    for g in range(GC):
        imp = imp_ref[g]
        rest = jnp.where(forced | dead, NEG, imp)
        sel_rest = _removed_topk(rest, k - 3)
        sel = sel_rest | (forced & jnp.logical_not(dead))
        sb_ref[g] = jnp.where(sel, 0.0, NEG).astype(BF16)
        count = jnp.sum(jnp.where(sel_rest, 1.0, 0.0), axis=1, keepdims=True)
        cand = jnp.sum(jnp.where(forced | dead, 0.0, 1.0), axis=1, keepdims=True)
        tied = jnp.max(jnp.where(count != jnp.minimum(cand, float(k - 3)), 1.0, 0.0))
        fallbacks.append((g, imp, tied))

    wk = jnp.concatenate([r[...] for r in kw_refs], axis=0)
    wv = jnp.concatenate([r[...] for r in vw_refs], axis=0)
    blk = lax.broadcasted_iota(jnp.int32, (nwb * tq, DHC), 0) // tq
    flags = jnp.where(blk + qi < nwb - 1, 1.0, 0.0).astype(BF16)
    off_lane = jnp.where(lax.broadcasted_iota(jnp.int32, (tq, DHC), 1) == 0, NEG, 0.0).astype(BF16)
    for h in range(HC):
        g = h // PC
        cols = slice(g * DHC, (g + 1) * DHC)
        s = _dot_t(jnp.concatenate([qs[h], off_lane], axis=1),
                   jnp.concatenate([wk[:, cols], flags], axis=1)) + fw_ref[h]
        e = jnp.exp2(s - jnp.max(s, axis=1, keepdims=True))
        o = jnp.dot(e.astype(BF16), wv[:, cols], preferred_element_type=F32)
        ow_ref[:, h * DHC:(h + 1) * DHC] = (o * (1.0 / jnp.sum(e, axis=1, keepdims=True))).astype(ow_ref.dtype)

    for g, imp, tied in fallbacks:
        @pl.when(tied > 0.0)
        def _(g=g, imp=imp):
            full = jnp.where(dead, NEG, jnp.where(forced, imp + SEL_FORCE, imp))
            sb_ref[g] = jnp.where(_topk_mask(full, jj, k), 0.0, NEG).astype(BF16)


def _nsa_cmp_win(tab, q, kc, vc, overlap, kw, vw, ns, tq=TILE):
    t = q.shape[0]
    ncp = kc.shape[1]
    nsp = overlap.shape[1]
    assert (tq - CMP_STRIDE + CMP_LEN - 1 + MAX_DISTANCE) // CMP_STRIDE < CMP_BIAS_LANES
    assert WINDOW % tq == 0 and min(SEL_TOPK, ns) > 3
    nwb = WINDOW // tq + 1
    full = lambda shape: pl.BlockSpec(shape, lambda qi: (0,) * len(shape))
    rows = pl.BlockSpec((tq, HC * DHC), lambda qi: (qi, 0))
    back = [pl.BlockSpec((tq, GC * DHC), functools.partial(lambda qi, b: (jnp.maximum(qi - b, 0), 0), b=b))
            for b in range(nwb - 1, -1, -1)]
    return pl.pallas_call(
        functools.partial(_nsa_cmp_kernel, tq=tq, ns=ns, nwb=nwb),
        grid=(t // tq,),
        in_specs=[_smem(), rows, full((GC, ncp, DHC)), full((GC, ncp, DHC)), full((ncp, nsp))] + back + back,
        out_specs=[rows, pl.BlockSpec((GC, tq, nsp), lambda qi: (0, qi, 0)), rows],
        out_shape=[jax.ShapeDtypeStruct((t, HC * DHC), BRANCH_DTYPE),
                   jax.ShapeDtypeStruct((GC, t, nsp), BF16),
                   jax.ShapeDtypeStruct((t, HC * DHC), BRANCH_DTYPE)],
        scratch_shapes=[pltpu.VMEM((2 * ncp, 128), BF16), pltpu.VMEM((HC, tq, 128), BF16),
                        pltpu.VMEM((HC, tq, nwb * tq), F32), pltpu.VMEM((GC, tq, nsp), F32)],
        compiler_params=_params("arbitrary"),
        name="nsa_compressed_window",
    )(tab, q, kc, vc, overlap, *([kw] * nwb), *([vw] * nwb))


def _nsa_sel_kernel(tab_ref, q_ref, sb_ref, k_ref, v_ref, o_ref, f_ref, m_ref, acc_ref, s_ref, hot_ref, *, nhalf,
                    nb, unroll):
    qi = pl.program_id(0)
    scale = DHC ** -0.5 * LOG2E
    tiles_per_half = 128 * SEL_BLOCK // TILE

    @pl.when(qi == 0)
    def _():
        _near_bias_tiles(f_ref, tab_ref, list(range(HC)))
        row = lax.broadcasted_iota(jnp.int32, hot_ref.shape, 0)
        lane = lax.broadcasted_iota(jnp.int32, hot_ref.shape, 1)
        hot_ref[...] = jnp.where(lane == (row // SEL_BLOCK) % 128, 1.0, 0.0).astype(BF16)

    hpc = PC // SEL_CHUNKS
    chunks = [(g, list(range(g * PC + c * hpc, g * PC + (c + 1) * hpc)))
              for g in range(GC) for c in range(SEL_CHUNKS)]
    q = q_ref[...]
    qas = []
    for g, heads in chunks:
        qc = jnp.concatenate([q[:, h * DHC:(h + 1) * DHC] for h in heads], axis=0)
        qc = (qc * scale).astype(BF16)
        sb = sb_ref[g]
        qas.append([jnp.concatenate([qc, jnp.concatenate([sb[:, hf * 128:(hf + 1) * 128]] * hpc, axis=0)], axis=1)
                    for hf in range(nhalf)])

    def qa_for(c, j):
        if nhalf == 1:
            return qas[c][0]
        return jnp.where(j >= tiles_per_half, qas[c][1], qas[c][0])

    def block_onehot(off, n):
        return hot_ref[pl.ds(off, n), :]

    def keys(g, off, n, hot):
        return jnp.concatenate([k_ref[pl.ds(off, n), g * DHC:(g + 1) * DHC], hot], axis=1)

    def values(g, off, n):
        ones_col = jnp.where(lax.broadcasted_iota(jnp.int32, (n, DHC), 1) == 0, 1.0, 0.0).astype(BF16)
        return jnp.concatenate([v_ref[pl.ds(off, n), g * DHC:(g + 1) * DHC], ones_col], axis=1)

    far_tiles = unroll * (jnp.maximum(qi - 1, 0) // unroll)
    groups = far_tiles // unroll
    last = jnp.maximum(groups - 1, 0)

    def group_onehot(grp):
        return block_onehot(pl.multiple_of(grp * (unroll * TILE), unroll * TILE), unroll * TILE)

    def produce(buf, grp, hot, only=None):
        off = pl.multiple_of(grp * (unroll * TILE), unroll * TILE)
        for c, (g, _) in enumerate(chunks):
            if only is None or c == only:
                s_ref[buf, c] = _dot_t(qa_for(c, grp * unroll), keys(g, off, unroll * TILE, hot))

    def consume(buf, grp, only=None):
        off = pl.multiple_of(grp * (unroll * TILE), unroll * TILE)
        for c, (g, _) in enumerate(chunks):
            if only is None or c == only:
                _flash_update(s_ref[buf, c], values(g, off, unroll * TILE), m_ref.at[c], acc_ref.at[c], False)

    produce(0, 0, group_onehot(0))
    start = jnp.minimum(far_tiles, nb - (unroll + 1))
    slot_hots = [block_onehot(pl.multiple_of((start + u) * TILE, TILE), TILE) for u in range(unroll + 1)]
    for c, (g, heads) in enumerate(chunks):
        cols = []
        for u in range(unroll + 1):
            j = start + u
            s = _dot_t(qa_for(c, j), keys(g, pl.multiple_of(j * TILE, TILE), TILE, slot_hots[u]))
            cols.append(jnp.concatenate(
                [s[i * TILE:(i + 1) * TILE] + _slot_bias(j, qi, far_tiles, f_ref[h, 0], f_ref[h, 1])
                 for i, h in enumerate(heads)], axis=0))
        vt = values(g, pl.multiple_of(start * TILE, TILE), (unroll + 1) * TILE)
        _flash_update(jnp.concatenate(cols, axis=1), vt, m_ref.at[c], acc_ref.at[c], True)

    def pair(k, carry):
        nxt = 2 * k + 1
        hot = group_onehot(nxt)
        for c in range(len(chunks)):
            produce(1, nxt, hot, c)
            consume(0, 2 * k, c)
        nxt = jnp.minimum(2 * k + 2, last)
        hot = group_onehot(nxt)
        for c in range(len(chunks)):
            produce(0, nxt, hot, c)
            consume(1, 2 * k + 1, c)
        return carry

    lax.fori_loop(0, groups // 2, pair, 0)

    @pl.when(groups % 2 == 1)
    def _():
        consume(0, groups - 1)


    for c, (g, heads) in enumerate(chunks):
        acc = acc_ref[c]
        o = acc[:, :DHC] / acc[:, DHC:DHC + 1]
        for i, h in enumerate(heads):
            o_ref[:, h * DHC:(h + 1) * DHC] = o[i * TILE:(i + 1) * TILE].astype(o_ref.dtype)


def _nsa_sel(tab, q, selbias, k, v):
    t = q.shape[0]
    nsp = selbias.shape[2]
    nchunks = GC * SEL_CHUNKS
    rows = HC // nchunks * TILE
    once = pl.Buffered(1)
    unroll = min(SEL_UNROLL, t // TILE - 1)
    return pl.pallas_call(
        functools.partial(_nsa_sel_kernel, nhalf=nsp // 128, nb=t // TILE, unroll=unroll),
        grid=(t // TILE,),
        in_specs=[_smem(),
                  pl.BlockSpec((TILE, HC * DHC), lambda qi: (qi, 0)),
                  pl.BlockSpec((GC, TILE, nsp), lambda qi: (0, qi, 0)),
                  pl.BlockSpec((t, GC * DHC), lambda qi: (0, 0), pipeline_mode=once),
                  pl.BlockSpec((t, GC * DHC), lambda qi: (0, 0), pipeline_mode=once)],
        out_specs=pl.BlockSpec((TILE, HC * DHC), lambda qi: (qi, 0)),
        out_shape=jax.ShapeDtypeStruct((t, HC * DHC), BRANCH_DTYPE),
        scratch_shapes=[pltpu.VMEM((HC, 2, TILE, TILE), F32),
                        pltpu.VMEM((nchunks, rows, 128), F32),
                        pltpu.VMEM((nchunks, rows, 2 * DHC), F32),
                        pltpu.VMEM((2, nchunks, rows, unroll * TILE), F32),
                        pltpu.VMEM((t, DHC), BF16)],
        compiler_params=_params("arbitrary"),
        name="nsa_selected",
    )(tab, q, selbias, k, v)


def _layer_ab(x, tab, w_in, conv_w, conv_b, w_rg, b_rg, w_ig, b_ig, lam, w_out, g_pre, g_post):
    da = HA * DHA
    segs = [(0, da), (da, da), (da, da), (2 * da, da), (3 * da, LRU_W), (3 * da + LRU_W, LRU_W)]
    q, k32, k, v, xr, gr = _rms_proj(x, g_pre, w_in.astype(BF16), segs, [F32, F32, BF16, BF16, F32, F32])
    o_a = _moba(tab, _moba_route(q, k32), k, v)

    eye = jnp.eye(w_rg.shape[0], dtype=F32)
    blockdiag = lambda wb: jnp.einsum('bij,bc->bicj', wb, eye).reshape(LRU_W, LRU_W).astype(BF16)
    o_b = _rglru(xr, gr, conv_w, conv_b, blockdiag(w_rg), b_rg, blockdiag(w_ig), b_ig, lam)
    wo = w_out.astype(BF16)
    return _outproj_ab(o_a, o_b, wo[:da], wo[da:], g_post, x)


def _layer_c(x, tab, w_in, pos_k, pos_v, k_w1, k_w2, v_w1, v_w2, w_out, g_pre, g_post):
    t = x.shape[0]
    dq = HC * DHC
    ckv = GC * DHC
    w = jnp.pad(w_in, ((0, 0), (0, 128 - 3 * HC))).astype(BF16)
    segs = _segments([dq, DHC, DHC, DHC, DHC, ckv, ckv, ckv, ckv, 128])
    q, kc0, kc1, vc0, vc1, ks, vs, kw, vw, gl = _rms_proj(
        x, g_pre, w, segs, [F32, F32, F32, F32, F32, BF16, BF16, BF16, BF16, F32])

    n = t // CMP_STRIDE
    kc = _compress([kc0, kc1], pos_k, k_w1.astype(BF16), k_w2.astype(BF16))
    vc = _compress([vc0, vc1], pos_v, v_w1.astype(BF16), v_w2.astype(BF16))

    ns = t // SEL_BLOCK
    nsp = -(-ns // 128) * 128
    cs = jnp.arange(n)[:, None] * CMP_STRIDE
    ss = jnp.arange(nsp)[None, :] * SEL_BLOCK
    overlap = ((cs < ss + SEL_BLOCK) & (cs + CMP_LEN > ss) & (jnp.arange(n)[:, None] < n - 1)
               & (jnp.arange(nsp)[None, :] < ns)).astype(BF16)
    o_c, selbias, o_w = _nsa_cmp_win(tab, q, kc, vc, overlap, kw, vw, ns)
    o_s = _nsa_sel(tab, q, selbias, ks, vs)
    return _outproj_c(o_c, o_s, o_w, gl, w_out.astype(BF16), g_post, x)


def kernel(x, rel_bias, ab_w_in, ab_conv_w, ab_conv_b, ab_w_rg, ab_b_rg, ab_w_ig, ab_b_ig, ab_lru_lambda, ab_w_out, c_w_in, c_cmp_pos_k, c_cmp_pos_v, c_cmp_k_w1, c_cmp_k_w2, c_cmp_v_w1, c_cmp_v_w2, c_w_out, norm_mix_pre, norm_mix_post, norm_mlp_pre, norm_mlp_post, w_up, w_down):
    batch = x.shape[0]
    depth = norm_mix_pre.shape[0]
    outs = []
    for b in range(batch):
        xb = x[b]
        for layer in range(depth):
            i = layer // 2
            if layer % 2 == 0:
                xb = _layer_ab(xb, rel_bias, ab_w_in[i], ab_conv_w[i], ab_conv_b[i], ab_w_rg[i], ab_b_rg[i],
                               ab_w_ig[i], ab_b_ig[i], ab_lru_lambda[i], ab_w_out[i],
                               norm_mix_pre[layer], norm_mix_post[layer])
            else:
                xb = _layer_c(xb, rel_bias, c_w_in[i], c_cmp_pos_k[i], c_cmp_pos_v[i], c_cmp_k_w1[i],
                              c_cmp_k_w2[i], c_cmp_v_w1[i], c_cmp_v_w2[i], c_w_out[i],
                              norm_mix_pre[layer], norm_mix_post[layer])
            xb = _mlp(xb, norm_mlp_pre[layer], w_up, w_down, norm_mlp_post[layer], layer)
        outs.append(xb)
    return jnp.stack(outs)
```

```python
import functools
import math

import numpy as np
import jax
import jax.numpy as jnp
from jax import lax
from jax.experimental import pallas as pl
from jax.experimental.pallas import tpu as pltpu

F32 = jnp.float32
BF16 = jnp.bfloat16
HIGHEST = lax.Precision.HIGHEST
LOG2E = 1.4426950408889634

D = 1024
EPS = 1e-6
NEG = -1e30
N_BUCKETS = 32
MAX_DISTANCE = 128

HA = 8
DHA = 64
MOBA_BLOCK = 256
MOBA_TOPK = 3
LRU_W = 512
LRU_C = 8.0
CONV_W = 4
HC = 8
DHC = 128
GC = 2
PC = 4
CMP_LEN = 32
CMP_STRIDE = 16
SEL_BLOCK = 64
SEL_TOPK = 16
WINDOW = 512
SEL_FORCE = 1e4
D_FF = 4096

TILE = 256
MOBA_UNROLL = 2
ROUTE_TILES = 2
SEL_UNROLL = 2
SEL_CHUNKS = 2
CMP_BIAS_LANES = 32
CMP_MAX_FLOOR = -1e20
CMP_WIDTH_CLASS = 256
BRANCH_DTYPE = BF16
VMEM_LIMIT = 56 * 1024 * 1024
VMEM_LIMIT_RESIDENT = 60 * 1024 * 1024


def _bucket_np(d):
    n = np.maximum(d, 0)
    exact = N_BUCKETS // 2
    nf = np.maximum(n, 1).astype(np.float32)
    large = exact + (np.log(nf / np.float32(exact)) / np.float32(math.log(MAX_DISTANCE / exact))
                     * np.float32(N_BUCKETS - exact)).astype(np.int32)
    return np.where(n < exact, n, np.minimum(large, N_BUCKETS - 1))


_THR = [int(np.argmax(_bucket_np(np.arange(0, 4 * MAX_DISTANCE)) >= k)) for k in range(1, N_BUCKETS)]
assert _THR[-1] <= MAX_DISTANCE


def _bias_delta(d, tab_ref, h):
    acc = jnp.full(d.shape, tab_ref[0, h] - tab_ref[N_BUCKETS - 1, h], F32)
    for k in range(1, N_BUCKETS):
        acc = acc + jnp.where(d >= _THR[k - 1], tab_ref[k, h] - tab_ref[k - 1, h], 0.0)
    return acc


def _dot_t(a, b, **kw):
    return lax.dot_general(a, b, (((1,), (1,)), ((), ())), preferred_element_type=F32, **kw)


def _gelu_tanh(x):
    return 0.5 * x * (1.0 + jnp.tanh(0.7978845608028654 * (x + 0.044715 * x * x * x)))


def _sigmoid(x):
    return 0.5 * jnp.tanh(0.5 * x) + 0.5


def _params(*sem, vmem_limit=VMEM_LIMIT):
    return pltpu.CompilerParams(dimension_semantics=sem, vmem_limit_bytes=vmem_limit)


def _smem():
    return pl.BlockSpec(memory_space=pltpu.SMEM)


def _proj_kernel(x_ref, g_ref, w_ref, *out_refs, segs):
    x = x_ref[...]
    ms = jnp.mean(x * x, axis=-1, keepdims=True)
    hn = (x * lax.rsqrt(ms + EPS) * g_ref[...]).astype(BF16)
    for o_ref, (start, width) in zip(out_refs, segs):
        o_ref[...] = jnp.dot(hn, w_ref[:, start:start + width], preferred_element_type=F32).astype(o_ref.dtype)


def _rms_proj(x, g, w, segs, dtypes, tm=1024):
    t = x.shape[0]
    return pl.pallas_call(
        functools.partial(_proj_kernel, segs=segs),
        grid=(t // tm,),
        in_specs=[pl.BlockSpec((tm, D), lambda i: (i, 0)),
                  pl.BlockSpec((1, D), lambda i: (0, 0)),
                  pl.BlockSpec(w.shape, lambda i: (0, 0))],
        out_specs=[pl.BlockSpec((tm, wd), lambda i: (i, 0)) for (_, wd) in segs],
        out_shape=[jax.ShapeDtypeStruct((t, wd), dt) for (_, wd), dt in zip(segs, dtypes)],
        compiler_params=_params("arbitrary"),
        name="rms_proj",
    )(x, g.reshape(1, D), w)


def _segments(widths):
    segs, off = [], 0
    for wd in widths:
        segs.append((off, wd))
        off += wd
    return segs


def _topk_mask(r, lane, k):
    sel = jnp.zeros(r.shape, jnp.bool_)
    lanef = lane.astype(F32)
    for _ in range(k):
        mx = jnp.max(r, axis=1, keepdims=True)
        first = jnp.min(jnp.where(r == mx, lanef, 1e9), axis=1, keepdims=True)
        pick = lanef == first
        sel = sel | (pick & (mx > NEG / 2))
        r = jnp.where(pick, NEG, r)
    return sel


def _removed_topk(r0, k):
    r = r0
    for _ in range(k):
        r = jnp.where(r == jnp.max(r, axis=1, keepdims=True), NEG, r)
    return (r0 > NEG / 2) & (r < NEG / 2)


def _near_bias_tiles(f_ref, tab_ref, heads):
    d0 = (lax.broadcasted_iota(jnp.int32, (TILE, TILE), 0)
          - lax.broadcasted_iota(jnp.int32, (TILE, TILE), 1))
    for i, h in enumerate(heads):
        f_ref[i, 0] = jnp.where(d0 >= 0, LOG2E * _bias_delta(jnp.maximum(d0, 0), tab_ref, h), NEG)
        f_ref[i, 1] = LOG2E * _bias_delta(d0 + TILE, tab_ref, h)


def _slot_bias(j, qi, lo, f_diag, f_prev):
    other = jnp.where((j >= lo) & (j <= qi), 0.0, NEG)
    return jnp.where(j == qi, f_diag, jnp.where(j == qi - 1, f_prev, other))


def _flash_update(s, vt, m_ref, acc_ref, first):
    rm = jnp.max(s, axis=1, keepdims=True)
    if first:
        m_ref[...] = jnp.broadcast_to(rm, m_ref.shape)
        p = jnp.exp2(s - rm)
        acc_ref[...] = jnp.dot(p.astype(BF16), vt, preferred_element_type=F32)
        return
    m_old = m_ref[...]
    m_new = jnp.maximum(m_old, rm)
    alpha = jnp.exp2(m_old - m_new)
    p = jnp.exp2(s - jnp.concatenate([m_new] * (s.shape[1] // m_new.shape[1]), axis=1))
    pv = jnp.dot(p.astype(BF16), vt, preferred_element_type=F32)
    acc_ref[...] = jnp.concatenate([alpha] * (pv.shape[1] // alpha.shape[1]), axis=1) * acc_ref[...] + pv
    m_ref[...] = m_new


def _moba_route_kernel(q_ref, k_ref, qa_ref, km_ref, *, tiles):
    scale = DHA ** -0.5 * LOG2E
    lane = lax.broadcasted_iota(jnp.int32, (TILE, 128), 1)

    @pl.when(pl.program_id(0) == 0)
    def _():
        km_ref[...] = jnp.zeros_like(km_ref)

    def augmented_query(q, head_lanes, in_aug, blk, sel, qi):
        keep = sel | (in_aug & (blk == qi))
        return jnp.where(head_lanes, q * scale, jnp.where(keep, 0.0, NEG)).astype(BF16)

    fallbacks = []
    for sub in range(tiles):
        qi = pl.program_id(0) * tiles + sub
        rows = slice(sub * TILE, (sub + 1) * TILE)
        for hh in range(HA):
            pair = slice(hh // 2 * 128, (hh // 2 + 1) * 128)
            q = q_ref[rows, pair]
            head_lanes = (lane < DHA) if hh % 2 == 0 else (lane >= DHA)
            qh = jnp.where(head_lanes, q, 0.0)
            route = _dot_t(qh, km_ref[:, pair], precision=HIGHEST)
            blk = lane - DHA if hh % 2 == 0 else lane
            in_aug = jnp.logical_not(head_lanes)
            cand = in_aug & (blk < qi)
            r = jnp.where(cand, route, NEG)
            sel = _removed_topk(r, MOBA_TOPK)
            qa_ref[rows, hh * 128:(hh + 1) * 128] = augmented_query(q, head_lanes, in_aug, blk, sel, qi)
            count = jnp.sum(jnp.where(sel, 1.0, 0.0), axis=1, keepdims=True)
            want = jnp.minimum(qi, MOBA_TOPK).astype(F32)
            tied = jnp.max(jnp.where(count != want, 1.0, 0.0))
            fallbacks.append((rows, hh, tied, (q, head_lanes, in_aug, blk), r, qi))

        mean = jnp.mean(k_ref[rows, :], axis=0, keepdims=True)
        km_ref[pl.ds(qi, 1), :] = mean
        km_ref[pl.ds(qi + DHA, 1), :] = mean

    for rows, hh, tied, args, r, qi in fallbacks:
        @pl.when(tied > 0.0)
        def _(rows=rows, hh=hh, args=args, r=r, qi=qi):
            qa_ref[rows, hh * 128:(hh + 1) * 128] = augmented_query(*args, _topk_mask(r, lane, MOBA_TOPK), qi)


def _moba_route(q, k):
    t = q.shape[0]
    assert t // MOBA_BLOCK <= DHA and MOBA_BLOCK == TILE
    tiles = min(ROUTE_TILES, t // TILE)
    rows = lambda w: pl.BlockSpec((tiles * TILE, w), lambda i: (i, 0))
    return pl.pallas_call(
        functools.partial(_moba_route_kernel, tiles=tiles),
        grid=(t // (tiles * TILE),),
        in_specs=[rows(HA * DHA), rows(HA * DHA)],
        out_specs=rows(HA * 128),
        out_shape=jax.ShapeDtypeStruct((t, HA * 128), BF16),
        scratch_shapes=[pltpu.VMEM((2 * DHA, HA * DHA), F32)],
        compiler_params=_params("arbitrary"),
        name="moba_route",
    )(q, k)


def _moba_kernel(tab_ref, qa_ref, k_ref, v_ref, o_ref, f_ref, m_ref, acc_ref, s_ref, *, nb, unroll):
    qi = pl.program_id(0)
    lane = lax.broadcasted_iota(jnp.int32, (TILE, 128), 1)

    @pl.when(qi == 0)
    def _():
        _near_bias_tiles(f_ref, tab_ref, list(range(HA)))

    def own_lanes(hh, rows):
        ln = lax.broadcasted_iota(jnp.int32, (rows, 128), 1)
        return (ln < DHA) if hh % 2 == 0 else (ln >= DHA)

    def keys(j, hh, n):
        pair = slice(hh // 2 * 128, (hh // 2 + 1) * 128)
        parts = []
        for u in range(n):
            off = pl.multiple_of((j + u) * TILE, TILE)
            hot = jnp.where(lane == (j + u) + (DHA if hh % 2 == 0 else 0), 1.0, 0.0).astype(BF16)
            parts.append(jnp.where(own_lanes(hh, TILE), k_ref[pl.ds(off, TILE), pair], hot))
        return jnp.concatenate(parts, axis=0)

    def values(j, hh, n):
        pair = slice(hh // 2 * 128, (hh // 2 + 1) * 128)
        off = pl.multiple_of(j * TILE, TILE)
        ln = lax.broadcasted_iota(jnp.int32, (n * TILE, 128), 1)
        ones_col = jnp.where(ln == (DHA if hh % 2 == 0 else 0), 1.0, 0.0).astype(BF16)
        return jnp.where(own_lanes(hh, n * TILE), v_ref[pl.ds(off, n * TILE), pair], ones_col)

    def scores(j, hh, n):
        return _dot_t(qa_ref[:, hh * 128:(hh + 1) * 128], keys(j, hh, n))

    far_tiles = unroll * (jnp.maximum(qi - 1, 0) // unroll)
    groups = far_tiles // unroll
    last = jnp.maximum(groups - 1, 0)

    def produce(buf, grp):
        for hh in range(HA):
            s_ref[buf, hh] = scores(grp * unroll, hh, unroll)

    def consume(buf, grp):
        for hh in range(HA):
            _flash_update(s_ref[buf, hh], values(grp * unroll, hh, unroll), m_ref.at[hh], acc_ref.at[hh], False)

    produce(0, 0)
    start = jnp.minimum(far_tiles, nb - (unroll + 1))
    for hh in range(HA):
        bias = jnp.concatenate([_slot_bias(start + u, qi, far_tiles, f_ref[hh, 0], f_ref[hh, 1])
                                for u in range(unroll + 1)], axis=1)
        _flash_update(scores(start, hh, unroll + 1) + bias, values(start, hh, unroll + 1),
                      m_ref.at[hh], acc_ref.at[hh], True)

    def pair(k, c):
        produce(1, 2 * k + 1)
        consume(0, 2 * k)
        produce(0, jnp.minimum(2 * k + 2, last))
        consume(1, 2 * k + 1)
        return c

    lax.fori_loop(0, groups // 2, pair, 0)

    @pl.when(groups % 2 == 1)
    def _():
        consume(0, groups - 1)

    for pp in range(HA // 2):
        even, odd = acc_ref[2 * pp], acc_ref[2 * pp + 1]
        o_ref[:, pp * 128:(pp + 1) * 128] = jnp.where(lane < DHA, even * (1.0 / even[:, DHA:DHA + 1]),
                                                      odd * (1.0 / odd[:, 0:1])).astype(o_ref.dtype)


def _moba(tab, qa, k, v):
    t = qa.shape[0]
    once = pl.Buffered(1)
    unroll = min(MOBA_UNROLL, t // TILE - 1)
    return pl.pallas_call(
        functools.partial(_moba_kernel, nb=t // TILE, unroll=unroll),
        grid=(t // TILE,),
        in_specs=[_smem(),
                  pl.BlockSpec((TILE, HA * 128), lambda qi: (qi, 0)),
                  pl.BlockSpec((t, HA * DHA), lambda qi: (0, 0), pipeline_mode=once),
                  pl.BlockSpec((t, HA * DHA), lambda qi: (0, 0), pipeline_mode=once)],
        out_specs=pl.BlockSpec((TILE, HA * DHA), lambda qi: (qi, 0)),
        out_shape=jax.ShapeDtypeStruct((t, HA * DHA), BF16),
        scratch_shapes=[pltpu.VMEM((HA, 2, TILE, TILE), F32),
                        pltpu.VMEM((HA, TILE, 128), F32),
                        pltpu.VMEM((HA, TILE, 128), F32),
                        pltpu.VMEM((2, HA, TILE, unroll * TILE), F32)],
        compiler_params=_params("arbitrary", vmem_limit=VMEM_LIMIT_RESIDENT),
        name="moba_attention",
    )(tab, qa, k, v)


def _rglru_kernel(xr_ref, gr_ref, cw_ref, cb_ref, wrg_ref, brg_ref, wig_ref, big_ref, lam_ref,
                  o_ref, xbuf, a_s, b_s, hcar, *, tt):
    @pl.when(pl.program_id(0) == 0)
    def _():
        xbuf[0:8, :] = jnp.zeros((8, LRU_W), F32)
        hcar[...] = jnp.zeros_like(hcar)

    xbuf[8:8 + tt, :] = xr_ref[...]
    xc = cb_ref[...]
    for j in range(CONV_W):
        xc = xc + xbuf[8 - (CONV_W - 1) + j:8 - (CONV_W - 1) + j + tt, :] * cw_ref[j:j + 1, :]
    xbuf[0:8, :] = xbuf[tt:tt + 8, :]

    xcb = xc.astype(BF16)
    r = _sigmoid(jnp.dot(xcb, wrg_ref[...], preferred_element_type=F32) + brg_ref[...])
    ig = _sigmoid(jnp.dot(xcb, wig_ref[...], preferred_element_type=F32) + big_ref[...])
    z = -lam_ref[...]
    softplus = jnp.maximum(z, 0.0) + jnp.log(1.0 + jnp.exp(-jnp.abs(z)))
    a = jnp.exp(-LRU_C * r * softplus)
    a_s[...] = a
    b_s[...] = jnp.sqrt(1.0 - a * a) * ig * xc

    row = lax.broadcasted_iota(jnp.int32, (8, LRU_W), 0)

    def body(gi, h):
        off = pl.multiple_of(gi * 8, 8)
        a8 = a_s[pl.ds(off, 8), :]
        b8 = b_s[pl.ds(off, 8), :]
        for s in (1, 2, 4):
            a_sh = jnp.where(row >= s, pltpu.roll(a8, s, 0), 1.0)
            b_sh = jnp.where(row >= s, pltpu.roll(b8, s, 0), 0.0)
            b8 = a8 * b_sh + b8
            a8 = a8 * a_sh
        h8 = b8 + a8 * h
        b_s[pl.ds(off, 8), :] = h8
        return h8[7:8, :]

    hcar[0:1, :] = lax.fori_loop(0, tt // 8, body, hcar[0:1, :])
    o_ref[...] = (b_s[...] * _gelu_tanh(gr_ref[...])).astype(o_ref.dtype)


def _rglru(xr, gr, conv_w, conv_b, wrg, b_rg, wig, b_ig, lam, tt=512):
    t = xr.shape[0]
    row = lambda a: a.reshape(1, LRU_W)
    full = lambda shape: pl.BlockSpec(shape, lambda i: (0, 0))
    return pl.pallas_call(
        functools.partial(_rglru_kernel, tt=tt),
        grid=(t // tt,),
        in_specs=[pl.BlockSpec((tt, LRU_W), lambda i: (i, 0)),
                  pl.BlockSpec((tt, LRU_W), lambda i: (i, 0)),
                  full((CONV_W, LRU_W)), full((1, LRU_W)),
                  full((LRU_W, LRU_W)), full((1, LRU_W)),
                  full((LRU_W, LRU_W)), full((1, LRU_W)), full((1, LRU_W))],
        out_specs=pl.BlockSpec((tt, LRU_W), lambda i: (i, 0)),
        out_shape=jax.ShapeDtypeStruct((t, LRU_W), BF16),
        scratch_shapes=[pltpu.VMEM((tt + 8, LRU_W), F32),
                        pltpu.VMEM((tt, LRU_W), F32),
                        pltpu.VMEM((tt, LRU_W), F32),
                        pltpu.VMEM((8, LRU_W), F32)],
        compiler_params=_params("arbitrary"),
        name="rglru",
    )(xr, gr, conv_w, row(conv_b), wrg, row(b_rg), wig, row(b_ig), row(lam))


def _post(y, g_ref, x_ref, o_ref):
    ms = jnp.mean(y * y, axis=-1, keepdims=True)
    o_ref[...] = x_ref[...] + y * lax.rsqrt(ms + EPS) * g_ref[...]


def _outproj_ab_kernel(a1_ref, a2_ref, w1_ref, w2_ref, g_ref, x_ref, o_ref):
    y = (jnp.dot(a1_ref[...], w1_ref[...], preferred_element_type=F32)
         + jnp.dot(a2_ref[...], w2_ref[...], preferred_element_type=F32))
    _post(y, g_ref, x_ref, o_ref)


def _outproj_ab(a1, a2, w1, w2, g, x, tm=1024):
    t = x.shape[0]
    rows = lambda w: pl.BlockSpec((tm, w), lambda i: (i, 0))
    full = lambda shape: pl.BlockSpec(shape, lambda i: (0, 0))
    return pl.pallas_call(
        _outproj_ab_kernel,
        grid=(t // tm,),
        in_specs=[rows(a1.shape[1]), rows(a2.shape[1]), full(w1.shape), full(w2.shape),
                  full((1, D)), rows(D)],
        out_specs=rows(D),
        out_shape=jax.ShapeDtypeStruct((t, D), F32),
        compiler_params=_params("arbitrary"),
        name="outproj_ab",
    )(a1, a2, w1, w2, g.reshape(1, D), x)


def _outproj_c_kernel(oc_ref, os_ref, ow_ref, gl_ref, w_ref, g_ref, x_ref, o_ref):
    gates = jax.nn.sigmoid(gl_ref[...])
    parts = []
    for h in range(HC):
        sl = slice(h * DHC, (h + 1) * DHC)
        parts.append(gates[:, 3 * h:3 * h + 1] * oc_ref[:, sl]
                     + gates[:, 3 * h + 1:3 * h + 2] * os_ref[:, sl]
                     + gates[:, 3 * h + 2:3 * h + 3] * ow_ref[:, sl])
    o = jnp.concatenate(parts, axis=1).astype(BF16)
    _post(jnp.dot(o, w_ref[...], preferred_element_type=F32), g_ref, x_ref, o_ref)


def _outproj_c(oc, os_, ow, gl, w, g, x, tm=1024):
    t = x.shape[0]
    rows = lambda w_: pl.BlockSpec((tm, w_), lambda i: (i, 0))
    full = lambda shape: pl.BlockSpec(shape, lambda i: (0, 0))
    return pl.pallas_call(
        _outproj_c_kernel,
        grid=(t // tm,),
        in_specs=[rows(D), rows(D), rows(D), rows(128), full(w.shape), full((1, D)), rows(D)],
        out_specs=rows(D),
        out_shape=jax.ShapeDtypeStruct((t, D), F32),
        compiler_params=_params("arbitrary"),
        name="outproj_c",
    )(oc, os_, ow, gl, w, g.reshape(1, D), x)


def _mlp_kernel(x_ref, gpre_ref, wup_ref, wdn_ref, gpost_ref, o_ref, h_s, acc_s):
    c = pl.program_id(1)

    @pl.when(c == 0)
    def _():
        x = x_ref[...]
        ms = jnp.mean(x * x, axis=-1, keepdims=True)
        h_s[...] = (x * lax.rsqrt(ms + EPS) * gpre_ref[...]).astype(BF16)
        acc_s[...] = jnp.zeros_like(acc_s)

    u = jnp.maximum(jnp.dot(h_s[...], wup_ref[...].astype(BF16), preferred_element_type=F32), 0.0)
    acc_s[...] += jnp.dot((u * u).astype(BF16), wdn_ref[...].astype(BF16), preferred_element_type=F32)

    @pl.when(c == pl.num_programs(1) - 1)
    def _():
        _post(acc_s[...], gpost_ref, x_ref, o_ref)


def _mlp(x, gpre, wup, wdn, gpost, layer, tm=1024, tf=1024):
    t = x.shape[0]
    return pl.pallas_call(
        _mlp_kernel,
        grid=(t // tm, D_FF // tf),
        in_specs=[pl.BlockSpec((tm, D), lambda i, c: (i, 0)),
                  pl.BlockSpec((1, D), lambda i, c: (0, 0)),
                  pl.BlockSpec((None, D, tf), lambda i, c: (layer, 0, c)),
                  pl.BlockSpec((None, tf, D), lambda i, c: (layer, c, 0)),
                  pl.BlockSpec((1, D), lambda i, c: (0, 0))],
        out_specs=pl.BlockSpec((tm, D), lambda i, c: (i, 0)),
        out_shape=jax.ShapeDtypeStruct((t, D), F32),
        scratch_shapes=[pltpu.VMEM((tm, D), BF16), pltpu.VMEM((tm, D), F32)],
        compiler_params=_params("arbitrary", "arbitrary"),
        name="mlp",
    )(x, gpre.reshape(1, D), wup, wdn, gpost.reshape(1, D))


def _compress_kernel(*refs):
    x_refs, (pos_ref, w1_ref, w2_ref, o_ref) = refs[:GC], refs[GC:]
    n = o_ref.shape[1]
    for g, x_ref in enumerate(x_refs):
        top = jnp.zeros((n, w1_ref.shape[1]), F32)
        bot = jnp.zeros((n, w1_ref.shape[1]), F32)
        for l in range(CMP_STRIDE):
            xl = x_ref[pl.ds(l, n, stride=CMP_STRIDE), :]
            lo, hi = l * DHC, (CMP_STRIDE + l) * DHC
            top = top + jnp.dot((xl + pos_ref[l:l + 1, :]).astype(BF16), w1_ref[lo:lo + DHC, :],
                                preferred_element_type=F32)
            bot = bot + jnp.dot((xl + pos_ref[CMP_STRIDE + l:CMP_STRIDE + l + 1, :]).astype(BF16),
                                w1_ref[hi:hi + DHC, :], preferred_element_type=F32)
        hid = top + pltpu.roll(bot, n - 1, 0)
        o_ref[g] = jnp.dot(_gelu_tanh(hid).astype(BF16), w2_ref[...],
                           preferred_element_type=F32).astype(o_ref.dtype)


def _compress(xs, pos, w1, w2):
    t = xs[0].shape[0]
    n = t // CMP_STRIDE
    full = lambda shape: pl.BlockSpec(shape, lambda i: (0,) * len(shape))
    return pl.pallas_call(
        _compress_kernel,
        grid=(1,),
        in_specs=[full((t, DHC))] * GC + [full(pos.shape), full(w1.shape), full(w2.shape)],
        out_specs=full((GC, n, DHC)),
        out_shape=jax.ShapeDtypeStruct((GC, n, DHC), BF16),
        compiler_params=_params("arbitrary"),
        name="nsa_compress",
    )(*xs, pos, w1, w2)


def _stack_heads(q):
    return jnp.concatenate([q[:, p * DHC:(p + 1) * DHC] for p in range(PC)], axis=0)


def _nsa_cmp_kernel(tab_ref, q_ref, kc_ref, vc_ref, ov_ref, *refs, tq, ns, nwb):
    kw_refs, vw_refs = refs[:nwb], refs[nwb:2 * nwb]
    oc_ref, sb_ref, ow_ref, e_ref, qb_ref, fw_ref, imp_ref = refs[2 * nwb:]
    qi = pl.program_id(0)
    ncp = kc_ref.shape[1]
    nsp = ov_ref.shape[1]
    scale = DHC ** -0.5 * LOG2E
    cpq = tq // CMP_STRIDE
    k = min(SEL_TOPK, ns)

    @pl.when(qi == 0)
    def _():
        u = lax.broadcasted_iota(jnp.int32, (2 * ncp, 128), 0)
        lane = lax.broadcasted_iota(jnp.int32, (2 * ncp, 128), 1)
        hit = (lane < 2 * CMP_BIAS_LANES) & (u + lane % CMP_BIAS_LANES == ncp - 1)
        after = (lane == 2 * CMP_BIAS_LANES) & (u >= ncp)
        e_ref[...] = (jnp.where(hit, 1.0, 0.0) + jnp.where(after, 1.0, 0.0)).astype(BF16)
        i = lax.broadcasted_iota(jnp.int32, (tq, 128), 0)
        lane = lax.broadcasted_iota(jnp.int32, (tq, 128), 1)
        d = i + CMP_STRIDE * (lane % CMP_BIAS_LANES) - (tq - CMP_STRIDE + CMP_LEN - 1)
        dw = (WINDOW + lax.broadcasted_iota(jnp.int32, (tq, nwb * tq), 0)
              - lax.broadcasted_iota(jnp.int32, (tq, nwb * tq), 1))
        for h in range(HC):
            b = jnp.where(d >= 0, LOG2E * _bias_delta(jnp.maximum(d, 0), tab_ref, h), NEG)
            hi = b.astype(BF16).astype(F32)
            qb = jnp.where(lane < CMP_BIAS_LANES, hi, jnp.where(lane < 2 * CMP_BIAS_LANES, b - hi, 0.0))
            qb_ref[h] = jnp.where(lane == 2 * CMP_BIAS_LANES, NEG, qb).astype(BF16)
            fw_ref[h] = jnp.where((dw >= 0) & (dw < WINDOW),
                                  LOG2E * _bias_delta(jnp.maximum(dw, 0), tab_ref, h), NEG)

    off = pl.multiple_of(ncp - cpq * (qi + 1), 16)
    jj = lax.broadcasted_iota(jnp.int32, (tq, nsp), 1)
    cur = (qi * tq + lax.broadcasted_iota(jnp.int32, (tq, nsp), 0)) // SEL_BLOCK
    forced = (jj == 0) | (jj == cur) | (jj == cur - 1)
    dead = (jj > cur) | (jj >= ns)
    q = q_ref[...]
    qs = [(q[:, h * DHC:(h + 1) * DHC] * scale).astype(BF16) for h in range(HC)]

    def compressed(width):
        onehots = e_ref[pl.ds(off, width), :]
        for g in range(GC):
            heads = range(g * PC, (g + 1) * PC)
            qa = jnp.concatenate([jnp.concatenate([qs[h] for h in heads], axis=0),
                                  jnp.concatenate([qb_ref[h] for h in heads], axis=0)], axis=1)
            s_all = _dot_t(qa, jnp.concatenate([kc_ref[g, 0:width, :], onehots], axis=1))
            psum = jnp.zeros((tq, width), F32)
            for p, h in enumerate(heads):
                s = s_all[p * tq:(p + 1) * tq]
                m = jnp.maximum(jnp.max(s, axis=1, keepdims=True), CMP_MAX_FLOOR)
                e = jnp.exp2(s - m)
                pc = e * (1.0 / jnp.maximum(jnp.sum(e, axis=1, keepdims=True), 1e-30))
                oc_ref[:, h * DHC:(h + 1) * DHC] = jnp.dot(pc.astype(BF16), vc_ref[g, 0:width, :],
                                                           preferred_element_type=F32).astype(oc_ref.dtype)
                psum = psum + pc
            hi = psum.astype(BF16)
            lo = (psum - hi.astype(F32)).astype(BF16)
            imp_ref[g] = (jnp.dot(hi, ov_ref[0:width, :], preferred_element_type=F32)
                          + jnp.dot(lo, ov_ref[0:width, :], preferred_element_type=F32))

    cw = min(CMP_WIDTH_CLASS, ncp)
    cls = (cpq * (qi + 1) - 1) // cw
    for c in range(ncp // cw):
        @pl.when(cls == c)
        def _(c=c):
            compressed((c + 1) * cw)

    fallbacks = []
    for g in range(GC):
        imp = imp_ref[g]
        rest = jnp.where(forced | dead, NEG, imp)
        sel_rest = _removed_topk(rest, k - 3)
        sel = sel_rest | (forced & jnp.logical_not(dead))
        sb_ref[g] = jnp.where(sel, 0.0, NEG).astype(BF16)
        count = jnp.sum(jnp.where(sel_rest, 1.0, 0.0), axis=1, keepdims=True)
        cand = jnp.sum(jnp.where(forced | dead, 0.0, 1.0), axis=1, keepdims=True)
        tied = jnp.max(jnp.where(count != jnp.minimum(cand, float(k - 3)), 1.0, 0.0))
        fallbacks.append((g, imp, tied))

    wk = jnp.concatenate([r[...] for r in kw_refs], axis=0)
    wv = jnp.concatenate([r[...] for r in vw_refs], axis=0)
    blk = lax.broadcasted_iota(jnp.int32, (nwb * tq, DHC), 0) // tq
    flags = jnp.where(blk + qi < nwb - 1, 1.0, 0.0).astype(BF16)
    off_lane = jnp.where(lax.broadcasted_iota(jnp.int32, (tq, DHC), 1) == 0, NEG, 0.0).astype(BF16)
    for h in range(HC):
        g = h // PC
        cols = slice(g * DHC, (g + 1) * DHC)
        s = _dot_t(jnp.concatenate([qs[h], off_lane], axis=1),
                   jnp.concatenate([wk[:, cols], flags], axis=1)) + fw_ref[h]
        e = jnp.exp2(s - jnp.max(s, axis=1, keepdims=True))
        o = jnp.dot(e.astype(BF16), wv[:, cols], preferred_element_type=F32)
        ow_ref[:, h * DHC:(h + 1) * DHC] = (o * (1.0 / jnp.sum(e, axis=1, keepdims=True))).astype(ow_ref.dtype)

    for g, imp, tied in fallbacks:
        @pl.when(tied > 0.0)
        def _(g=g, imp=imp):
            full = jnp.where(dead, NEG, jnp.where(forced, imp + SEL_FORCE, imp))
            sb_ref[g] = jnp.where(_topk_mask(full, jj, k), 0.0, NEG).astype(BF16)


def _nsa_cmp_win(tab, q, kc, vc, overlap, kw, vw, ns, tq=TILE):
    t = q.shape[0]
    ncp = kc.shape[1]
    nsp = overlap.shape[1]
    assert (tq - CMP_STRIDE + CMP_LEN - 1 + MAX_DISTANCE) // CMP_STRIDE < CMP_BIAS_LANES
    assert WINDOW % tq == 0 and min(SEL_TOPK, ns) > 3
    nwb = WINDOW // tq + 1
    full = lambda shape: pl.BlockSpec(shape, lambda qi: (0,) * len(shape))
    rows = pl.BlockSpec((tq, HC * DHC), lambda qi: (qi, 0))
    back = [pl.BlockSpec((tq, GC * DHC), functools.partial(lambda qi, b: (jnp.maximum(qi - b, 0), 0), b=b))
            for b in range(nwb - 1, -1, -1)]
    return pl.pallas_call(
        functools.partial(_nsa_cmp_kernel, tq=tq, ns=ns, nwb=nwb),
        grid=(t // tq,),
        in_specs=[_smem(), rows, full((GC, ncp, DHC)), full((GC, ncp, DHC)), full((ncp, nsp))] + back + back,
        out_specs=[rows, pl.BlockSpec((GC, tq, nsp), lambda qi: (0, qi, 0)), rows],
        out_shape=[jax.ShapeDtypeStruct((t, HC * DHC), BRANCH_DTYPE),
                   jax.ShapeDtypeStruct((GC, t, nsp), BF16),
                   jax.ShapeDtypeStruct((t, HC * DHC), BRANCH_DTYPE)],
        scratch_shapes=[pltpu.VMEM((2 * ncp, 128), BF16), pltpu.VMEM((HC, tq, 128), BF16),
                        pltpu.VMEM((HC, tq, nwb * tq), F32), pltpu.VMEM((GC, tq, nsp), F32)],
        compiler_params=_params("arbitrary"),
        name="nsa_compressed_window",
    )(tab, q, kc, vc, overlap, *([kw] * nwb), *([vw] * nwb))


def _nsa_sel_kernel(tab_ref, q_ref, sb_ref, k_ref, v_ref, o_ref, f_ref, m_ref, acc_ref, s_ref, hot_ref, *, nhalf,
                    nb, unroll):
    qi = pl.program_id(0)
    scale = DHC ** -0.5 * LOG2E
    tiles_per_half = 128 * SEL_BLOCK // TILE

    @pl.when(qi == 0)
    def _():
        _near_bias_tiles(f_ref, tab_ref, list(range(HC)))
        row = lax.broadcasted_iota(jnp.int32, hot_ref.shape, 0)
        lane = lax.broadcasted_iota(jnp.int32, hot_ref.shape, 1)
        hot_ref[...] = jnp.where(lane == (row // SEL_BLOCK) % 128, 1.0, 0.0).astype(BF16)

    hpc = PC // SEL_CHUNKS
    chunks = [(g, list(range(g * PC + c * hpc, g * PC + (c + 1) * hpc)))
              for g in range(GC) for c in range(SEL_CHUNKS)]
    q = q_ref[...]
    qas = []
    for g, heads in chunks:
        qc = jnp.concatenate([q[:, h * DHC:(h + 1) * DHC] for h in heads], axis=0)
        qc = (qc * scale).astype(BF16)
        sb = sb_ref[g]
        qas.append([jnp.concatenate([qc, jnp.concatenate([sb[:, hf * 128:(hf + 1) * 128]] * hpc, axis=0)], axis=1)
                    for hf in range(nhalf)])

    def qa_for(c, j):
        if nhalf == 1:
            return qas[c][0]
        return jnp.where(j >= tiles_per_half, qas[c][1], qas[c][0])

    def block_onehot(off, n):
        return hot_ref[pl.ds(off, n), :]

    def keys(g, off, n, hot):
        return jnp.concatenate([k_ref[pl.ds(off, n), g * DHC:(g + 1) * DHC], hot], axis=1)

    def values(g, off, n):
        ones_col = jnp.where(lax.broadcasted_iota(jnp.int32, (n, DHC), 1) == 0, 1.0, 0.0).astype(BF16)
        return jnp.concatenate([v_ref[pl.ds(off, n), g * DHC:(g + 1) * DHC], ones_col], axis=1)

    far_tiles = unroll * (jnp.maximum(qi - 1, 0) // unroll)
    groups = far_tiles // unroll
    last = jnp.maximum(groups - 1, 0)

    def group_onehot(grp):
        return block_onehot(pl.multiple_of(grp * (unroll * TILE), unroll * TILE), unroll * TILE)

    def produce(buf, grp, hot, only=None):
        off = pl.multiple_of(grp * (unroll * TILE), unroll * TILE)
        for c, (g, _) in enumerate(chunks):
            if only is None or c == only:
                s_ref[buf, c] = _dot_t(qa_for(c, grp * unroll), keys(g, off, unroll * TILE, hot))

    def consume(buf, grp, only=None):
        off = pl.multiple_of(grp * (unroll * TILE), unroll * TILE)
        for c, (g, _) in enumerate(chunks):
            if only is None or c == only:
                _flash_update(s_ref[buf, c], values(g, off, unroll * TILE), m_ref.at[c], acc_ref.at[c], False)

    produce(0, 0, group_onehot(0))
    start = jnp.minimum(far_tiles, nb - (unroll + 1))
    slot_hots = [block_onehot(pl.multiple_of((start + u) * TILE, TILE), TILE) for u in range(unroll + 1)]
    for c, (g, heads) in enumerate(chunks):
        cols = []
        for u in range(unroll + 1):
            j = start + u
            s = _dot_t(qa_for(c, j), keys(g, pl.multiple_of(j * TILE, TILE), TILE, slot_hots[u]))
            cols.append(jnp.concatenate(
                [s[i * TILE:(i + 1) * TILE] + _slot_bias(j, qi, far_tiles, f_ref[h, 0], f_ref[h, 1])
                 for i, h in enumerate(heads)], axis=0))
        vt = values(g, pl.multiple_of(start * TILE, TILE), (unroll + 1) * TILE)
        _flash_update(jnp.concatenate(cols, axis=1), vt, m_ref.at[c], acc_ref.at[c], True)

    def pair(k, carry):
        nxt = 2 * k + 1
        hot = group_onehot(nxt)
        for c in range(len(chunks)):
            produce(1, nxt, hot, c)
            consume(0, 2 * k, c)
        nxt = jnp.minimum(2 * k + 2, last)
        hot = group_onehot(nxt)
        for c in range(len(chunks)):
            produce(0, nxt, hot, c)
            consume(1, 2 * k + 1, c)
        return carry

    lax.fori_loop(0, groups // 2, pair, 0)

    @pl.when(groups % 2 == 1)
    def _():
        consume(0, groups - 1)


    for c, (g, heads) in enumerate(chunks):
        acc = acc_ref[c]
        o = acc[:, :DHC] / acc[:, DHC:DHC + 1]
        for i, h in enumerate(heads):
            o_ref[:, h * DHC:(h + 1) * DHC] = o[i * TILE:(i + 1) * TILE].astype(o_ref.dtype)


def _nsa_sel(tab, q, selbias, k, v):
    t = q.shape[0]
    nsp = selbias.shape[2]
    nchunks = GC * SEL_CHUNKS
    rows = HC // nchunks * TILE
    once = pl.Buffered(1)
    unroll = min(SEL_UNROLL, t // TILE - 1)
    return pl.pallas_call(
        functools.partial(_nsa_sel_kernel, nhalf=nsp // 128, nb=t // TILE, unroll=unroll),
        grid=(t // TILE,),
        in_specs=[_smem(),
                  pl.BlockSpec((TILE, HC * DHC), lambda qi: (qi, 0)),
                  pl.BlockSpec((GC, TILE, nsp), lambda qi: (0, qi, 0)),
                  pl.BlockSpec((t, GC * DHC), lambda qi: (0, 0), pipeline_mode=once),
                  pl.BlockSpec((t, GC * DHC), lambda qi: (0, 0), pipeline_mode=once)],
        out_specs=pl.BlockSpec((TILE, HC * DHC), lambda qi: (qi, 0)),
        out_shape=jax.ShapeDtypeStruct((t, HC * DHC), BRANCH_DTYPE),
        scratch_shapes=[pltpu.VMEM((HC, 2, TILE, TILE), F32),
                        pltpu.VMEM((nchunks, rows, 128), F32),
                        pltpu.VMEM((nchunks, rows, 2 * DHC), F32),
                        pltpu.VMEM((2, nchunks, rows, unroll * TILE), F32),
                        pltpu.VMEM((t, DHC), BF16)],
        compiler_params=_params("arbitrary"),
        name="nsa_selected",
    )(tab, q, selbias, k, v)


def _layer_ab(x, tab, w_in, conv_w, conv_b, w_rg, b_rg, w_ig, b_ig, lam, w_out, g_pre, g_post):
    da = HA * DHA
    segs = [(0, da), (da, da), (da, da), (2 * da, da), (3 * da, LRU_W), (3 * da + LRU_W, LRU_W)]
    q, k32, k, v, xr, gr = _rms_proj(x, g_pre, w_in.astype(BF16), segs, [F32, F32, BF16, BF16, F32, F32])
    o_a = _moba(tab, _moba_route(q, k32), k, v)

    eye = jnp.eye(w_rg.shape[0], dtype=F32)
    blockdiag = lambda wb: jnp.einsum('bij,bc->bicj', wb, eye).reshape(LRU_W, LRU_W).astype(BF16)
    o_b = _rglru(xr, gr, conv_w, conv_b, blockdiag(w_rg), b_rg, blockdiag(w_ig), b_ig, lam)
    wo = w_out.astype(BF16)
    return _outproj_ab(o_a, o_b, wo[:da], wo[da:], g_post, x)


def _layer_c(x, tab, w_in, pos_k, pos_v, k_w1, k_w2, v_w1, v_w2, w_out, g_pre, g_post):
    t = x.shape[0]
    dq = HC * DHC
    ckv = GC * DHC
    w = jnp.pad(w_in, ((0, 0), (0, 128 - 3 * HC))).astype(BF16)
    segs = _segments([dq, DHC, DHC, DHC, DHC, ckv, ckv, ckv, ckv, 128])
    q, kc0, kc1, vc0, vc1, ks, vs, kw, vw, gl = _rms_proj(
        x, g_pre, w, segs, [F32, F32, F32, F32, F32, BF16, BF16, BF16, BF16, F32])

    n = t // CMP_STRIDE
    kc = _compress([kc0, kc1], pos_k, k_w1.astype(BF16), k_w2.astype(BF16))
    vc = _compress([vc0, vc1], pos_v, v_w1.astype(BF16), v_w2.astype(BF16))

    ns = t // SEL_BLOCK
    nsp = -(-ns // 128) * 128
    cs = jnp.arange(n)[:, None] * CMP_STRIDE
    ss = jnp.arange(nsp)[None, :] * SEL_BLOCK
    overlap = ((cs < ss + SEL_BLOCK) & (cs + CMP_LEN > ss) & (jnp.arange(n)[:, None] < n - 1)
               & (jnp.arange(nsp)[None, :] < ns)).astype(BF16)
    o_c, selbias, o_w = _nsa_cmp_win(tab, q, kc, vc, overlap, kw, vw, ns)
    o_s = _nsa_sel(tab, q, selbias, ks, vs)
    return _outproj_c(o_c, o_s, o_w, gl, w_out.astype(BF16), g_post, x)


def kernel(x, rel_bias, ab_w_in, ab_conv_w, ab_conv_b, ab_w_rg, ab_b_rg, ab_w_ig, ab_b_ig, ab_lru_lambda, ab_w_out, c_w_in, c_cmp_pos_k, c_cmp_pos_v, c_cmp_k_w1, c_cmp_k_w2, c_cmp_v_w1, c_cmp_v_w2, c_w_out, norm_mix_pre, norm_mix_post, norm_mlp_pre, norm_mlp_post, w_up, w_down):
    batch = x.shape[0]
    depth = norm_mix_pre.shape[0]
    outs = []
    for b in range(batch):
        xb = x[b]
        for layer in range(depth):
            i = layer // 2
            if layer % 2 == 0:
                xb = _layer_ab(xb, rel_bias, ab_w_in[i], ab_conv_w[i], ab_conv_b[i], ab_w_rg[i], ab_b_rg[i],
                               ab_w_ig[i], ab_b_ig[i], ab_lru_lambda[i], ab_w_out[i],
                               norm_mix_pre[layer], norm_mix_post[layer])
            else:
                xb = _layer_c(xb, rel_bias, c_w_in[i], c_cmp_pos_k[i], c_cmp_pos_v[i], c_cmp_k_w1[i],
                              c_cmp_k_w2[i], c_cmp_v_w1[i], c_cmp_v_w2[i], c_w_out[i],
                              norm_mix_pre[layer], norm_mix_post[layer])
            xb = _mlp(xb, norm_mlp_pre[layer], w_up, w_down, norm_mlp_post[layer], layer)
        outs.append(xb)
    return jnp.stack(outs)
```

```python
import functools
import math

import numpy as np
import jax
import jax.numpy as jnp
from jax import lax
from jax.experimental import pallas as pl
from jax.experimental.pallas import tpu as pltpu

F32 = jnp.float32
BF16 = jnp.bfloat16
HIGHEST = lax.Precision.HIGHEST
LOG2E = 1.4426950408889634

D = 1024
EPS = 1e-6
NEG = -1e30
N_BUCKETS = 32
MAX_DISTANCE = 128

HA = 8
DHA = 64
MOBA_BLOCK = 256
MOBA_TOPK = 3
LRU_W = 512
LRU_C = 8.0
CONV_W = 4
HC = 8
DHC = 128
GC = 2
PC = 4
CMP_LEN = 32
CMP_STRIDE = 16
SEL_BLOCK = 64
SEL_TOPK = 16
WINDOW = 512
SEL_FORCE = 1e4
D_FF = 4096

TILE = 256
MOBA_UNROLL = 2
ROUTE_TILES = 2
SEL_UNROLL = 2
SEL_CHUNKS = 2
CMP_BIAS_LANES = 32
CMP_MAX_FLOOR = -1e20
CMP_WIDTH_CLASS = 256
BRANCH_DTYPE = BF16
VMEM_LIMIT = 56 * 1024 * 1024
VMEM_LIMIT_RESIDENT = 60 * 1024 * 1024


def _bucket_np(d):
    n = np.maximum(d, 0)
    exact = N_BUCKETS // 2
    nf = np.maximum(n, 1).astype(np.float32)
    large = exact + (np.log(nf / np.float32(exact)) / np.float32(math.log(MAX_DISTANCE / exact))
                     * np.float32(N_BUCKETS - exact)).astype(np.int32)
    return np.where(n < exact, n, np.minimum(large, N_BUCKETS - 1))


_THR = [int(np.argmax(_bucket_np(np.arange(0, 4 * MAX_DISTANCE)) >= k)) for k in range(1, N_BUCKETS)]
assert _THR[-1] <= MAX_DISTANCE


def _bias_delta(d, tab_ref, h):
    acc = jnp.full(d.shape, tab_ref[0, h] - tab_ref[N_BUCKETS - 1, h], F32)
    for k in range(1, N_BUCKETS):
        acc = acc + jnp.where(d >= _THR[k - 1], tab_ref[k, h] - tab_ref[k - 1, h], 0.0)
    return acc


def _dot_t(a, b, **kw):
    return lax.dot_general(a, b, (((1,), (1,)), ((), ())), preferred_element_type=F32, **kw)


def _gelu_tanh(x):
    return 0.5 * x * (1.0 + jnp.tanh(0.7978845608028654 * (x + 0.044715 * x * x * x)))


def _sigmoid(x):
    return 0.5 * jnp.tanh(0.5 * x) + 0.5


def _params(*sem, vmem_limit=VMEM_LIMIT):
    return pltpu.CompilerParams(dimension_semantics=sem, vmem_limit_bytes=vmem_limit)


def _smem():
    return pl.BlockSpec(memory_space=pltpu.SMEM)


def _proj_kernel(x_ref, g_ref, w_ref, *out_refs, segs):
    x = x_ref[...]
    ms = jnp.mean(x * x, axis=-1, keepdims=True)
    hn = (x * lax.rsqrt(ms + EPS) * g_ref[...]).astype(BF16)
    for o_ref, (start, width) in zip(out_refs, segs):
        o_ref[...] = jnp.dot(hn, w_ref[:, start:start + width], preferred_element_type=F32).astype(o_ref.dtype)


def _rms_proj(x, g, w, segs, dtypes, tm=1024):
    t = x.shape[0]
    return pl.pallas_call(
        functools.partial(_proj_kernel, segs=segs),
        grid=(t // tm,),
        in_specs=[pl.BlockSpec((tm, D), lambda i: (i, 0)),
                  pl.BlockSpec((1, D), lambda i: (0, 0)),
                  pl.BlockSpec(w.shape, lambda i: (0, 0))],
        out_specs=[pl.BlockSpec((tm, wd), lambda i: (i, 0)) for (_, wd) in segs],
        out_shape=[jax.ShapeDtypeStruct((t, wd), dt) for (_, wd), dt in zip(segs, dtypes)],
        compiler_params=_params("arbitrary"),
        name="rms_proj",
    )(x, g.reshape(1, D), w)


def _segments(widths):
    segs, off = [], 0
    for wd in widths:
        segs.append((off, wd))
        off += wd
    return segs


def _topk_mask(r, lane, k):
    sel = jnp.zeros(r.shape, jnp.bool_)
    lanef = lane.astype(F32)
    for _ in range(k):
        mx = jnp.max(r, axis=1, keepdims=True)
        first = jnp.min(jnp.where(r == mx, lanef, 1e9), axis=1, keepdims=True)
        pick = lanef == first
        sel = sel | (pick & (mx > NEG / 2))
        r = jnp.where(pick, NEG, r)
    return sel


def _removed_topk(r0, k):
    r = r0
    for _ in range(k):
        r = jnp.where(r == jnp.max(r, axis=1, keepdims=True), NEG, r)
    return (r0 > NEG / 2) & (r < NEG / 2)


def _near_bias_tiles(f_ref, tab_ref, heads):
    d0 = (lax.broadcasted_iota(jnp.int32, (TILE, TILE), 0)
          - lax.broadcasted_iota(jnp.int32, (TILE, TILE), 1))
    for i, h in enumerate(heads):
        f_ref[i, 0] = jnp.where(d0 >= 0, LOG2E * _bias_delta(jnp.maximum(d0, 0), tab_ref, h), NEG)
        f_ref[i, 1] = LOG2E * _bias_delta(d0 + TILE, tab_ref, h)


def _slot_bias(j, qi, lo, f_diag, f_prev):
    other = jnp.where((j >= lo) & (j <= qi), 0.0, NEG)
    return jnp.where(j == qi, f_diag, jnp.where(j == qi - 1, f_prev, other))


def _flash_update(s, vt, m_ref, acc_ref, first):
    rm = jnp.max(s, axis=1, keepdims=True)
    if first:
        m_ref[...] = jnp.broadcast_to(rm, m_ref.shape)
        p = jnp.exp2(s - rm)
        acc_ref[...] = jnp.dot(p.astype(BF16), vt, preferred_element_type=F32)
        return
    m_old = m_ref[...]
    m_new = jnp.maximum(m_old, rm)
    alpha = jnp.exp2(m_old - m_new)
    p = jnp.exp2(s - jnp.concatenate([m_new] * (s.shape[1] // m_new.shape[1]), axis=1))
    pv = jnp.dot(p.astype(BF16), vt, preferred_element_type=F32)
    acc_ref[...] = jnp.concatenate([alpha] * (pv.shape[1] // alpha.shape[1]), axis=1) * acc_ref[...] + pv
    m_ref[...] = m_new


def _moba_route_kernel(q_ref, k_ref, qa_ref, km_ref, *, tiles):
    scale = DHA ** -0.5 * LOG2E
    lane = lax.broadcasted_iota(jnp.int32, (TILE, 128), 1)

    @pl.when(pl.program_id(0) == 0)
    def _():
        km_ref[...] = jnp.zeros_like(km_ref)

    def augmented_query(q, head_lanes, in_aug, blk, sel, qi):
        keep = sel | (in_aug & (blk == qi))
        return jnp.where(head_lanes, q * scale, jnp.where(keep, 0.0, NEG)).astype(BF16)

    fallbacks = []
    for sub in range(tiles):
        qi = pl.program_id(0) * tiles + sub
        rows = slice(sub * TILE, (sub + 1) * TILE)
        for hh in range(HA):
            pair = slice(hh // 2 * 128, (hh // 2 + 1) * 128)
            q = q_ref[rows, pair]
            head_lanes = (lane < DHA) if hh % 2 == 0 else (lane >= DHA)
            qh = jnp.where(head_lanes, q, 0.0)
            route = _dot_t(qh, km_ref[:, pair], precision=HIGHEST)
            blk = lane - DHA if hh % 2 == 0 else lane
            in_aug = jnp.logical_not(head_lanes)
            cand = in_aug & (blk < qi)
            r = jnp.where(cand, route, NEG)
            sel = _removed_topk(r, MOBA_TOPK)
            qa_ref[rows, hh * 128:(hh + 1) * 128] = augmented_query(q, head_lanes, in_aug, blk, sel, qi)
            count = jnp.sum(jnp.where(sel, 1.0, 0.0), axis=1, keepdims=True)
            want = jnp.minimum(qi, MOBA_TOPK).astype(F32)
            tied = jnp.max(jnp.where(count != want, 1.0, 0.0))
            fallbacks.append((rows, hh, tied, (q, head_lanes, in_aug, blk), r, qi))

        mean = jnp.mean(k_ref[rows, :], axis=0, keepdims=True)
        km_ref[pl.ds(qi, 1), :] = mean
        km_ref[pl.ds(qi + DHA, 1), :] = mean

    for rows, hh, tied, args, r, qi in fallbacks:
        @pl.when(tied > 0.0)
        def _(rows=rows, hh=hh, args=args, r=r, qi=qi):
            qa_ref[rows, hh * 128:(hh + 1) * 128] = augmented_query(*args, _topk_mask(r, lane, MOBA_TOPK), qi)


def _moba_route(q, k):
    t = q.shape[0]
    assert t // MOBA_BLOCK <= DHA and MOBA_BLOCK == TILE
    tiles = min(ROUTE_TILES, t // TILE)
    rows = lambda w: pl.BlockSpec((tiles * TILE, w), lambda i: (i, 0))
    return pl.pallas_call(
        functools.partial(_moba_route_kernel, tiles=tiles),
        grid=(t // (tiles * TILE),),
        in_specs=[rows(HA * DHA), rows(HA * DHA)],
        out_specs=rows(HA * 128),
        out_shape=jax.ShapeDtypeStruct((t, HA * 128), BF16),
        scratch_shapes=[pltpu.VMEM((2 * DHA, HA * DHA), F32)],
        compiler_params=_params("arbitrary"),
        name="moba_route",
    )(q, k)


def _moba_kernel(tab_ref, qa_ref, k_ref, v_ref, o_ref, f_ref, m_ref, acc_ref, s_ref, *, nb, unroll):
    qi = pl.program_id(0)
    lane = lax.broadcasted_iota(jnp.int32, (TILE, 128), 1)

    @pl.when(qi == 0)
    def _():
        _near_bias_tiles(f_ref, tab_ref, list(range(HA)))

    def own_lanes(hh, rows):
        ln = lax.broadcasted_iota(jnp.int32, (rows, 128), 1)
        return (ln < DHA) if hh % 2 == 0 else (ln >= DHA)

    def keys(j, hh, n):
        pair = slice(hh // 2 * 128, (hh // 2 + 1) * 128)
        parts = []
        for u in range(n):
            off = pl.multiple_of((j + u) * TILE, TILE)
            hot = jnp.where(lane == (j + u) + (DHA if hh % 2 == 0 else 0), 1.0, 0.0).astype(BF16)
            parts.append(jnp.where(own_lanes(hh, TILE), k_ref[pl.ds(off, TILE), pair], hot))
        return jnp.concatenate(parts, axis=0)

    def values(j, hh, n):
        pair = slice(hh // 2 * 128, (hh // 2 + 1) * 128)
        off = pl.multiple_of(j * TILE, TILE)
        ln = lax.broadcasted_iota(jnp.int32, (n * TILE, 128), 1)
        ones_col = jnp.where(ln == (DHA if hh % 2 == 0 else 0), 1.0, 0.0).astype(BF16)
        return jnp.where(own_lanes(hh, n * TILE), v_ref[pl.ds(off, n * TILE), pair], ones_col)

    def scores(j, hh, n):
        return _dot_t(qa_ref[:, hh * 128:(hh + 1) * 128], keys(j, hh, n))

    far_tiles = unroll * (jnp.maximum(qi - 1, 0) // unroll)
    groups = far_tiles // unroll
    last = jnp.maximum(groups - 1, 0)

    def produce(buf, grp):
        for hh in range(HA):
            s_ref[buf, hh] = scores(grp * unroll, hh, unroll)

    def consume(buf, grp):
        for hh in range(HA):
            _flash_update(s_ref[buf, hh], values(grp * unroll, hh, unroll), m_ref.at[hh], acc_ref.at[hh], False)

    produce(0, 0)
    start = jnp.minimum(far_tiles, nb - (unroll + 1))
    for hh in range(HA):
        bias = jnp.concatenate([_slot_bias(start + u, qi, far_tiles, f_ref[hh, 0], f_ref[hh, 1])
                                for u in range(unroll + 1)], axis=1)
        _flash_update(scores(start, hh, unroll + 1) + bias, values(start, hh, unroll + 1),
                      m_ref.at[hh], acc_ref.at[hh], True)

    def pair(k, c):
        produce(1, 2 * k + 1)
        consume(0, 2 * k)
        produce(0, jnp.minimum(2 * k + 2, last))
        consume(1, 2 * k + 1)
        return c

    lax.fori_loop(0, groups // 2, pair, 0)

    @pl.when(groups % 2 == 1)
    def _():
        consume(0, groups - 1)

    for pp in range(HA // 2):
        even, odd = acc_ref[2 * pp], acc_ref[2 * pp + 1]
        o_ref[:, pp * 128:(pp + 1) * 128] = jnp.where(lane < DHA, even * (1.0 / even[:, DHA:DHA + 1]),
                                                      odd * (1.0 / odd[:, 0:1])).astype(o_ref.dtype)


def _moba(tab, qa, k, v):
    t = qa.shape[0]
    once = pl.Buffered(1)
    unroll = min(MOBA_UNROLL, t // TILE - 1)
    return pl.pallas_call(
        functools.partial(_moba_kernel, nb=t // TILE, unroll=unroll),
        grid=(t // TILE,),
        in_specs=[_smem(),
                  pl.BlockSpec((TILE, HA * 128), lambda qi: (qi, 0)),
                  pl.BlockSpec((t, HA * DHA), lambda qi: (0, 0), pipeline_mode=once),
                  pl.BlockSpec((t, HA * DHA), lambda qi: (0, 0), pipeline_mode=once)],
        out_specs=pl.BlockSpec((TILE, HA * DHA), lambda qi: (qi, 0)),
        out_shape=jax.ShapeDtypeStruct((t, HA * DHA), BF16),
        scratch_shapes=[pltpu.VMEM((HA, 2, TILE, TILE), F32),
                        pltpu.VMEM((HA, TILE, 128), F32),
                        pltpu.VMEM((HA, TILE, 128), F32),
                        pltpu.VMEM((2, HA, TILE, unroll * TILE), F32)],
        compiler_params=_params("arbitrary", vmem_limit=VMEM_LIMIT_RESIDENT),
        name="moba_attention",
    )(tab, qa, k, v)


def _rglru_kernel(xr_ref, gr_ref, cw_ref, cb_ref, wrg_ref, brg_ref, wig_ref, big_ref, lam_ref,
                  o_ref, xbuf, a_s, b_s, hcar, *, tt):
    @pl.when(pl.program_id(0) == 0)
    def _():
        xbuf[0:8, :] = jnp.zeros((8, LRU_W), F32)
        hcar[...] = jnp.zeros_like(hcar)

    xbuf[8:8 + tt, :] = xr_ref[...]
    xc = cb_ref[...]
    for j in range(CONV_W):
        xc = xc + xbuf[8 - (CONV_W - 1) + j:8 - (CONV_W - 1) + j + tt, :] * cw_ref[j:j + 1, :]
    xbuf[0:8, :] = xbuf[tt:tt + 8, :]

    xcb = xc.astype(BF16)
    r = _sigmoid(jnp.dot(xcb, wrg_ref[...], preferred_element_type=F32) + brg_ref[...])
    ig = _sigmoid(jnp.dot(xcb, wig_ref[...], preferred_element_type=F32) + big_ref[...])
    z = -lam_ref[...]
    softplus = jnp.maximum(z, 0.0) + jnp.log(1.0 + jnp.exp(-jnp.abs(z)))
    a = jnp.exp(-LRU_C * r * softplus)
    a_s[...] = a
    b_s[...] = jnp.sqrt(1.0 - a * a) * ig * xc

    row = lax.broadcasted_iota(jnp.int32, (8, LRU_W), 0)

    def body(gi, h):
        off = pl.multiple_of(gi * 8, 8)
        a8 = a_s[pl.ds(off, 8), :]
        b8 = b_s[pl.ds(off, 8), :]
        for s in (1, 2, 4):
            a_sh = jnp.where(row >= s, pltpu.roll(a8, s, 0), 1.0)
            b_sh = jnp.where(row >= s, pltpu.roll(b8, s, 0), 0.0)
            b8 = a8 * b_sh + b8
            a8 = a8 * a_sh
        h8 = b8 + a8 * h
        b_s[pl.ds(off, 8), :] = h8
        return h8[7:8, :]

    hcar[0:1, :] = lax.fori_loop(0, tt // 8, body, hcar[0:1, :])
    o_ref[...] = (b_s[...] * _gelu_tanh(gr_ref[...])).astype(o_ref.dtype)


def _rglru(xr, gr, conv_w, conv_b, wrg, b_rg, wig, b_ig, lam, tt=512):
    t = xr.shape[0]
    row = lambda a: a.reshape(1, LRU_W)
    full = lambda shape: pl.BlockSpec(shape, lambda i: (0, 0))
    return pl.pallas_call(
        functools.partial(_rglru_kernel, tt=tt),
        grid=(t // tt,),
        in_specs=[pl.BlockSpec((tt, LRU_W), lambda i: (i, 0)),
                  pl.BlockSpec((tt, LRU_W), lambda i: (i, 0)),
                  full((CONV_W, LRU_W)), full((1, LRU_W)),
                  full((LRU_W, LRU_W)), full((1, LRU_W)),
                  full((LRU_W, LRU_W)), full((1, LRU_W)), full((1, LRU_W))],
        out_specs=pl.BlockSpec((tt, LRU_W), lambda i: (i, 0)),
        out_shape=jax.ShapeDtypeStruct((t, LRU_W), BF16),
        scratch_shapes=[pltpu.VMEM((tt + 8, LRU_W), F32),
                        pltpu.VMEM((tt, LRU_W), F32),
                        pltpu.VMEM((tt, LRU_W), F32),
                        pltpu.VMEM((8, LRU_W), F32)],
        compiler_params=_params("arbitrary"),
        name="rglru",
    )(xr, gr, conv_w, row(conv_b), wrg, row(b_rg), wig, row(b_ig), row(lam))


def _post(y, g_ref, x_ref, o_ref):
    ms = jnp.mean(y * y, axis=-1, keepdims=True)
    o_ref[...] = x_ref[...] + y * lax.rsqrt(ms + EPS) * g_ref[...]


def _outproj_ab_kernel(a1_ref, a2_ref, w1_ref, w2_ref, g_ref, x_ref, o_ref):
    y = (jnp.dot(a1_ref[...], w1_ref[...], preferred_element_type=F32)
         + jnp.dot(a2_ref[...], w2_ref[...], preferred_element_type=F32))
    _post(y, g_ref, x_ref, o_ref)


def _outproj_ab(a1, a2, w1, w2, g, x, tm=1024):
    t = x.shape[0]
    rows = lambda w: pl.BlockSpec((tm, w), lambda i: (i, 0))
    full = lambda shape: pl.BlockSpec(shape, lambda i: (0, 0))
    return pl.pallas_call(
        _outproj_ab_kernel,
        grid=(t // tm,),
        in_specs=[rows(a1.shape[1]), rows(a2.shape[1]), full(w1.shape), full(w2.shape),
                  full((1, D)), rows(D)],
        out_specs=rows(D),
        out_shape=jax.ShapeDtypeStruct((t, D), F32),
        compiler_params=_params("arbitrary"),
        name="outproj_ab",
    )(a1, a2, w1, w2, g.reshape(1, D), x)


def _outproj_c_kernel(oc_ref, os_ref, ow_ref, gl_ref, w_ref, g_ref, x_ref, o_ref):
    gates = jax.nn.sigmoid(gl_ref[...])
    parts = []
    for h in range(HC):
        sl = slice(h * DHC, (h + 1) * DHC)
        parts.append(gates[:, 3 * h:3 * h + 1] * oc_ref[:, sl]
                     + gates[:, 3 * h + 1:3 * h + 2] * os_ref[:, sl]
                     + gates[:, 3 * h + 2:3 * h + 3] * ow_ref[:, sl])
    o = jnp.concatenate(parts, axis=1).astype(BF16)
    _post(jnp.dot(o, w_ref[...], preferred_element_type=F32), g_ref, x_ref, o_ref)


def _outproj_c(oc, os_, ow, gl, w, g, x, tm=1024):
    t = x.shape[0]
    rows = lambda w_: pl.BlockSpec((tm, w_), lambda i: (i, 0))
    full = lambda shape: pl.BlockSpec(shape, lambda i: (0, 0))
    return pl.pallas_call(
        _outproj_c_kernel,
        grid=(t // tm,),
        in_specs=[rows(D), rows(D), rows(D), rows(128), full(w.shape), full((1, D)), rows(D)],
        out_specs=rows(D),
        out_shape=jax.ShapeDtypeStruct((t, D), F32),
        compiler_params=_params("arbitrary"),
        name="outproj_c",
    )(oc, os_, ow, gl, w, g.reshape(1, D), x)


def _mlp_kernel(x_ref, gpre_ref, wup_ref, wdn_ref, gpost_ref, o_ref, h_s, acc_s):
    c = pl.program_id(1)

    @pl.when(c == 0)
    def _():
        x = x_ref[...]
        ms = jnp.mean(x * x, axis=-1, keepdims=True)
        h_s[...] = (x * lax.rsqrt(ms + EPS) * gpre_ref[...]).astype(BF16)
        acc_s[...] = jnp.zeros_like(acc_s)

    u = jnp.maximum(jnp.dot(h_s[...], wup_ref[...].astype(BF16), preferred_element_type=F32), 0.0)
    acc_s[...] += jnp.dot((u * u).astype(BF16), wdn_ref[...].astype(BF16), preferred_element_type=F32)

    @pl.when(c == pl.num_programs(1) - 1)
    def _():
        _post(acc_s[...], gpost_ref, x_ref, o_ref)


def _mlp(x, gpre, wup, wdn, gpost, layer, tm=1024, tf=1024):
    t = x.shape[0]
    return pl.pallas_call(
        _mlp_kernel,
        grid=(t // tm, D_FF // tf),
        in_specs=[pl.BlockSpec((tm, D), lambda i, c: (i, 0)),
                  pl.BlockSpec((1, D), lambda i, c: (0, 0)),
                  pl.BlockSpec((None, D, tf), lambda i, c: (layer, 0, c)),
                  pl.BlockSpec((None, tf, D), lambda i, c: (layer, c, 0)),
                  pl.BlockSpec((1, D), lambda i, c: (0, 0))],
        out_specs=pl.BlockSpec((tm, D), lambda i, c: (i, 0)),
        out_shape=jax.ShapeDtypeStruct((t, D), F32),
        scratch_shapes=[pltpu.VMEM((tm, D), BF16), pltpu.VMEM((tm, D), F32)],
        compiler_params=_params("arbitrary", "arbitrary"),
        name="mlp",
    )(x, gpre.reshape(1, D), wup, wdn, gpost.reshape(1, D))


def _compress_kernel(*refs):
    x_refs, (pos_ref, w1_ref, w2_ref, o_ref) = refs[:GC], refs[GC:]
    n = o_ref.shape[1]
    for g, x_ref in enumerate(x_refs):
        top = jnp.zeros((n, w1_ref.shape[1]), F32)
        bot = jnp.zeros((n, w1_ref.shape[1]), F32)
        for l in range(CMP_STRIDE):
            xl = x_ref[pl.ds(l, n, stride=CMP_STRIDE), :]
            lo, hi = l * DHC, (CMP_STRIDE + l) * DHC
            top = top + jnp.dot((xl + pos_ref[l:l + 1, :]).astype(BF16), w1_ref[lo:lo + DHC, :],
                                preferred_element_type=F32)
            bot = bot + jnp.dot((xl + pos_ref[CMP_STRIDE + l:CMP_STRIDE + l + 1, :]).astype(BF16),
                                w1_ref[hi:hi + DHC, :], preferred_element_type=F32)
        hid = top + pltpu.roll(bot, n - 1, 0)
        o_ref[g] = jnp.dot(_gelu_tanh(hid).astype(BF16), w2_ref[...],
                           preferred_element_type=F32).astype(o_ref.dtype)


def _compress(xs, pos, w1, w2):
    t = xs[0].shape[0]
    n = t // CMP_STRIDE
    full = lambda shape: pl.BlockSpec(shape, lambda i: (0,) * len(shape))
    return pl.pallas_call(
        _compress_kernel,
        grid=(1,),
        in_specs=[full((t, DHC))] * GC + [full(pos.shape), full(w1.shape), full(w2.shape)],
        out_specs=full((GC, n, DHC)),
        out_shape=jax.ShapeDtypeStruct((GC, n, DHC), BF16),
        compiler_params=_params("arbitrary"),
        name="nsa_compress",
    )(*xs, pos, w1, w2)


def _nsa_cmp_kernel(tab_ref, q_ref, kc_ref, vc_ref, ov_ref, *refs, tq, ns, nwb):
    kw_refs, vw_refs = refs[:nwb], refs[nwb:2 * nwb]
    oc_ref, sb_ref, ow_ref, e_ref, qb_ref, fw_ref, imp_ref = refs[2 * nwb:]
    qi = pl.program_id(0)
    ncp = kc_ref.shape[1]
    nsp = ov_ref.shape[1]
    scale = DHC ** -0.5 * LOG2E
    cpq = tq // CMP_STRIDE
    k = min(SEL_TOPK, ns)

    @pl.when(qi == 0)
    def _():
        u = lax.broadcasted_iota(jnp.int32, (2 * ncp, 128), 0)
        lane = lax.broadcasted_iota(jnp.int32, (2 * ncp, 128), 1)
        hit = (lane < 2 * CMP_BIAS_LANES) & (u + lane % CMP_BIAS_LANES == ncp - 1)
        after = (lane == 2 * CMP_BIAS_LANES) & (u >= ncp)
        e_ref[...] = (jnp.where(hit, 1.0, 0.0) + jnp.where(after, 1.0, 0.0)).astype(BF16)
        i = lax.broadcasted_iota(jnp.int32, (tq, 128), 0)
        lane = lax.broadcasted_iota(jnp.int32, (tq, 128), 1)
        d = i + CMP_STRIDE * (lane % CMP_BIAS_LANES) - (tq - CMP_STRIDE + CMP_LEN - 1)
        dw = (WINDOW + lax.broadcasted_iota(jnp.int32, (tq, nwb * tq), 0)
              - lax.broadcasted_iota(jnp.int32, (tq, nwb * tq), 1))
        for h in range(HC):
            b = jnp.where(d >= 0, LOG2E * _bias_delta(jnp.maximum(d, 0), tab_ref, h), NEG)
            hi = b.astype(BF16).astype(F32)
            qb = jnp.where(lane < CMP_BIAS_LANES, hi, jnp.where(lane < 2 * CMP_BIAS_LANES, b - hi, 0.0))
            qb_ref[h] = jnp.where(lane == 2 * CMP_BIAS_LANES, NEG, qb).astype(BF16)
            fw_ref[h] = jnp.where((dw >= 0) & (dw < WINDOW),
                                  LOG2E * _bias_delta(jnp.maximum(dw, 0), tab_ref, h), NEG)

    off = pl.multiple_of(ncp - cpq * (qi + 1), 16)
    jj = lax.broadcasted_iota(jnp.int32, (tq, nsp), 1)
    cur = (qi * tq + lax.broadcasted_iota(jnp.int32, (tq, nsp), 0)) // SEL_BLOCK
    forced = (jj == 0) | (jj == cur) | (jj == cur - 1)
    dead = (jj > cur) | (jj >= ns)
    q = q_ref[...]
    qs = [(q[:, h * DHC:(h + 1) * DHC] * scale).astype(BF16) for h in range(HC)]

    def compressed(width):
        onehots = e_ref[pl.ds(off, width), :]
        for g in range(GC):
            heads = range(g * PC, (g + 1) * PC)
            qa = jnp.concatenate([jnp.concatenate([qs[h] for h in heads], axis=0),
                                  jnp.concatenate([qb_ref[h] for h in heads], axis=0)], axis=1)
            s_all = _dot_t(qa, jnp.concatenate([kc_ref[g, 0:width, :], onehots], axis=1))
            psum = jnp.zeros((tq, width), F32)
            for p, h in enumerate(heads):
                s = s_all[p * tq:(p + 1) * tq]
                m = jnp.maximum(jnp.max(s, axis=1, keepdims=True), CMP_MAX_FLOOR)
                e = jnp.exp2(s - m)
                pc = e * (1.0 / jnp.maximum(jnp.sum(e, axis=1, keepdims=True), 1e-30))
                oc_ref[:, h * DHC:(h + 1) * DHC] = jnp.dot(pc.astype(BF16), vc_ref[g, 0:width, :],
                                                           preferred_element_type=F32).astype(oc_ref.dtype)
                psum = psum + pc
            hi = psum.astype(BF16)
            lo = (psum - hi.astype(F32)).astype(BF16)
            imp_ref[g] = (jnp.dot(hi, ov_ref[0:width, :], preferred_element_type=F32)
                          + jnp.dot(lo, ov_ref[0:width, :], preferred_element_type=F32))

    cw = min(CMP_WIDTH_CLASS, ncp)
    cls = (cpq * (qi + 1) - 1) // cw
    for c in range(ncp // cw):
        @pl.when(cls == c)
        def _(c=c):
            compressed((c + 1) * cw)

    fallbacks = []
    for g in range(GC):
        imp = imp_ref[g]
        rest = jnp.where(forced | dead, NEG, imp)
        sel_rest = _removed_topk(rest, k - 3)
        sel = sel_rest | (forced & jnp.logical_not(dead))
        sb_ref[g] = jnp.where(sel, 0.0, NEG).astype(BF16)
        count = jnp.sum(jnp.where(sel_rest, 1.0, 0.0), axis=1, keepdims=True)
        cand = jnp.sum(jnp.where(forced | dead, 0.0, 1.0), axis=1, keepdims=True)
        tied = jnp.max(jnp.where(count != jnp.minimum(cand, float(k - 3)), 1.0, 0.0))
        fallbacks.append((g, imp, tied))

    wk = jnp.concatenate([r[...] for r in kw_refs], axis=0)
    wv = jnp.concatenate([r[...] for r in vw_refs], axis=0)
    blk = lax.broadcasted_iota(jnp.int32, (nwb * tq, DHC), 0) // tq
    flags = jnp.where(blk + qi < nwb - 1, 1.0, 0.0).astype(BF16)
    off_lane = jnp.where(lax.broadcasted_iota(jnp.int32, (tq, DHC), 1) == 0, NEG, 0.0).astype(BF16)
    for h in range(HC):
        g = h // PC
        cols = slice(g * DHC, (g + 1) * DHC)
        s = _dot_t(jnp.concatenate([qs[h], off_lane], axis=1),
                   jnp.concatenate([wk[:, cols], flags], axis=1)) + fw_ref[h]
        e = jnp.exp2(s - jnp.max(s, axis=1, keepdims=True))
        o = jnp.dot(e.astype(BF16), wv[:, cols], preferred_element_type=F32)
        ow_ref[:, h * DHC:(h + 1) * DHC] = (o * (1.0 / jnp.sum(e, axis=1, keepdims=True))).astype(ow_ref.dtype)

    for g, imp, tied in fallbacks:
        @pl.when(tied > 0.0)
        def _(g=g, imp=imp):
            full = jnp.where(dead, NEG, jnp.where(forced, imp + SEL_FORCE, imp))
            sb_ref[g] = jnp.where(_topk_mask(full, jj, k), 0.0, NEG).astype(BF16)


def _nsa_cmp_win(tab, q, kc, vc, overlap, kw, vw, ns, tq=TILE):
    t = q.shape[0]
    ncp = kc.shape[1]
    nsp = overlap.shape[1]
    assert (tq - CMP_STRIDE + CMP_LEN - 1 + MAX_DISTANCE) // CMP_STRIDE < CMP_BIAS_LANES
    assert WINDOW % tq == 0 and min(SEL_TOPK, ns) > 3
    nwb = WINDOW // tq + 1
    full = lambda shape: pl.BlockSpec(shape, lambda qi: (0,) * len(shape))
    rows = pl.BlockSpec((tq, HC * DHC), lambda qi: (qi, 0))
    back = [pl.BlockSpec((tq, GC * DHC), functools.partial(lambda qi, b: (jnp.maximum(qi - b, 0), 0), b=b))
            for b in range(nwb - 1, -1, -1)]
    return pl.pallas_call(
        functools.partial(_nsa_cmp_kernel, tq=tq, ns=ns, nwb=nwb),
        grid=(t // tq,),
        in_specs=[_smem(), rows, full((GC, ncp, DHC)), full((GC, ncp, DHC)), full((ncp, nsp))] + back + back,
        out_specs=[rows, pl.BlockSpec((GC, tq, nsp), lambda qi: (0, qi, 0)), rows],
        out_shape=[jax.ShapeDtypeStruct((t, HC * DHC), BRANCH_DTYPE),
                   jax.ShapeDtypeStruct((GC, t, nsp), BF16),
                   jax.ShapeDtypeStruct((t, HC * DHC), BRANCH_DTYPE)],
        scratch_shapes=[pltpu.VMEM((2 * ncp, 128), BF16), pltpu.VMEM((HC, tq, 128), BF16),
                        pltpu.VMEM((HC, tq, nwb * tq), F32), pltpu.VMEM((GC, tq, nsp), F32)],
        compiler_params=_params("arbitrary"),
        name="nsa_compressed_window",
    )(tab, q, kc, vc, overlap, *([kw] * nwb), *([vw] * nwb))


def _nsa_sel_kernel(tab_ref, q_ref, sb_ref, k_ref, v_ref, o_ref, f_ref, m_ref, acc_ref, s_ref, hot_ref, *, nhalf,
                    nb, unroll):
    qi = pl.program_id(0)
    scale = DHC ** -0.5 * LOG2E
    tiles_per_half = 128 * SEL_BLOCK // TILE

    @pl.when(qi == 0)
    def _():
        _near_bias_tiles(f_ref, tab_ref, list(range(HC)))
        row = lax.broadcasted_iota(jnp.int32, hot_ref.shape, 0)
        lane = lax.broadcasted_iota(jnp.int32, hot_ref.shape, 1)
        hot_ref[...] = jnp.where(lane == (row // SEL_BLOCK) % 128, 1.0, 0.0).astype(BF16)

    hpc = PC // SEL_CHUNKS
    chunks = [(g, list(range(g * PC + c * hpc, g * PC + (c + 1) * hpc)))
              for g in range(GC) for c in range(SEL_CHUNKS)]
    q = q_ref[...]
    qas = []
    for g, heads in chunks:
        qc = jnp.concatenate([q[:, h * DHC:(h + 1) * DHC] for h in heads], axis=0)
        qc = (qc * scale).astype(BF16)
        sb = sb_ref[g]
        qas.append([jnp.concatenate([qc, jnp.concatenate([sb[:, hf * 128:(hf + 1) * 128]] * hpc, axis=0)], axis=1)
                    for hf in range(nhalf)])

    def qa_for(c, j):
        if nhalf == 1:
            return qas[c][0]
        return jnp.where(j >= tiles_per_half, qas[c][1], qas[c][0])

    def block_onehot(off, n):
        return hot_ref[pl.ds(off, n), :]

    def keys(g, off, n, hot):
        return jnp.concatenate([k_ref[pl.ds(off, n), g * DHC:(g + 1) * DHC], hot], axis=1)

    def values(g, off, n):
        ones_col = jnp.where(lax.broadcasted_iota(jnp.int32, (n, DHC), 1) == 0, 1.0, 0.0).astype(BF16)
        return jnp.concatenate([v_ref[pl.ds(off, n), g * DHC:(g + 1) * DHC], ones_col], axis=1)

    far_tiles = unroll * (jnp.maximum(qi - 1, 0) // unroll)
    groups = far_tiles // unroll
    last = jnp.maximum(groups - 1, 0)

    def group_onehot(grp):
        return block_onehot(pl.multiple_of(grp * (unroll * TILE), unroll * TILE), unroll * TILE)

    def produce(buf, grp, hot, only=None):
        off = pl.multiple_of(grp * (unroll * TILE), unroll * TILE)
        for c, (g, _) in enumerate(chunks):
            if only is None or c == only:
                s_ref[buf, c] = _dot_t(qa_for(c, grp * unroll), keys(g, off, unroll * TILE, hot))

    def consume(buf, grp, only=None):
        off = pl.multiple_of(grp * (unroll * TILE), unroll * TILE)
        for c, (g, _) in enumerate(chunks):
            if only is None or c == only:
                _flash_update(s_ref[buf, c], values(g, off, unroll * TILE), m_ref.at[c], acc_ref.at[c], False)

    produce(0, 0, group_onehot(0))
    start = jnp.minimum(far_tiles, nb - (unroll + 1))
    slot_hots = [block_onehot(pl.multiple_of((start + u) * TILE, TILE), TILE) for u in range(unroll + 1)]
    for c, (g, heads) in enumerate(chunks):
        cols = []
        for u in range(unroll + 1):
            j = start + u
            s = _dot_t(qa_for(c, j), keys(g, pl.multiple_of(j * TILE, TILE), TILE, slot_hots[u]))
            cols.append(jnp.concatenate(
                [s[i * TILE:(i + 1) * TILE] + _slot_bias(j, qi, far_tiles, f_ref[h, 0], f_ref[h, 1])
                 for i, h in enumerate(heads)], axis=0))
        vt = values(g, pl.multiple_of(start * TILE, TILE), (unroll + 1) * TILE)
        _flash_update(jnp.concatenate(cols, axis=1), vt, m_ref.at[c], acc_ref.at[c], True)

    def pair(k, carry):
        nxt = 2 * k + 1
        hot = group_onehot(nxt)
        for c in range(len(chunks)):
            produce(1, nxt, hot, c)
            consume(0, 2 * k, c)
        nxt = jnp.minimum(2 * k + 2, last)
        hot = group_onehot(nxt)
        for c in range(len(chunks)):
            produce(0, nxt, hot, c)
            consume(1, 2 * k + 1, c)
        return carry

    lax.fori_loop(0, groups // 2, pair, 0)

    @pl.when(groups % 2 == 1)
    def _():
        consume(0, groups - 1)


    for c, (g, heads) in enumerate(chunks):
        acc = acc_ref[c]
        o = acc[:, :DHC] / acc[:, DHC:DHC + 1]
        for i, h in enumerate(heads):
            o_ref[:, h * DHC:(h + 1) * DHC] = o[i * TILE:(i + 1) * TILE].astype(o_ref.dtype)


def _nsa_sel(tab, q, selbias, k, v):
    t = q.shape[0]
    nsp = selbias.shape[2]
    nchunks = GC * SEL_CHUNKS
    rows = HC // nchunks * TILE
    once = pl.Buffered(1)
    unroll = min(SEL_UNROLL, t // TILE - 1)
    return pl.pallas_call(
        functools.partial(_nsa_sel_kernel, nhalf=nsp // 128, nb=t // TILE, unroll=unroll),
        grid=(t // TILE,),
        in_specs=[_smem(),
                  pl.BlockSpec((TILE, HC * DHC), lambda qi: (qi, 0)),
                  pl.BlockSpec((GC, TILE, nsp), lambda qi: (0, qi, 0)),
                  pl.BlockSpec((t, GC * DHC), lambda qi: (0, 0), pipeline_mode=once),
                  pl.BlockSpec((t, GC * DHC), lambda qi: (0, 0), pipeline_mode=once)],
        out_specs=pl.BlockSpec((TILE, HC * DHC), lambda qi: (qi, 0)),
        out_shape=jax.ShapeDtypeStruct((t, HC * DHC), BRANCH_DTYPE),
        scratch_shapes=[pltpu.VMEM((HC, 2, TILE, TILE), F32),
                        pltpu.VMEM((nchunks, rows, 128), F32),
                        pltpu.VMEM((nchunks, rows, 2 * DHC), F32),
                        pltpu.VMEM((2, nchunks, rows, unroll * TILE), F32),
                        pltpu.VMEM((t, DHC), BF16)],
        compiler_params=_params("arbitrary"),
        name="nsa_selected",
    )(tab, q, selbias, k, v)


def _layer_ab(x, tab, w_in, conv_w, conv_b, w_rg, b_rg, w_ig, b_ig, lam, w_out, g_pre, g_post):
    da = HA * DHA
    segs = [(0, da), (da, da), (da, da), (2 * da, da), (3 * da, LRU_W), (3 * da + LRU_W, LRU_W)]
    q, k32, k, v, xr, gr = _rms_proj(x, g_pre, w_in.astype(BF16), segs, [F32, F32, BF16, BF16, F32, F32])
    o_a = _moba(tab, _moba_route(q, k32), k, v)

    eye = jnp.eye(w_rg.shape[0], dtype=F32)
    blockdiag = lambda wb: jnp.einsum('bij,bc->bicj', wb, eye).reshape(LRU_W, LRU_W).astype(BF16)
    o_b = _rglru(xr, gr, conv_w, conv_b, blockdiag(w_rg), b_rg, blockdiag(w_ig), b_ig, lam)
    wo = w_out.astype(BF16)
    return _outproj_ab(o_a, o_b, wo[:da], wo[da:], g_post, x)


def _layer_c(x, tab, w_in, pos_k, pos_v, k_w1, k_w2, v_w1, v_w2, w_out, g_pre, g_post):
    t = x.shape[0]
    dq = HC * DHC
    ckv = GC * DHC
    w = jnp.pad(w_in, ((0, 0), (0, 128 - 3 * HC))).astype(BF16)
    segs = _segments([dq, DHC, DHC, DHC, DHC, ckv, ckv, ckv, ckv, 128])
    q, kc0, kc1, vc0, vc1, ks, vs, kw, vw, gl = _rms_proj(
        x, g_pre, w, segs, [F32, F32, F32, F32, F32, BF16, BF16, BF16, BF16, F32])

    n = t // CMP_STRIDE
    kc = _compress([kc0, kc1], pos_k, k_w1.astype(BF16), k_w2.astype(BF16))
    vc = _compress([vc0, vc1], pos_v, v_w1.astype(BF16), v_w2.astype(BF16))

    ns = t // SEL_BLOCK
    nsp = -(-ns // 128) * 128
    cs = jnp.arange(n)[:, None] * CMP_STRIDE
    ss = jnp.arange(nsp)[None, :] * SEL_BLOCK
    overlap = ((cs < ss + SEL_BLOCK) & (cs + CMP_LEN > ss) & (jnp.arange(n)[:, None] < n - 1)
               & (jnp.arange(nsp)[None, :] < ns)).astype(BF16)
    o_c, selbias, o_w = _nsa_cmp_win(tab, q, kc, vc, overlap, kw, vw, ns)
    o_s = _nsa_sel(tab, q, selbias, ks, vs)
    return _outproj_c(o_c, o_s, o_w, gl, w_out.astype(BF16), g_post, x)


def kernel(x, rel_bias, ab_w_in, ab_conv_w, ab_conv_b, ab_w_rg, ab_b_rg, ab_w_ig, ab_b_ig, ab_lru_lambda, ab_w_out, c_w_in, c_cmp_pos_k, c_cmp_pos_v, c_cmp_k_w1, c_cmp_k_w2, c_cmp_v_w1, c_cmp_v_w2, c_w_out, norm_mix_pre, norm_mix_post, norm_mlp_pre, norm_mlp_post, w_up, w_down):
    batch = x.shape[0]
    depth = norm_mix_pre.shape[0]
    outs = []
    for b in range(batch):
        xb = x[b]
        for layer in range(depth):
            i = layer // 2
            if layer % 2 == 0:
                xb = _layer_ab(xb, rel_bias, ab_w_in[i], ab_conv_w[i], ab_conv_b[i], ab_w_rg[i], ab_b_rg[i],
                               ab_w_ig[i], ab_b_ig[i], ab_lru_lambda[i], ab_w_out[i],
                               norm_mix_pre[layer], norm_mix_post[layer])
            else:
                xb = _layer_c(xb, rel_bias, c_w_in[i], c_cmp_pos_k[i], c_cmp_pos_v[i], c_cmp_k_w1[i],
                              c_cmp_k_w2[i], c_cmp_v_w1[i], c_cmp_v_w2[i], c_w_out[i],
                              norm_mix_pre[layer], norm_mix_post[layer])
            xb = _mlp(xb, norm_mlp_pre[layer], w_up, w_down, norm_mlp_post[layer], layer)
        outs.append(xb)
    return jnp.stack(outs)
```

```python
import functools
import math

import numpy as np
import jax
import jax.numpy as jnp
from jax import lax
from jax.experimental import pallas as pl
from jax.experimental.pallas import tpu as pltpu

F32 = jnp.float32
BF16 = jnp.bfloat16
HIGHEST = lax.Precision.HIGHEST
LOG2E = 1.4426950408889634

D = 1024
EPS = 1e-6
NEG = -1e30
N_BUCKETS = 32
MAX_DISTANCE = 128

HA = 8
DHA = 64
MOBA_BLOCK = 256
MOBA_TOPK = 3
LRU_W = 512
LRU_C = 8.0
CONV_W = 4
HC = 8
DHC = 128
GC = 2
PC = 4
CMP_LEN = 32
CMP_STRIDE = 16
SEL_BLOCK = 64
SEL_TOPK = 16
WINDOW = 512
SEL_FORCE = 1e4
D_FF = 4096

TILE = 256
MOBA_UNROLL = 2
ROUTE_TILES = 2
SEL_UNROLL = 2
SEL_CHUNKS = 2
CMP_BIAS_LANES = 32
CMP_MAX_FLOOR = -1e20
CMP_WIDTH_CLASS = 256
BRANCH_DTYPE = BF16
VMEM_LIMIT = 56 * 1024 * 1024
VMEM_LIMIT_RESIDENT = 60 * 1024 * 1024


def _bucket_np(d):
    n = np.maximum(d, 0)
    exact = N_BUCKETS // 2
    nf = np.maximum(n, 1).astype(np.float32)
    large = exact + (np.log(nf / np.float32(exact)) / np.float32(math.log(MAX_DISTANCE / exact))
                     * np.float32(N_BUCKETS - exact)).astype(np.int32)
    return np.where(n < exact, n, np.minimum(large, N_BUCKETS - 1))


_THR = [int(np.argmax(_bucket_np(np.arange(0, 4 * MAX_DISTANCE)) >= k)) for k in range(1, N_BUCKETS)]
assert _THR[-1] <= MAX_DISTANCE


def _bias_delta(d, tab_ref, h):
    acc = jnp.full(d.shape, tab_ref[0, h] - tab_ref[N_BUCKETS - 1, h], F32)
    for k in range(1, N_BUCKETS):
        acc = acc + jnp.where(d >= _THR[k - 1], tab_ref[k, h] - tab_ref[k - 1, h], 0.0)
    return acc


def _dot_t(a, b, **kw):
    return lax.dot_general(a, b, (((1,), (1,)), ((), ())), preferred_element_type=F32, **kw)


def _gelu_tanh(x):
    return 0.5 * x * (1.0 + jnp.tanh(0.7978845608028654 * (x + 0.044715 * x * x * x)))


def _sigmoid(x):
    return 0.5 * jnp.tanh(0.5 * x) + 0.5


def _params(*sem, vmem_limit=VMEM_LIMIT):
    return pltpu.CompilerParams(dimension_semantics=sem, vmem_limit_bytes=vmem_limit)


def _smem():
    return pl.BlockSpec(memory_space=pltpu.SMEM)


def _proj_kernel(x_ref, g_ref, w_ref, *out_refs, segs):
    x = x_ref[...]
    ms = jnp.mean(x * x, axis=-1, keepdims=True)
    hn = (x * lax.rsqrt(ms + EPS) * g_ref[...]).astype(BF16)
    for o_ref, (start, width) in zip(out_refs, segs):
        o_ref[...] = jnp.dot(hn, w_ref[:, start:start + width], preferred_element_type=F32).astype(o_ref.dtype)


def _rms_proj(x, g, w, segs, dtypes, tm=1024):
    t = x.shape[0]
    return pl.pallas_call(
        functools.partial(_proj_kernel, segs=segs),
        grid=(t // tm,),
        in_specs=[pl.BlockSpec((tm, D), lambda i: (i, 0)),
                  pl.BlockSpec((1, D), lambda i: (0, 0)),
                  pl.BlockSpec(w.shape, lambda i: (0, 0))],
        out_specs=[pl.BlockSpec((tm, wd), lambda i: (i, 0)) for (_, wd) in segs],
        out_shape=[jax.ShapeDtypeStruct((t, wd), dt) for (_, wd), dt in zip(segs, dtypes)],
        compiler_params=_params("arbitrary"),
        name="rms_proj",
    )(x, g.reshape(1, D), w)


def _segments(widths):
    segs, off = [], 0
    for wd in widths:
        segs.append((off, wd))
        off += wd
    return segs


def _topk_mask(r, lane, k):
    sel = jnp.zeros(r.shape, jnp.bool_)
    lanef = lane.astype(F32)
    for _ in range(k):
        mx = jnp.max(r, axis=1, keepdims=True)
        first = jnp.min(jnp.where(r == mx, lanef, 1e9), axis=1, keepdims=True)
        pick = lanef == first
        sel = sel | (pick & (mx > NEG / 2))
        r = jnp.where(pick, NEG, r)
    return sel


def _removed_topk(r0, k):
    r = r0
    for _ in range(k):
        r = jnp.where(r == jnp.max(r, axis=1, keepdims=True), NEG, r)
    return (r0 > NEG / 2) & (r < NEG / 2)


def _near_bias_tiles(f_ref, tab_ref, heads):
    d0 = (lax.broadcasted_iota(jnp.int32, (TILE, TILE), 0)
          - lax.broadcasted_iota(jnp.int32, (TILE, TILE), 1))
    for i, h in enumerate(heads):
        f_ref[i, 0] = jnp.where(d0 >= 0, LOG2E * _bias_delta(jnp.maximum(d0, 0), tab_ref, h), NEG)
        f_ref[i, 1] = LOG2E * _bias_delta(d0 + TILE, tab_ref, h)


def _slot_bias(j, qi, lo, f_diag, f_prev):
    other = jnp.where((j >= lo) & (j <= qi), 0.0, NEG)
    return jnp.where(j == qi, f_diag, jnp.where(j == qi - 1, f_prev, other))


def _flash_update(s, vt, m_ref, acc_ref, first):
    rm = jnp.max(s, axis=1, keepdims=True)
    if first:
        m_ref[...] = jnp.broadcast_to(rm, m_ref.shape)
        p = jnp.exp2(s - rm)
        acc_ref[...] = jnp.dot(p.astype(BF16), vt, preferred_element_type=F32)
        return
    m_old = m_ref[...]
    m_new = jnp.maximum(m_old, rm)
    alpha = jnp.exp2(m_old - m_new)
    p = jnp.exp2(s - jnp.concatenate([m_new] * (s.shape[1] // m_new.shape[1]), axis=1))
    pv = jnp.dot(p.astype(BF16), vt, preferred_element_type=F32)
    acc_ref[...] = jnp.concatenate([alpha] * (pv.shape[1] // alpha.shape[1]), axis=1) * acc_ref[...] + pv
    m_ref[...] = m_new


def _moba_route_kernel(q_ref, k_ref, qa_ref, km_ref, *, tiles):
    scale = DHA ** -0.5 * LOG2E
    lane = lax.broadcasted_iota(jnp.int32, (TILE, 128), 1)

    @pl.when(pl.program_id(0) == 0)
    def _():
        km_ref[...] = jnp.zeros_like(km_ref)

    def augmented_query(q, head_lanes, in_aug, blk, sel, qi):
        keep = sel | (in_aug & (blk == qi))
        return jnp.where(head_lanes, q * scale, jnp.where(keep, 0.0, NEG)).astype(BF16)

    fallbacks = []
    for sub in range(tiles):
        qi = pl.program_id(0) * tiles + sub
        rows = slice(sub * TILE, (sub + 1) * TILE)
        for hh in range(HA):
            pair = slice(hh // 2 * 128, (hh // 2 + 1) * 128)
            q = q_ref[rows, pair]
            head_lanes = (lane < DHA) if hh % 2 == 0 else (lane >= DHA)
            qh = jnp.where(head_lanes, q, 0.0)
            route = _dot_t(qh, km_ref[:, pair], precision=HIGHEST)
            blk = lane - DHA if hh % 2 == 0 else lane
            in_aug = jnp.logical_not(head_lanes)
            cand = in_aug & (blk < qi)
            r = jnp.where(cand, route, NEG)
            sel = _removed_topk(r, MOBA_TOPK)
            qa_ref[rows, hh * 128:(hh + 1) * 128] = augmented_query(q, head_lanes, in_aug, blk, sel, qi)
            count = jnp.sum(jnp.where(sel, 1.0, 0.0), axis=1, keepdims=True)
            want = jnp.minimum(qi, MOBA_TOPK).astype(F32)
            tied = jnp.max(jnp.where(count != want, 1.0, 0.0))
            fallbacks.append((rows, hh, tied, (q, head_lanes, in_aug, blk), r, qi))

        mean = jnp.mean(k_ref[rows, :], axis=0, keepdims=True)
        km_ref[pl.ds(qi, 1), :] = mean
        km_ref[pl.ds(qi + DHA, 1), :] = mean

    for rows, hh, tied, args, r, qi in fallbacks:
        @pl.when(tied > 0.0)
        def _(rows=rows, hh=hh, args=args, r=r, qi=qi):
            qa_ref[rows, hh * 128:(hh + 1) * 128] = augmented_query(*args, _topk_mask(r, lane, MOBA_TOPK), qi)


def _moba_route(q, k):
    t = q.shape[0]
    assert t // MOBA_BLOCK <= DHA and MOBA_BLOCK == TILE
    tiles = min(ROUTE_TILES, t // TILE)
    rows = lambda w: pl.BlockSpec((tiles * TILE, w), lambda i: (i, 0))
    return pl.pallas_call(
        functools.partial(_moba_route_kernel, tiles=tiles),
        grid=(t // (tiles * TILE),),
        in_specs=[rows(HA * DHA), rows(HA * DHA)],
        out_specs=rows(HA * 128),
        out_shape=jax.ShapeDtypeStruct((t, HA * 128), BF16),
        scratch_shapes=[pltpu.VMEM((2 * DHA, HA * DHA), F32)],
        compiler_params=_params("arbitrary"),
        name="moba_route",
    )(q, k)


def _moba_kernel(tab_ref, qa_ref, k_ref, v_ref, o_ref, f_ref, m_ref, acc_ref, s_ref, *, nb, unroll):
    qi = pl.program_id(0)
    lane = lax.broadcasted_iota(jnp.int32, (TILE, 128), 1)

    @pl.when(qi == 0)
    def _():
        _near_bias_tiles(f_ref, tab_ref, list(range(HA)))

    def own_lanes(hh, rows):
        ln = lax.broadcasted_iota(jnp.int32, (rows, 128), 1)
        return (ln < DHA) if hh % 2 == 0 else (ln >= DHA)

    def keys(j, hh, n):
        pair = slice(hh // 2 * 128, (hh // 2 + 1) * 128)
        parts = []
        for u in range(n):
            off = pl.multiple_of((j + u) * TILE, TILE)
            hot = jnp.where(lane == (j + u) + (DHA if hh % 2 == 0 else 0), 1.0, 0.0).astype(BF16)
            parts.append(jnp.where(own_lanes(hh, TILE), k_ref[pl.ds(off, TILE), pair], hot))
        return jnp.concatenate(parts, axis=0)

    def values(j, hh, n):
        pair = slice(hh // 2 * 128, (hh // 2 + 1) * 128)
        off = pl.multiple_of(j * TILE, TILE)
        ln = lax.broadcasted_iota(jnp.int32, (n * TILE, 128), 1)
        ones_col = jnp.where(ln == (DHA if hh % 2 == 0 else 0), 1.0, 0.0).astype(BF16)
        return jnp.where(own_lanes(hh, n * TILE), v_ref[pl.ds(off, n * TILE), pair], ones_col)

    def scores(j, hh, n):
        return _dot_t(qa_ref[:, hh * 128:(hh + 1) * 128], keys(j, hh, n))

    far_tiles = unroll * (jnp.maximum(qi - 1, 0) // unroll)
    groups = far_tiles // unroll
    last = jnp.maximum(groups - 1, 0)

    def produce(buf, grp):
        for hh in range(HA):
            s_ref[buf, hh] = scores(grp * unroll, hh, unroll)

    def consume(buf, grp):
        for hh in range(HA):
            _flash_update(s_ref[buf, hh], values(grp * unroll, hh, unroll), m_ref.at[hh], acc_ref.at[hh], False)

    produce(0, 0)
    start = jnp.minimum(far_tiles, nb - (unroll + 1))
    for hh in range(HA):
        bias = jnp.concatenate([_slot_bias(start + u, qi, far_tiles, f_ref[hh, 0], f_ref[hh, 1])
                                for u in range(unroll + 1)], axis=1)
        _flash_update(scores(start, hh, unroll + 1) + bias, values(start, hh, unroll + 1),
                      m_ref.at[hh], acc_ref.at[hh], True)

    def pair(k, c):
        produce(1, 2 * k + 1)
        consume(0, 2 * k)
        produce(0, jnp.minimum(2 * k + 2, last))
        consume(1, 2 * k + 1)
        return c

    lax.fori_loop(0, groups // 2, pair, 0)

    @pl.when(groups % 2 == 1)
    def _():
        consume(0, groups - 1)

    for pp in range(HA // 2):
        even, odd = acc_ref[2 * pp], acc_ref[2 * pp + 1]
        o_ref[:, pp * 128:(pp + 1) * 128] = jnp.where(lane < DHA, even * (1.0 / even[:, DHA:DHA + 1]),
                                                      odd * (1.0 / odd[:, 0:1])).astype(o_ref.dtype)


def _moba(tab, qa, k, v):
    t = qa.shape[0]
    once = pl.Buffered(1)
    unroll = min(MOBA_UNROLL, t // TILE - 1)
    return pl.pallas_call(
        functools.partial(_moba_kernel, nb=t // TILE, unroll=unroll),
        grid=(t // TILE,),
        in_specs=[_smem(),
                  pl.BlockSpec((TILE, HA * 128), lambda qi: (qi, 0)),
                  pl.BlockSpec((t, HA * DHA), lambda qi: (0, 0), pipeline_mode=once),
                  pl.BlockSpec((t, HA * DHA), lambda qi: (0, 0), pipeline_mode=once)],
        out_specs=pl.BlockSpec((TILE, HA * DHA), lambda qi: (qi, 0)),
        out_shape=jax.ShapeDtypeStruct((t, HA * DHA), BF16),
        scratch_shapes=[pltpu.VMEM((HA, 2, TILE, TILE), F32),
                        pltpu.VMEM((HA, TILE, 128), F32),
                        pltpu.VMEM((HA, TILE, 128), F32),
                        pltpu.VMEM((2, HA, TILE, unroll * TILE), F32)],
        compiler_params=_params("arbitrary", vmem_limit=VMEM_LIMIT_RESIDENT),
        name="moba_attention",
    )(tab, qa, k, v)


def _rglru_kernel(xr_ref, gr_ref, cw_ref, cb_ref, wrg_ref, brg_ref, wig_ref, big_ref, lam_ref,
                  o_ref, xbuf, a_s, b_s, hcar, *, tt):
    @pl.when(pl.program_id(0) == 0)
    def _():
        xbuf[0:8, :] = jnp.zeros((8, LRU_W), F32)
        hcar[...] = jnp.zeros_like(hcar)

    xbuf[8:8 + tt, :] = xr_ref[...]
    xc = cb_ref[...]
    for j in range(CONV_W):
        xc = xc + xbuf[8 - (CONV_W - 1) + j:8 - (CONV_W - 1) + j + tt, :] * cw_ref[j:j + 1, :]
    xbuf[0:8, :] = xbuf[tt:tt + 8, :]

    xcb = xc.astype(BF16)
    r = _sigmoid(jnp.dot(xcb, wrg_ref[...], preferred_element_type=F32) + brg_ref[...])
    ig = _sigmoid(jnp.dot(xcb, wig_ref[...], preferred_element_type=F32) + big_ref[...])
    z = -lam_ref[...]
    softplus = jnp.maximum(z, 0.0) + jnp.log(1.0 + jnp.exp(-jnp.abs(z)))
    a = jnp.exp(-LRU_C * r * softplus)
    a_s[...] = a
    b_s[...] = jnp.sqrt(1.0 - a * a) * ig * xc

    row = lax.broadcasted_iota(jnp.int32, (8, LRU_W), 0)

    def body(gi, h):
        off = pl.multiple_of(gi * 8, 8)
        a8 = a_s[pl.ds(off, 8), :]
        b8 = b_s[pl.ds(off, 8), :]
        for s in (1, 2, 4):
            a_sh = jnp.where(row >= s, pltpu.roll(a8, s, 0), 1.0)
            b_sh = jnp.where(row >= s, pltpu.roll(b8, s, 0), 0.0)
            b8 = a8 * b_sh + b8
            a8 = a8 * a_sh
        h8 = b8 + a8 * h
        b_s[pl.ds(off, 8), :] = h8
        return h8[7:8, :]

    hcar[0:1, :] = lax.fori_loop(0, tt // 8, body, hcar[0:1, :])
    o_ref[...] = (b_s[...] * _gelu_tanh(gr_ref[...])).astype(o_ref.dtype)


def _rglru(xr, gr, conv_w, conv_b, wrg, b_rg, wig, b_ig, lam, tt=512):
    t = xr.shape[0]
    row = lambda a: a.reshape(1, LRU_W)
    full = lambda shape: pl.BlockSpec(shape, lambda i: (0, 0))
    return pl.pallas_call(
        functools.partial(_rglru_kernel, tt=tt),
        grid=(t // tt,),
        in_specs=[pl.BlockSpec((tt, LRU_W), lambda i: (i, 0)),
                  pl.BlockSpec((tt, LRU_W), lambda i: (i, 0)),
                  full((CONV_W, LRU_W)), full((1, LRU_W)),
                  full((LRU_W, LRU_W)), full((1, LRU_W)),
                  full((LRU_W, LRU_W)), full((1, LRU_W)), full((1, LRU_W))],
        out_specs=pl.BlockSpec((tt, LRU_W), lambda i: (i, 0)),
        out_shape=jax.ShapeDtypeStruct((t, LRU_W), BF16),
        scratch_shapes=[pltpu.VMEM((tt + 8, LRU_W), F32),
                        pltpu.VMEM((tt, LRU_W), F32),
                        pltpu.VMEM((tt, LRU_W), F32),
                        pltpu.VMEM((8, LRU_W), F32)],
        compiler_params=_params("arbitrary"),
        name="rglru",
    )(xr, gr, conv_w, row(conv_b), wrg, row(b_rg), wig, row(b_ig), row(lam))


def _post(y, g_ref, x_ref, o_ref):
    ms = jnp.mean(y * y, axis=-1, keepdims=True)
    o_ref[...] = x_ref[...] + y * lax.rsqrt(ms + EPS) * g_ref[...]


def _outproj_ab_kernel(a1_ref, a2_ref, w1_ref, w2_ref, g_ref, x_ref, o_ref):
    y = (jnp.dot(a1_ref[...], w1_ref[...], preferred_element_type=F32)
         + jnp.dot(a2_ref[...], w2_ref[...], preferred_element_type=F32))
    _post(y, g_ref, x_ref, o_ref)


def _outproj_ab(a1, a2, w1, w2, g, x, tm=1024):
    t = x.shape[0]
    rows = lambda w: pl.BlockSpec((tm, w), lambda i: (i, 0))
    full = lambda shape: pl.BlockSpec(shape, lambda i: (0, 0))
    return pl.pallas_call(
        _outproj_ab_kernel,
        grid=(t // tm,),
        in_specs=[rows(a1.shape[1]), rows(a2.shape[1]), full(w1.shape), full(w2.shape),
                  full((1, D)), rows(D)],
        out_specs=rows(D),
        out_shape=jax.ShapeDtypeStruct((t, D), F32),
        compiler_params=_params("arbitrary"),
        name="outproj_ab",
    )(a1, a2, w1, w2, g.reshape(1, D), x)


def _outproj_c_kernel(oc_ref, os_ref, ow_ref, gl_ref, w_ref, g_ref, x_ref, o_ref):
    gates = jax.nn.sigmoid(gl_ref[...])
    parts = []
    for h in range(HC):
        sl = slice(h * DHC, (h + 1) * DHC)
        parts.append(gates[:, 3 * h:3 * h + 1] * oc_ref[:, sl]
                     + gates[:, 3 * h + 1:3 * h + 2] * os_ref[:, sl]
                     + gates[:, 3 * h + 2:3 * h + 3] * ow_ref[:, sl])
    o = jnp.concatenate(parts, axis=1).astype(BF16)
    _post(jnp.dot(o, w_ref[...], preferred_element_type=F32), g_ref, x_ref, o_ref)


def _outproj_c(oc, os_, ow, gl, w, g, x, tm=1024):
    t = x.shape[0]
    rows = lambda w_: pl.BlockSpec((tm, w_), lambda i: (i, 0))
    full = lambda shape: pl.BlockSpec(shape, lambda i: (0, 0))
    return pl.pallas_call(
        _outproj_c_kernel,
        grid=(t // tm,),
        in_specs=[rows(D), rows(D), rows(D), rows(128), full(w.shape), full((1, D)), rows(D)],
        out_specs=rows(D),
        out_shape=jax.ShapeDtypeStruct((t, D), F32),
        compiler_params=_params("arbitrary"),
        name="outproj_c",
    )(oc, os_, ow, gl, w, g.reshape(1, D), x)


def _mlp_kernel(x_ref, gpre_ref, wup_ref, wdn_ref, gpost_ref, o_ref, h_s, acc_s):
    c = pl.program_id(1)

    @pl.when(c == 0)
    def _():
        x = x_ref[...]
        ms = jnp.mean(x * x, axis=-1, keepdims=True)
        h_s[...] = (x * lax.rsqrt(ms + EPS) * gpre_ref[...]).astype(BF16)
        acc_s[...] = jnp.zeros_like(acc_s)

    u = jnp.maximum(jnp.dot(h_s[...], wup_ref[...].astype(BF16), preferred_element_type=F32), 0.0)
    acc_s[...] += jnp.dot((u * u).astype(BF16), wdn_ref[...].astype(BF16), preferred_element_type=F32)

    @pl.when(c == pl.num_programs(1) - 1)
    def _():
        _post(acc_s[...], gpost_ref, x_ref, o_ref)


def _mlp(x, gpre, wup, wdn, gpost, layer, tm=1024, tf=1024):
    t = x.shape[0]
    return pl.pallas_call(
        _mlp_kernel,
        grid=(t // tm, D_FF // tf),
        in_specs=[pl.BlockSpec((tm, D), lambda i, c: (i, 0)),
                  pl.BlockSpec((1, D), lambda i, c: (0, 0)),
                  pl.BlockSpec((None, D, tf), lambda i, c: (layer, 0, c)),
                  pl.BlockSpec((None, tf, D), lambda i, c: (layer, c, 0)),
                  pl.BlockSpec((1, D), lambda i, c: (0, 0))],
        out_specs=pl.BlockSpec((tm, D), lambda i, c: (i, 0)),
        out_shape=jax.ShapeDtypeStruct((t, D), F32),
        scratch_shapes=[pltpu.VMEM((tm, D), BF16), pltpu.VMEM((tm, D), F32)],
        compiler_params=_params("arbitrary", "arbitrary"),
        name="mlp",
    )(x, gpre.reshape(1, D), wup, wdn, gpost.reshape(1, D))


def _mix_mlp_kernel(a1_ref, a2_ref, w1_ref, w2_ref, gmix_ref, x_ref, gpre_ref, wup_ref, wdn_ref, gpost_ref,
                    o_ref, h_s, acc_s, x1_s):
    c = pl.program_id(1)

    @pl.when(c == 0)
    def _():
        y = (jnp.dot(a1_ref[...], w1_ref[...], preferred_element_type=F32)
             + jnp.dot(a2_ref[...], w2_ref[...], preferred_element_type=F32))
        x1 = x_ref[...] + y * lax.rsqrt(jnp.mean(y * y, axis=-1, keepdims=True) + EPS) * gmix_ref[...]
        x1_s[...] = x1
        ms = jnp.mean(x1 * x1, axis=-1, keepdims=True)
        h_s[...] = (x1 * lax.rsqrt(ms + EPS) * gpre_ref[...]).astype(BF16)
        acc_s[...] = jnp.zeros_like(acc_s)

    u = jnp.maximum(jnp.dot(h_s[...], wup_ref[...].astype(BF16), preferred_element_type=F32), 0.0)
    acc_s[...] += jnp.dot((u * u).astype(BF16), wdn_ref[...].astype(BF16), preferred_element_type=F32)

    @pl.when(c == pl.num_programs(1) - 1)
    def _():
        _post(acc_s[...], gpost_ref, x1_s, o_ref)


def _mix_mlp(a1, a2, w1, w2, gmix, x, gpre, wup, wdn, gpost, layer, tm=1024, tf=1024):
    t = x.shape[0]
    rows = lambda w: pl.BlockSpec((tm, w), lambda i, c: (i, 0))
    full = lambda shape: pl.BlockSpec(shape, lambda i, c: (0, 0))
    vec = lambda g: g.reshape(1, D)
    return pl.pallas_call(
        _mix_mlp_kernel,
        grid=(t // tm, D_FF // tf),
        in_specs=[rows(a1.shape[1]), rows(a2.shape[1]), full(w1.shape), full(w2.shape), full((1, D)), rows(D),
                  full((1, D)),
                  pl.BlockSpec((None, D, tf), lambda i, c: (layer, 0, c)),
                  pl.BlockSpec((None, tf, D), lambda i, c: (layer, c, 0)),
                  full((1, D))],
        out_specs=rows(D),
        out_shape=jax.ShapeDtypeStruct((t, D), F32),
        scratch_shapes=[pltpu.VMEM((tm, D), BF16), pltpu.VMEM((tm, D), F32), pltpu.VMEM((tm, D), F32)],
        compiler_params=_params("arbitrary", "arbitrary", vmem_limit=VMEM_LIMIT_RESIDENT),
        name="mix_mlp",
    )(a1, a2, w1, w2, vec(gmix), x, vec(gpre), wup, wdn, vec(gpost))


def _compress_kernel(*refs):
    x_refs, (pos_ref, w1_ref, w2_ref, o_ref) = refs[:GC], refs[GC:]
    n = o_ref.shape[1]
    for g, x_ref in enumerate(x_refs):
        top = jnp.zeros((n, w1_ref.shape[1]), F32)
        bot = jnp.zeros((n, w1_ref.shape[1]), F32)
        for l in range(CMP_STRIDE):
            xl = x_ref[pl.ds(l, n, stride=CMP_STRIDE), :]
            lo, hi = l * DHC, (CMP_STRIDE + l) * DHC
            top = top + jnp.dot((xl + pos_ref[l:l + 1, :]).astype(BF16), w1_ref[lo:lo + DHC, :],
                                preferred_element_type=F32)
            bot = bot + jnp.dot((xl + pos_ref[CMP_STRIDE + l:CMP_STRIDE + l + 1, :]).astype(BF16),
                                w1_ref[hi:hi + DHC, :], preferred_element_type=F32)
        hid = top + pltpu.roll(bot, n - 1, 0)
        o_ref[g] = jnp.dot(_gelu_tanh(hid).astype(BF16), w2_ref[...],
                           preferred_element_type=F32).astype(o_ref.dtype)


def _compress(xs, pos, w1, w2):
    t = xs[0].shape[0]
    n = t // CMP_STRIDE
    full = lambda shape: pl.BlockSpec(shape, lambda i: (0,) * len(shape))
    return pl.pallas_call(
        _compress_kernel,
        grid=(1,),
        in_specs=[full((t, DHC))] * GC + [full(pos.shape), full(w1.shape), full(w2.shape)],
        out_specs=full((GC, n, DHC)),
        out_shape=jax.ShapeDtypeStruct((GC, n, DHC), BF16),
        compiler_params=_params("arbitrary"),
        name="nsa_compress",
    )(*xs, pos, w1, w2)


def _nsa_cmp_kernel(tab_ref, q_ref, kc_ref, vc_ref, ov_ref, *refs, tq, ns, nwb):
    kw_refs, vw_refs = refs[:nwb], refs[nwb:2 * nwb]
    oc_ref, sb_ref, ow_ref, e_ref, qb_ref, fw_ref, imp_ref = refs[2 * nwb:]
    qi = pl.program_id(0)
    ncp = kc_ref.shape[1]
    nsp = ov_ref.shape[1]
    scale = DHC ** -0.5 * LOG2E
    cpq = tq // CMP_STRIDE
    k = min(SEL_TOPK, ns)

    @pl.when(qi == 0)
    def _():
        u = lax.broadcasted_iota(jnp.int32, (2 * ncp, 128), 0)
        lane = lax.broadcasted_iota(jnp.int32, (2 * ncp, 128), 1)
        hit = (lane < 2 * CMP_BIAS_LANES) & (u + lane % CMP_BIAS_LANES == ncp - 1)
        after = (lane == 2 * CMP_BIAS_LANES) & (u >= ncp)
        e_ref[...] = (jnp.where(hit, 1.0, 0.0) + jnp.where(after, 1.0, 0.0)).astype(BF16)
        i = lax.broadcasted_iota(jnp.int32, (tq, 128), 0)
        lane = lax.broadcasted_iota(jnp.int32, (tq, 128), 1)
        d = i + CMP_STRIDE * (lane % CMP_BIAS_LANES) - (tq - CMP_STRIDE + CMP_LEN - 1)
        dw = (WINDOW + lax.broadcasted_iota(jnp.int32, (tq, nwb * tq), 0)
              - lax.broadcasted_iota(jnp.int32, (tq, nwb * tq), 1))
        for h in range(HC):
            b = jnp.where(d >= 0, LOG2E * _bias_delta(jnp.maximum(d, 0), tab_ref, h), NEG)
            hi = b.astype(BF16).astype(F32)
            qb = jnp.where(lane < CMP_BIAS_LANES, hi, jnp.where(lane < 2 * CMP_BIAS_LANES, b - hi, 0.0))
            qb_ref[h] = jnp.where(lane == 2 * CMP_BIAS_LANES, NEG, qb).astype(BF16)
            fw_ref[h] = jnp.where((dw >= 0) & (dw < WINDOW),
                                  LOG2E * _bias_delta(jnp.maximum(dw, 0), tab_ref, h), NEG)

    off = pl.multiple_of(ncp - cpq * (qi + 1), 16)
    jj = lax.broadcasted_iota(jnp.int32, (tq, nsp), 1)
    cur = (qi * tq + lax.broadcasted_iota(jnp.int32, (tq, nsp), 0)) // SEL_BLOCK
    forced = (jj == 0) | (jj == cur) | (jj == cur - 1)
    dead = (jj > cur) | (jj >= ns)
    q = q_ref[...]
    qs = [(q[:, h * DHC:(h + 1) * DHC] * scale).astype(BF16) for h in range(HC)]

    def compressed(width):
        onehots = e_ref[pl.ds(off, width), :]
        for g in range(GC):
            heads = range(g * PC, (g + 1) * PC)
            qa = jnp.concatenate([jnp.concatenate([qs[h] for h in heads], axis=0),
                                  jnp.concatenate([qb_ref[h] for h in heads], axis=0)], axis=1)
            s_all = _dot_t(qa, jnp.concatenate([kc_ref[g, 0:width, :], onehots], axis=1))
            psum = jnp.zeros((tq, width), F32)
            for p, h in enumerate(heads):
                s = s_all[p * tq:(p + 1) * tq]
                m = jnp.maximum(jnp.max(s, axis=1, keepdims=True), CMP_MAX_FLOOR)
                e = jnp.exp2(s - m)
                pc = e * (1.0 / jnp.maximum(jnp.sum(e, axis=1, keepdims=True), 1e-30))
                oc_ref[:, h * DHC:(h + 1) * DHC] = jnp.dot(pc.astype(BF16), vc_ref[g, 0:width, :],
                                                           preferred_element_type=F32).astype(oc_ref.dtype)
                psum = psum + pc
            hi = psum.astype(BF16)
            lo = (psum - hi.astype(F32)).astype(BF16)
            imp_ref[g] = (jnp.dot(hi, ov_ref[0:width, :], preferred_element_type=F32)
                          + jnp.dot(lo, ov_ref[0:width, :], preferred_element_type=F32))

    cw = min(CMP_WIDTH_CLASS, ncp)
    cls = (cpq * (qi + 1) - 1) // cw
    for c in range(ncp // cw):
        @pl.when(cls == c)
        def _(c=c):
            compressed((c + 1) * cw)

    fallbacks = []
    for g in range(GC):
        imp = imp_ref[g]
        rest = jnp.where(forced | dead, NEG, imp)
        sel_rest = _removed_topk(rest, k - 3)
        sel = sel_rest | (forced & jnp.logical_not(dead))
        sb_ref[g] = jnp.where(sel, 0.0, NEG).astype(BF16)
        count = jnp.sum(jnp.where(sel_rest, 1.0, 0.0), axis=1, keepdims=True)
        cand = jnp.sum(jnp.where(forced | dead, 0.0, 1.0), axis=1, keepdims=True)
        tied = jnp.max(jnp.where(count != jnp.minimum(cand, float(k - 3)), 1.0, 0.0))
        fallbacks.append((g, imp, tied))

    wk = jnp.concatenate([r[...] for r in kw_refs], axis=0)
    wv = jnp.concatenate([r[...] for r in vw_refs], axis=0)
    blk = lax.broadcasted_iota(jnp.int32, (nwb * tq, DHC), 0) // tq
    flags = jnp.where(blk + qi < nwb - 1, 1.0, 0.0).astype(BF16)
    off_lane = jnp.where(lax.broadcasted_iota(jnp.int32, (tq, DHC), 1) == 0, NEG, 0.0).astype(BF16)
    for h in range(HC):
        g = h // PC
        cols = slice(g * DHC, (g + 1) * DHC)
        s = _dot_t(jnp.concatenate([qs[h], off_lane], axis=1),
                   jnp.concatenate([wk[:, cols], flags], axis=1)) + fw_ref[h]
        e = jnp.exp2(s - jnp.max(s, axis=1, keepdims=True))
        o = jnp.dot(e.astype(BF16), wv[:, cols], preferred_element_type=F32)
        ow_ref[:, h * DHC:(h + 1) * DHC] = (o * (1.0 / jnp.sum(e, axis=1, keepdims=True))).astype(ow_ref.dtype)

    for g, imp, tied in fallbacks:
        @pl.when(tied > 0.0)
        def _(g=g, imp=imp):
            full = jnp.where(dead, NEG, jnp.where(forced, imp + SEL_FORCE, imp))
            sb_ref[g] = jnp.where(_topk_mask(full, jj, k), 0.0, NEG).astype(BF16)


def _nsa_cmp_win(tab, q, kc, vc, overlap, kw, vw, ns, tq=TILE):
    t = q.shape[0]
    ncp = kc.shape[1]
    nsp = overlap.shape[1]
    assert (tq - CMP_STRIDE + CMP_LEN - 1 + MAX_DISTANCE) // CMP_STRIDE < CMP_BIAS_LANES
    assert WINDOW % tq == 0 and min(SEL_TOPK, ns) > 3
    nwb = WINDOW // tq + 1
    full = lambda shape: pl.BlockSpec(shape, lambda qi: (0,) * len(shape))
    rows = pl.BlockSpec((tq, HC * DHC), lambda qi: (qi, 0))
    back = [pl.BlockSpec((tq, GC * DHC), functools.partial(lambda qi, b: (jnp.maximum(qi - b, 0), 0), b=b))
            for b in range(nwb - 1, -1, -1)]
    return pl.pallas_call(
        functools.partial(_nsa_cmp_kernel, tq=tq, ns=ns, nwb=nwb),
        grid=(t // tq,),
        in_specs=[_smem(), rows, full((GC, ncp, DHC)), full((GC, ncp, DHC)), full((ncp, nsp))] + back + back,
        out_specs=[rows, pl.BlockSpec((GC, tq, nsp), lambda qi: (0, qi, 0)), rows],
        out_shape=[jax.ShapeDtypeStruct((t, HC * DHC), BRANCH_DTYPE),
                   jax.ShapeDtypeStruct((GC, t, nsp), BF16),
                   jax.ShapeDtypeStruct((t, HC * DHC), BRANCH_DTYPE)],
        scratch_shapes=[pltpu.VMEM((2 * ncp, 128), BF16), pltpu.VMEM((HC, tq, 128), BF16),
                        pltpu.VMEM((HC, tq, nwb * tq), F32), pltpu.VMEM((GC, tq, nsp), F32)],
        compiler_params=_params("arbitrary"),
        name="nsa_compressed_window",
    )(tab, q, kc, vc, overlap, *([kw] * nwb), *([vw] * nwb))


def _nsa_sel_kernel(tab_ref, q_ref, sb_ref, k_ref, v_ref, o_ref, f_ref, m_ref, acc_ref, s_ref, hot_ref, *, nhalf,
                    nb, unroll):
    qi = pl.program_id(0)
    scale = DHC ** -0.5 * LOG2E
    tiles_per_half = 128 * SEL_BLOCK // TILE

    @pl.when(qi == 0)
    def _():
        _near_bias_tiles(f_ref, tab_ref, list(range(HC)))
        row = lax.broadcasted_iota(jnp.int32, hot_ref.shape, 0)
        lane = lax.broadcasted_iota(jnp.int32, hot_ref.shape, 1)
        hot_ref[...] = jnp.where(lane == (row // SEL_BLOCK) % 128, 1.0, 0.0).astype(BF16)

    hpc = PC // SEL_CHUNKS
    chunks = [(g, list(range(g * PC + c * hpc, g * PC + (c + 1) * hpc)))
              for g in range(GC) for c in range(SEL_CHUNKS)]
    q = q_ref[...]
    qas = []
    for g, heads in chunks:
        qc = jnp.concatenate([q[:, h * DHC:(h + 1) * DHC] for h in heads], axis=0)
        qc = (qc * scale).astype(BF16)
        sb = sb_ref[g]
        qas.append([jnp.concatenate([qc, jnp.concatenate([sb[:, hf * 128:(hf + 1) * 128]] * hpc, axis=0)], axis=1)
                    for hf in range(nhalf)])

    def qa_for(c, j):
        if nhalf == 1:
            return qas[c][0]
        return jnp.where(j >= tiles_per_half, qas[c][1], qas[c][0])

    def block_onehot(off, n):
        return hot_ref[pl.ds(off, n), :]

    def keys(g, off, n, hot):
        return jnp.concatenate([k_ref[pl.ds(off, n), g * DHC:(g + 1) * DHC], hot], axis=1)

    def values(g, off, n):
        ones_col = jnp.where(lax.broadcasted_iota(jnp.int32, (n, DHC), 1) == 0, 1.0, 0.0).astype(BF16)
        return jnp.concatenate([v_ref[pl.ds(off, n), g * DHC:(g + 1) * DHC], ones_col], axis=1)

    far_tiles = unroll * (jnp.maximum(qi - 1, 0) // unroll)
    groups = far_tiles // unroll
    last = jnp.maximum(groups - 1, 0)

    def group_onehot(grp):
        return block_onehot(pl.multiple_of(grp * (unroll * TILE), unroll * TILE), unroll * TILE)

    def produce(buf, grp, hot, only=None):
        off = pl.multiple_of(grp * (unroll * TILE), unroll * TILE)
        for c, (g, _) in enumerate(chunks):
            if only is None or c == only:
                s_ref[buf, c] = _dot_t(qa_for(c, grp * unroll), keys(g, off, unroll * TILE, hot))

    def consume(buf, grp, only=None):
        off = pl.multiple_of(grp * (unroll * TILE), unroll * TILE)
        for c, (g, _) in enumerate(chunks):
            if only is None or c == only:
                _flash_update(s_ref[buf, c], values(g, off, unroll * TILE), m_ref.at[c], acc_ref.at[c], False)

    produce(0, 0, group_onehot(0))
    start = jnp.minimum(far_tiles, nb - (unroll + 1))
    slot_hots = [block_onehot(pl.multiple_of((start + u) * TILE, TILE), TILE) for u in range(unroll + 1)]
    for c, (g, heads) in enumerate(chunks):
        cols = []
        for u in range(unroll + 1):
            j = start + u
            s = _dot_t(qa_for(c, j), keys(g, pl.multiple_of(j * TILE, TILE), TILE, slot_hots[u]))
            cols.append(jnp.concatenate(
                [s[i * TILE:(i + 1) * TILE] + _slot_bias(j, qi, far_tiles, f_ref[h, 0], f_ref[h, 1])
                 for i, h in enumerate(heads)], axis=0))
        vt = values(g, pl.multiple_of(start * TILE, TILE), (unroll + 1) * TILE)
        _flash_update(jnp.concatenate(cols, axis=1), vt, m_ref.at[c], acc_ref.at[c], True)

    def pair(k, carry):
        nxt = 2 * k + 1
        hot = group_onehot(nxt)
        for c in range(len(chunks)):
            produce(1, nxt, hot, c)
            consume(0, 2 * k, c)
        nxt = jnp.minimum(2 * k + 2, last)
        hot = group_onehot(nxt)
        for c in range(len(chunks)):
            produce(0, nxt, hot, c)
            consume(1, 2 * k + 1, c)
        return carry

    lax.fori_loop(0, groups // 2, pair, 0)

    @pl.when(groups % 2 == 1)
    def _():
        consume(0, groups - 1)


    for c, (g, heads) in enumerate(chunks):
        acc = acc_ref[c]
        o = acc[:, :DHC] / acc[:, DHC:DHC + 1]
        for i, h in enumerate(heads):
            o_ref[:, h * DHC:(h + 1) * DHC] = o[i * TILE:(i + 1) * TILE].astype(o_ref.dtype)


def _nsa_sel(tab, q, selbias, k, v):
    t = q.shape[0]
    nsp = selbias.shape[2]
    nchunks = GC * SEL_CHUNKS
    rows = HC // nchunks * TILE
    once = pl.Buffered(1)
    unroll = min(SEL_UNROLL, t // TILE - 1)
    return pl.pallas_call(
        functools.partial(_nsa_sel_kernel, nhalf=nsp // 128, nb=t // TILE, unroll=unroll),
        grid=(t // TILE,),
        in_specs=[_smem(),
                  pl.BlockSpec((TILE, HC * DHC), lambda qi: (qi, 0)),
                  pl.BlockSpec((GC, TILE, nsp), lambda qi: (0, qi, 0)),
                  pl.BlockSpec((t, GC * DHC), lambda qi: (0, 0), pipeline_mode=once),
                  pl.BlockSpec((t, GC * DHC), lambda qi: (0, 0), pipeline_mode=once)],
        out_specs=pl.BlockSpec((TILE, HC * DHC), lambda qi: (qi, 0)),
        out_shape=jax.ShapeDtypeStruct((t, HC * DHC), BRANCH_DTYPE),
        scratch_shapes=[pltpu.VMEM((HC, 2, TILE, TILE), F32),
                        pltpu.VMEM((nchunks, rows, 128), F32),
                        pltpu.VMEM((nchunks, rows, 2 * DHC), F32),
                        pltpu.VMEM((2, nchunks, rows, unroll * TILE), F32),
                        pltpu.VMEM((t, DHC), BF16)],
        compiler_params=_params("arbitrary"),
        name="nsa_selected",
    )(tab, q, selbias, k, v)


def _layer_ab(x, tab, w_in, conv_w, conv_b, w_rg, b_rg, w_ig, b_ig, lam, w_out, g_pre, g_post, mlp):
    da = HA * DHA
    segs = [(0, da), (da, da), (da, da), (2 * da, da), (3 * da, LRU_W), (3 * da + LRU_W, LRU_W)]
    q, k32, k, v, xr, gr = _rms_proj(x, g_pre, w_in.astype(BF16), segs, [F32, F32, BF16, BF16, F32, F32])
    o_a = _moba(tab, _moba_route(q, k32), k, v)

    eye = jnp.eye(w_rg.shape[0], dtype=F32)
    blockdiag = lambda wb: jnp.einsum('bij,bc->bicj', wb, eye).reshape(LRU_W, LRU_W).astype(BF16)
    o_b = _rglru(xr, gr, conv_w, conv_b, blockdiag(w_rg), b_rg, blockdiag(w_ig), b_ig, lam)
    wo = w_out.astype(BF16)
    gpre, wup, wdn, gpost, layer = mlp
    return _mix_mlp(o_a, o_b, wo[:da], wo[da:], g_post, x, gpre, wup, wdn, gpost, layer)


def _layer_c(x, tab, w_in, pos_k, pos_v, k_w1, k_w2, v_w1, v_w2, w_out, g_pre, g_post):
    t = x.shape[0]
    dq = HC * DHC
    ckv = GC * DHC
    w = jnp.pad(w_in, ((0, 0), (0, 128 - 3 * HC))).astype(BF16)
    segs = _segments([dq, DHC, DHC, DHC, DHC, ckv, ckv, ckv, ckv, 128])
    q, kc0, kc1, vc0, vc1, ks, vs, kw, vw, gl = _rms_proj(
        x, g_pre, w, segs, [F32, F32, F32, F32, F32, BF16, BF16, BF16, BF16, F32])

    n = t // CMP_STRIDE
    kc = _compress([kc0, kc1], pos_k, k_w1.astype(BF16), k_w2.astype(BF16))
    vc = _compress([vc0, vc1], pos_v, v_w1.astype(BF16), v_w2.astype(BF16))

    ns = t // SEL_BLOCK
    nsp = -(-ns // 128) * 128
    cs = jnp.arange(n)[:, None] * CMP_STRIDE
    ss = jnp.arange(nsp)[None, :] * SEL_BLOCK
    overlap = ((cs < ss + SEL_BLOCK) & (cs + CMP_LEN > ss) & (jnp.arange(n)[:, None] < n - 1)
               & (jnp.arange(nsp)[None, :] < ns)).astype(BF16)
    o_c, selbias, o_w = _nsa_cmp_win(tab, q, kc, vc, overlap, kw, vw, ns)
    o_s = _nsa_sel(tab, q, selbias, ks, vs)
    return _outproj_c(o_c, o_s, o_w, gl, w_out.astype(BF16), g_post, x)


def kernel(x, rel_bias, ab_w_in, ab_conv_w, ab_conv_b, ab_w_rg, ab_b_rg, ab_w_ig, ab_b_ig, ab_lru_lambda, ab_w_out, c_w_in, c_cmp_pos_k, c_cmp_pos_v, c_cmp_k_w1, c_cmp_k_w2, c_cmp_v_w1, c_cmp_v_w2, c_w_out, norm_mix_pre, norm_mix_post, norm_mlp_pre, norm_mlp_post, w_up, w_down):
    batch = x.shape[0]
    depth = norm_mix_pre.shape[0]
    outs = []
    for b in range(batch):
        xb = x[b]
        for layer in range(depth):
            i = layer // 2
            if layer % 2 == 0:
                xb = _layer_ab(xb, rel_bias, ab_w_in[i], ab_conv_w[i], ab_conv_b[i], ab_w_rg[i], ab_b_rg[i],
                               ab_w_ig[i], ab_b_ig[i], ab_lru_lambda[i], ab_w_out[i],
                               norm_mix_pre[layer], norm_mix_post[layer],
                               (norm_mlp_pre[layer], w_up, w_down, norm_mlp_post[layer], layer))
            else:
                xb = _layer_c(xb, rel_bias, c_w_in[i], c_cmp_pos_k[i], c_cmp_pos_v[i], c_cmp_k_w1[i],
                              c_cmp_k_w2[i], c_cmp_v_w1[i], c_cmp_v_w2[i], c_w_out[i],
                              norm_mix_pre[layer], norm_mix_post[layer])
                xb = _mlp(xb, norm_mlp_pre[layer], w_up, w_down, norm_mlp_post[layer], layer)
        outs.append(xb)
    return jnp.stack(outs)
```
